```python
import math
import jax
import jax.numpy as jnp
from jax import lax
import numpy as np

D_MODEL = 4096
BATCH = 4
SEQ = 2048
DEPTH = 2
DEC_BATCH = 128
DEC_SEQ = 4
PAST_LEN = 16384
PAGE_SIZE = 128

N_EVEN = (DEPTH + 1) // 2
N_ODD = DEPTH // 2
GDN_HEADS = 16
GDN_DK = 128
GDN_DV = 128
GDN_CONV = 4
RET_HEADS = 8
RET_DK = 256
RET_DV = 256
ROPE_BASE = 10000.0
CHUNK = 64
POOL_WINDOWS = (2, 4, 8, 16)
POOL_GROUPS = 4
POOL_DG = D_MODEL // POOL_GROUPS
POOL_PAST = max(POOL_WINDOWS) - 1
D_FF = 11008
FFN_CONV = 3
PLE_DIM = 256
EPS = 1e-6

GDN_QK = GDN_HEADS * GDN_DK
GDN_V = GDN_HEADS * GDN_DV
GDN_CONV_CH = 2 * GDN_QK + GDN_V
RET_QK = RET_HEADS * RET_DK
RET_V = RET_HEADS * RET_DV
MIX_WIDTH = GDN_V + RET_V
OFF_A = GDN_CONV_CH
OFF_B = OFF_A + GDN_HEADS
OFF_Z = OFF_B + GDN_HEADS
OFF_RQ = OFF_Z + GDN_V
OFF_RK = OFF_RQ + RET_QK
OFF_RV = OFF_RK + RET_QK
OFF_RG = OFF_RV + RET_V
IN_COLS = OFF_RG + RET_V

kernel_name = 'hybrid_gdn_retention_pool_decoder_step'


def rmsnorm(x, gain):
    xf = x.astype(jnp.float32)
    y = xf * lax.rsqrt(jnp.mean(xf * xf, axis=-1, keepdims=True) + EPS)
    return (y * gain.astype(jnp.float32)).astype(x.dtype)


def l2norm(x):
    return x * lax.rsqrt(jnp.sum(x * x, axis=-1, keepdims=True) + 1e-6)


def causal_dwconv(x_ext, w, length):
    out = x_ext[:, 0:length] * w[0]
    for j in range(1, w.shape[0]):
        out = out + x_ext[:, j:j + length] * w[j]
    return out


def rotary(x, pos):
    half = x.shape[-1] // 2
    inv_freq = ROPE_BASE ** (-jnp.arange(half, dtype=jnp.float32) / half)
    ang = pos.astype(jnp.float32)[:, None] * inv_freq[None, :]
    cos = jnp.cos(ang)[None, :, None, :]
    sin = jnp.sin(ang)[None, :, None, :]
    x1, x2 = x[..., :half], x[..., half:]
    return jnp.concatenate([x1 * cos - x2 * sin, x1 * sin + x2 * cos], axis=-1)


def _to_chunks(t, c, n):
    pad = n * c - t.shape[2]
    t = jnp.pad(t, [(0, 0), (0, 0), (0, pad)] + [(0, 0)] * (t.ndim - 3))
    return t.reshape(t.shape[:2] + (n, c) + t.shape[3:])


def _from_chunks(o, length):
    n, b, h, c, d = o.shape
    return jnp.moveaxis(o, 0, 2).reshape(b, h, n * c, d)[:, :, :length]


def _decay_matrix(gc):
    c = gc.shape[-1]
    causal = jnp.tril(jnp.ones((c, c), dtype=bool))
    diff = gc[..., :, None] - gc[..., None, :]
    return jnp.exp(jnp.where(causal, diff, -jnp.inf))


def gated_delta_chunked(q, k, v, g, beta, s0):
    length = q.shape[2]
    c = min(CHUNK, length)
    n = -(-length // c)
    q, k, v = _to_chunks(q, c, n), _to_chunks(k, c, n), _to_chunks(v, c, n)
    g, beta = _to_chunks(g, c, n), _to_chunks(beta, c, n)
    gc = jnp.cumsum(g, axis=-1)
    decay = _decay_matrix(gc)
    strict = jnp.tril(jnp.ones((c, c), dtype=bool), -1)
    kk = jnp.einsum('bhnid,bhnjd->bhnij', k, k)
    lower = jnp.where(strict, beta[..., :, None] * decay * kk, 0.0) + jnp.eye(c, dtype=q.dtype)
    rhs = jnp.concatenate([beta[..., None] * v, (beta * jnp.exp(gc))[..., None] * k], axis=-1)
    sol = lax.linalg.triangular_solve(lower, rhs, left_side=True, lower=True)
    dv = v.shape[-1]
    w_v, w_k = sol[..., :dv], sol[..., dv:]
    qk = jnp.einsum('bhnid,bhnjd->bhnij', q, k) * decay
    q_dec = q * jnp.exp(gc)[..., None]
    k_end = k * jnp.exp(gc[..., -1:] - gc)[..., None]
    g_end = jnp.exp(gc[..., -1])

    def step(s, xs):
        w_v_n, w_k_n, qk_n, q_dec_n, k_end_n, g_end_n = xs
        u = w_v_n - jnp.einsum('bhck,bhkv->bhcv', w_k_n, s)
        o = jnp.einsum('bhck,bhkv->bhcv', q_dec_n, s) + jnp.einsum('bhij,bhjv->bhiv', qk_n, u)
        s = g_end_n[..., None, None] * s + jnp.einsum('bhck,bhcv->bhkv', k_end_n, u)
        return s, o

    xs = tuple(jnp.moveaxis(t, 2, 0) for t in (w_v, w_k, qk, q_dec, k_end, g_end))
    s, o = lax.scan(step, s0, xs)
    return _from_chunks(o, length), s


def retention_chunked(q, k, v, g, s0):
    length = q.shape[2]
    c = min(CHUNK, length)
    n = -(-length // c)
    q, k, v, g = (_to_chunks(t, c, n) for t in (q, k, v, g))
    gc = jnp.cumsum(g, axis=-1)
    qk = jnp.einsum('bhnid,bhnjd->bhnij', q, k) * _decay_matrix(gc)
    q_dec = q * jnp.exp(gc)[..., None]
    k_end = k * jnp.exp(gc[..., -1:] - gc)[..., None]
    g_end = jnp.exp(gc[..., -1])

    def step(s, xs):
        v_n, qk_n, q_dec_n, k_end_n, g_end_n = xs
        o = jnp.einsum('bhck,bhkv->bhcv', q_dec_n, s) + jnp.einsum('bhij,bhjv->bhiv', qk_n, v_n)
        s = g_end_n[..., None, None] * s + jnp.einsum('bhck,bhcv->bhkv', k_end_n, v_n)
        return s, o

    xs = tuple(jnp.moveaxis(t, 2, 0) for t in (v, qk, q_dec, k_end, g_end))
    s, o = lax.scan(step, s0, xs)
    return _from_chunks(o, length), s


def delta_retention_mixer(h, pos, conv_buf, s_gdn, s_ret, w_in, conv_w, a_log, dt_bias, out_norm, w_out):
    bsz, length, _ = h.shape
    f32 = jnp.float32
    proj = h @ w_in

    def heads(t, nh):
        return t.reshape(bsz, length, nh, -1)

    def bhl(t):
        return jnp.swapaxes(t, 1, 2)

    ext = jnp.concatenate([conv_buf.astype(proj.dtype), proj[..., :GDN_CONV_CH]], axis=1)
    qkv = jax.nn.silu(causal_dwconv(ext, conv_w, length)).astype(f32)
    q = l2norm(heads(qkv[..., :GDN_QK], GDN_HEADS)) * (GDN_DK ** -0.5)
    k = l2norm(heads(qkv[..., GDN_QK:2 * GDN_QK], GDN_HEADS))
    v = heads(qkv[..., 2 * GDN_QK:], GDN_HEADS)
    a = proj[..., OFF_A:OFF_B].astype(f32)
    g = -jnp.exp(a_log.astype(f32)) * jax.nn.softplus(a + dt_bias.astype(f32))
    beta = jax.nn.sigmoid(proj[..., OFF_B:OFF_Z].astype(f32))
    o_a, s_gdn_new = gated_delta_chunked(bhl(q), bhl(k), bhl(v), bhl(g), bhl(beta), s_gdn.astype(f32))
    o_a = bhl(o_a)
    z = heads(proj[..., OFF_Z:OFF_RQ], GDN_HEADS).astype(f32)
    o_a = o_a * lax.rsqrt(jnp.mean(o_a * o_a, axis=-1, keepdims=True) + EPS) * out_norm.astype(f32) * jax.nn.silu(z)

    qr = rotary(heads(proj[..., OFF_RQ:OFF_RK].astype(f32), RET_HEADS), pos)
    kr = rotary(heads(proj[..., OFF_RK:OFF_RV].astype(f32), RET_HEADS), pos) * (RET_DK ** -0.5)
    vr = heads(proj[..., OFF_RV:OFF_RG].astype(f32), RET_HEADS)
    log_gamma = jnp.log1p(-jnp.exp2(-5.0 - jnp.arange(RET_HEADS, dtype=f32)))
    g_r = jnp.broadcast_to(log_gamma[None, :, None], (bsz, RET_HEADS, length))
    o_b, s_ret_new = retention_chunked(bhl(qr), bhl(kr), bhl(vr), g_r, s_ret.astype(f32))
    o_b = bhl(o_b)
    mu = jnp.mean(o_b, axis=-1, keepdims=True)
    var = jnp.mean(jnp.square(o_b - mu), axis=-1, keepdims=True)
    gate_r = heads(proj[..., OFF_RG:IN_COLS], RET_HEADS).astype(f32)
    o_b = (o_b - mu) * lax.rsqrt(var + EPS) * jax.nn.silu(gate_r)

    mixed = jnp.concatenate([o_a.reshape(bsz, length, GDN_V), o_b.reshape(bsz, length, RET_V)], axis=-1)
    y = mixed.astype(h.dtype) @ w_out
    return y, ext[:, -(GDN_CONV - 1):], s_gdn_new, s_ret_new


def pool_mixer(h, pos, buf, w_group, scale):
    bsz, length, _ = h.shape
    ext = jnp.concatenate([buf.astype(h.dtype), h], axis=1)
    csum = jnp.cumsum(ext.astype(jnp.float32), axis=1)
    csum = jnp.concatenate([jnp.zeros_like(csum[:, :1]), csum], axis=1)
    hf = h.astype(jnp.float32)
    groups = []
    for gi, w in enumerate(POOL_WINDOWS):
        sl = slice(gi * POOL_DG, (gi + 1) * POOL_DG)
        hi = csum[:, POOL_PAST + 1:POOL_PAST + 1 + length, sl]
        lo = csum[:, POOL_PAST + 1 - w:POOL_PAST + 1 - w + length, sl]
        cnt = jnp.minimum(w, pos + 1).astype(jnp.float32)[None, :, None]
        groups.append((hi - lo) / cnt - hf[..., sl])
    pooled = jnp.stack(groups, axis=2)
    y = jnp.einsum('blgc,gcd->blgd', pooled, w_group.astype(jnp.float32)).reshape(bsz, length, D_MODEL)
    y = (y * scale.astype(jnp.float32)).astype(h.dtype)
    return y, ext[:, -POOL_PAST:]


def conv_ffn(h, buf, w_up, conv_w, conv_b, w_down):
    length = h.shape[1]
    up = h @ w_up
    ext = jnp.concatenate([buf.astype(up.dtype), up], axis=1)
    c = causal_dwconv(ext, conv_w, length) + conv_b
    gate, val = c[..., :D_FF], c[..., D_FF:]
    return (jax.nn.silu(gate) * val) @ w_down, ext[:, -(FFN_CONV - 1):]


def trunk(x, p, pos, gdn_conv_st, gdn_st, ret_st, pool_st, ffn_conv_st,
          norm_mix, norm_ffn, norm_ple, norm_final, w_in, gdn_conv_w, gdn_a_log, gdn_dt_bias,
          gdn_out_norm, w_out, pool_w, pool_scale, ffn_w_up, ffn_conv_w, ffn_conv_b, ffn_w_down,
          ple_w_gate, ple_w_proj):
    new_gdn_conv, new_gdn, new_ret, new_pool, new_ffn = [], [], [], [], []
    for i in range(DEPTH):
        j = i // 2
        h = rmsnorm(x, norm_mix[i])
        if i % 2 == 0:
            y, c_buf, s_a, s_b = delta_retention_mixer(
                h, pos, gdn_conv_st[j], gdn_st[j], ret_st[j], w_in[j], gdn_conv_w[j],
                gdn_a_log[j], gdn_dt_bias[j], gdn_out_norm[j], w_out[j])
            new_gdn_conv.append(c_buf.astype(gdn_conv_st.dtype))
            new_gdn.append(s_a.astype(gdn_st.dtype))
            new_ret.append(s_b.astype(ret_st.dtype))
        else:
            y, p_buf = pool_mixer(h, pos, pool_st[j], pool_w[j], pool_scale[j])
            new_pool.append(p_buf.astype(pool_st.dtype))
        x = x + y
        h = rmsnorm(x, norm_ffn[i])
        y, f_buf = conv_ffn(h, ffn_conv_st[i], ffn_w_up[i], ffn_conv_w[i], ffn_conv_b[i], ffn_w_down[i])
        new_ffn.append(f_buf.astype(ffn_conv_st.dtype))
        x = x + y
        gate = jax.nn.sigmoid(rmsnorm(x, norm_ple[i]) @ ple_w_gate[i])
        x = x + gate * (p[i] @ ple_w_proj[i])
    out = rmsnorm(x, norm_final)
    return out, (jnp.stack(new_gdn_conv), jnp.stack(new_gdn), jnp.stack(new_ret),
                 jnp.stack(new_pool), jnp.stack(new_ffn))


def setup_inputs(seed: int = 0) -> dict:
    key = jax.random.key(seed)
    ks = jax.random.split(key, 32)
    f32 = jnp.float32

    def nrm(k, shape, scale):
        return scale * jax.random.normal(k, shape, f32)

    def gain(k, shape):
        return 1.0 + nrm(k, shape, 0.02)

    dt = jnp.exp(jax.random.uniform(ks[16], (N_EVEN, GDN_HEADS), f32, math.log(1e-3), math.log(1e-1)))
    return {
        'x_prompt': nrm(ks[0], (BATCH, SEQ, D_MODEL), 1.0),
        'x_sample': nrm(ks[1], (DEC_BATCH, DEC_SEQ, D_MODEL), 1.0),
        'state_gdn_conv': nrm(ks[2], (N_EVEN, DEC_BATCH, GDN_CONV - 1, GDN_CONV_CH), 1.0),
        'state_gdn': nrm(ks[3], (N_EVEN, DEC_BATCH, GDN_HEADS, GDN_DK, GDN_DV), 0.1),
        'state_ret': nrm(ks[4], (N_EVEN, DEC_BATCH, RET_HEADS, RET_DK, RET_DV), 0.1),
        'state_pool': nrm(ks[5], (N_ODD, DEC_BATCH, POOL_PAST, D_MODEL), 1.0),
        'state_ffn_conv': nrm(ks[6], (DEPTH, DEC_BATCH, FFN_CONV - 1, 2 * D_FF), 1.0),
        'p_prompt': nrm(ks[7], (DEPTH, BATCH, SEQ, PLE_DIM), 1.0),
        'p_sample': nrm(ks[8], (DEPTH, DEC_BATCH, DEC_SEQ, PLE_DIM), 1.0),
        'norm_mix': gain(ks[9], (DEPTH, D_MODEL)),
        'norm_ffn': gain(ks[10], (DEPTH, D_MODEL)),
        'norm_ple': gain(ks[11], (DEPTH, D_MODEL)),
        'norm_final': gain(ks[12], (D_MODEL,)),
        'w_in': nrm(ks[13], (N_EVEN, D_MODEL, IN_COLS), D_MODEL ** -0.5),
        'gdn_conv_w': nrm(ks[14], (N_EVEN, GDN_CONV, GDN_CONV_CH), GDN_CONV ** -0.5),
        'gdn_a_log': jnp.log(jax.random.uniform(ks[15], (N_EVEN, GDN_HEADS), f32, 1.0, 16.0)),
        'gdn_dt_bias': dt + jnp.log(-jnp.expm1(-dt)),
        'gdn_out_norm': gain(ks[17], (N_EVEN, GDN_DV)),
        'w_out': nrm(ks[18], (N_EVEN, MIX_WIDTH, D_MODEL), MIX_WIDTH ** -0.5),
        'pool_w': nrm(ks[19], (N_ODD, POOL_GROUPS, POOL_DG, POOL_DG), POOL_DG ** -0.5),
        'pool_scale': gain(ks[20], (N_ODD, D_MODEL)),
        'ffn_w_up': nrm(ks[21], (DEPTH, D_MODEL, 2 * D_FF), D_MODEL ** -0.5),
        'ffn_conv_w': nrm(ks[22], (DEPTH, FFN_CONV, 2 * D_FF), FFN_CONV ** -0.5),
        'ffn_conv_b': nrm(ks[23], (DEPTH, 2 * D_FF), 0.02),
        'ffn_w_down': nrm(ks[24], (DEPTH, D_FF, D_MODEL), D_FF ** -0.5),
        'ple_w_gate': nrm(ks[25], (DEPTH, D_MODEL, D_MODEL), D_MODEL ** -0.5),
        'ple_w_proj': nrm(ks[26], (DEPTH, PLE_DIM, D_MODEL), PLE_DIM ** -0.5),
    }


def reference(x_prompt, x_sample, state_gdn_conv, state_gdn, state_ret, state_pool, state_ffn_conv,
              p_prompt, p_sample, norm_mix, norm_ffn, norm_ple, norm_final, w_in, gdn_conv_w,
              gdn_a_log, gdn_dt_bias, gdn_out_norm, w_out, pool_w, pool_scale, ffn_w_up, ffn_conv_w,
              ffn_conv_b, ffn_w_down, ple_w_gate, ple_w_proj):
    bp, lp, _ = x_prompt.shape
    z_gdn_conv = jnp.zeros((N_EVEN, bp) + state_gdn_conv.shape[2:], state_gdn_conv.dtype)
    z_gdn = jnp.zeros((N_EVEN, bp) + state_gdn.shape[2:], state_gdn.dtype)
    z_ret = jnp.zeros((N_EVEN, bp) + state_ret.shape[2:], state_ret.dtype)
    z_pool = jnp.zeros((N_ODD, bp) + state_pool.shape[2:], state_pool.dtype)
    z_ffn = jnp.zeros((DEPTH, bp) + state_ffn_conv.shape[2:], state_ffn_conv.dtype)
    pos_p = jnp.arange(lp, dtype=jnp.int32)
    pos_s = PAST_LEN + jnp.arange(x_sample.shape[1], dtype=jnp.int32)

    y_prompt, (gc_p, gd_p, rt_p, pl_p, ff_p) = trunk(
        x_prompt, p_prompt, pos_p, z_gdn_conv, z_gdn, z_ret, z_pool, z_ffn,
        norm_mix, norm_ffn, norm_ple, norm_final, w_in, gdn_conv_w, gdn_a_log, gdn_dt_bias,
        gdn_out_norm, w_out, pool_w, pool_scale, ffn_w_up, ffn_conv_w, ffn_conv_b, ffn_w_down,
        ple_w_gate, ple_w_proj)
    y_sample, (gc_s, gd_s, rt_s, pl_s, ff_s) = trunk(
        x_sample, p_sample, pos_s, state_gdn_conv, state_gdn, state_ret, state_pool, state_ffn_conv,
        norm_mix, norm_ffn, norm_ple, norm_final, w_in, gdn_conv_w, gdn_a_log, gdn_dt_bias,
        gdn_out_norm, w_out, pool_w, pool_scale, ffn_w_up, ffn_conv_w, ffn_conv_b, ffn_w_down,
        ple_w_gate, ple_w_proj)
    return (y_prompt, y_sample, gc_p, gd_p, rt_p, pl_p, ff_p, gc_s, gd_s, rt_s, pl_s, ff_s)
```

```python
import functools
import math

import numpy as np
import jax
import jax.numpy as jnp
from jax import lax
from jax.experimental import pallas as pl
from jax.experimental.pallas import tpu as pltpu

F32 = jnp.float32
BF16 = jnp.bfloat16
HIGHEST = lax.Precision.HIGHEST

EPS = 1e-6
CHUNK = 64
PAST_LEN = 16384
ROPE_BASE = 10000.0
POOL_WINDOWS = (2, 4, 8, 16)
POOL_HIST = 16
LANES = 128
SUBLANES = 8
VMEM_LIMIT = 56 * 1024 * 1024


def _pick(n, pref, align=LANES):
    if n <= pref:
        return n
    t = (pref // align) * align
    while t >= align:
        if n % t == 0:
            return t
        t -= align
    return n


def _cparams(*sem):
    return pltpu.CompilerParams(dimension_semantics=sem, vmem_limit_bytes=VMEM_LIMIT)


def _silu(x):
    return x * (1.0 / (1.0 + jnp.exp(-x)))


def _sigmoid(x):
    return 1.0 / (1.0 + jnp.exp(-x))


def _dot(a, b, **kw):
    return jnp.dot(a, b, preferred_element_type=F32, **kw)


def _dot_nt(a, b, **kw):
    return lax.dot_general(a, b, (((1,), (1,)), ((), ())), preferred_element_type=F32, **kw)


def _dot_tn(a, b, **kw):
    return lax.dot_general(a, b, (((0,), (0,)), ((), ())), preferred_element_type=F32, **kw)


def _rmsnorm_kernel(x_ref, g_ref, *o_refs):
    x = x_ref[...]
    y = x * lax.rsqrt(jnp.mean(x * x, axis=-1, keepdims=True) + EPS) * g_ref[...]
    for o_ref in o_refs:
        o_ref[...] = y.astype(o_ref.dtype)


def _rmsnorm(x, gain, dtypes):
    m, d = x.shape
    tm = _pick(m, 256, SUBLANES)
    outs = pl.pallas_call(
        _rmsnorm_kernel,
        grid=(m // tm,),
        in_specs=[pl.BlockSpec((tm, d), lambda i: (i, 0)), pl.BlockSpec((1, d), lambda i: (0, 0))],
        out_specs=[pl.BlockSpec((tm, d), lambda i: (i, 0)) for _ in dtypes],
        out_shape=[jax.ShapeDtypeStruct((m, d), dt) for dt in dtypes],
        compiler_params=_cparams("parallel"),
        name="rmsnorm",
    )(x, gain.reshape(1, d))
    return outs


def _mm_kernel(*refs, nk, epilogue):
    x_ref, w_ref = refs[0], refs[1]
    pos = 2
    res_ref = p_ref = wp_ref = None
    if epilogue in ("res", "ple"):
        res_ref = refs[pos]
        pos += 1
    if epilogue == "ple":
        p_ref, wp_ref = refs[pos], refs[pos + 1]
        pos += 2
    o_ref = refs[pos]
    acc_ref = refs[pos + 1] if nk > 1 else None

    def finish(acc):
        if epilogue == "res":
            acc = res_ref[...] + acc
        elif epilogue == "ple":
            acc = res_ref[...] + _sigmoid(acc) * _dot(p_ref[...], wp_ref[...])
        o_ref[...] = acc.astype(o_ref.dtype)

    part = _dot(x_ref[...], w_ref[...])
    if nk == 1:
        finish(part)
    else:
        k = pl.program_id(2)

        @pl.when(k == 0)
        def _():
            acc_ref[...] = part

        @pl.when(jnp.logical_and(k > 0, k < nk - 1))
        def _():
            acc_ref[...] += part

        @pl.when(k == nk - 1)
        def _():
            finish(acc_ref[...] + part)


def _matmul(x, w, *, res=None, p=None, wp=None, tm=512, tn=512, nk=1, out_dtype=F32, name="matmul"):
    m, kdim = x.shape
    n = w.shape[1]
    tm = _pick(m, tm, SUBLANES)
    tn = _pick(n, tn)
    assert kdim % nk == 0 and (nk == 1 or (kdim // nk) % LANES == 0)
    tk = kdim // nk
    epilogue = "none" if res is None else ("res" if p is None else "ple")
    in_specs = [pl.BlockSpec((tm, tk), lambda j, i, k: (i, k)),
                pl.BlockSpec((tk, tn), lambda j, i, k: (k, j))]
    args = [x, w]
    if res is not None:
        in_specs.append(pl.BlockSpec((tm, tn), lambda j, i, k: (i, j)))
        args.append(res)
    if p is not None:
        pd = p.shape[1]
        in_specs += [pl.BlockSpec((tm, pd), lambda j, i, k: (i, 0)),
                     pl.BlockSpec((pd, tn), lambda j, i, k: (0, j))]
        args += [p, wp]
    scratch = [pltpu.VMEM((tm, tn), F32)] if nk > 1 else []
    return pl.pallas_call(
        functools.partial(_mm_kernel, nk=nk, epilogue=epilogue),
        grid=(n // tn, m // tm, nk),
        in_specs=in_specs,
        out_specs=pl.BlockSpec((tm, tn), lambda j, i, k: (i, j)),
        out_shape=jax.ShapeDtypeStruct((m, n), out_dtype),
        scratch_shapes=scratch,
        compiler_params=_cparams("parallel", "parallel", "arbitrary"),
        name=name,
    )(*args)


def _ffn_act_kernel(g_ref, v_ref, hg_ref, hv_ref, wg_ref, wv_ref, bg_ref, bv_ref, o_ref,
                    eg_ref, ev_ref, *, taps, stride, tm, off):
    hist = (taps - 1) * stride

    def conv(cur_ref, h_ref, w_ref, b_ref, e_ref):
        e_ref[off - hist:off, :] = h_ref[...]
        e_ref[off:off + tm, :] = cur_ref[...]
        acc = b_ref[...] + w_ref[0:1, :] * e_ref[off - hist:off - hist + tm, :]
        for j in range(1, taps):
            s = off - hist + j * stride
            acc = acc + w_ref[j:j + 1, :] * e_ref[s:s + tm, :]
        return acc

    gate = conv(g_ref, hg_ref, wg_ref, bg_ref, eg_ref)
    val = conv(v_ref, hv_ref, wv_ref, bv_ref, ev_ref)
    o_ref[...] = (_silu(gate) * val).astype(o_ref.dtype)


def _ffn_act(up, hist, conv_w, conv_b, *, tm, stride):
    t, f2 = up.shape
    f = f2 // 2
    taps = conv_w.shape[0]
    hr = (taps - 1) * stride
    off = -(-hr // SUBLANES) * SUBLANES
    tc = _pick(f, 512)
    nc = f // tc
    kern = functools.partial(_ffn_act_kernel, taps=taps, stride=stride, tm=tm, off=off)
    return pl.pallas_call(
        kern,
        grid=(t // tm, nc),
        in_specs=[
            pl.BlockSpec((tm, tc), lambda i, j: (i, j)),
            pl.BlockSpec((tm, tc), lambda i, j: (i, j + nc)),
            pl.BlockSpec((None, hr, tc), lambda i, j: (i, 0, j)),
            pl.BlockSpec((None, hr, tc), lambda i, j: (i, 0, j + nc)),
            pl.BlockSpec((taps, tc), lambda i, j: (0, j)),
            pl.BlockSpec((taps, tc), lambda i, j: (0, j + nc)),
            pl.BlockSpec((1, tc), lambda i, j: (0, j)),
            pl.BlockSpec((1, tc), lambda i, j: (0, j + nc)),
        ],
        out_specs=pl.BlockSpec((tm, tc), lambda i, j: (i, j)),
        out_shape=jax.ShapeDtypeStruct((t, f), BF16),
        scratch_shapes=[pltpu.VMEM((off + tm, tc), F32), pltpu.VMEM((off + tm, tc), F32)],
        compiler_params=_cparams("parallel", "parallel"),
        name="ffn_conv_act",
    )(up, up, hist, hist, conv_w, conv_w, conv_b.reshape(1, f2), conv_b.reshape(1, f2))


def _pool_kernel(h_ref, hist_ref, cnt_ref, w_ref, sc_ref, x_ref, o_ref, e_ref, acc_ref, *, stride, tm):
    g = pl.program_id(1)
    off = POOL_HIST * stride
    window = lax.shift_left(jnp.int32(2), g)
    e_ref[0:off, :] = hist_ref[...]
    e_ref[off:off + tm, :] = h_ref[...]
    acc_ref[...] = e_ref[off:off + tm, :] + e_ref[off - stride:off - stride + tm, :]
    for j in range(2, max(POOL_WINDOWS)):
        @pl.when(j < window)
        def _(j=j):
            acc_ref[...] += e_ref[off - j * stride:off - j * stride + tm, :]
    pooled = acc_ref[...] / cnt_ref[...] - h_ref[...]
    y = _dot(pooled.astype(BF16), w_ref[...]) * sc_ref[...]
    o_ref[...] = x_ref[...] + y


def _pool_mixer(h, hist, cnt, pool_w, scale, x, *, tm, stride):
    t, d = h.shape
    g = pool_w.shape[0]
    dg = d // g
    hr = POOL_HIST * stride
    kern = functools.partial(_pool_kernel, stride=stride, tm=tm)
    return pl.pallas_call(
        kern,
        grid=(t // tm, g),
        in_specs=[
            pl.BlockSpec((tm, dg), lambda i, j: (i, j)),
            pl.BlockSpec((None, hr, dg), lambda i, j: (i, 0, j)),
            pl.BlockSpec((None, tm, 1), lambda i, j: (j, i, 0)),
            pl.BlockSpec((None, dg, dg), lambda i, j: (j, 0, 0)),
            pl.BlockSpec((1, dg), lambda i, j: (0, j)),
            pl.BlockSpec((tm, dg), lambda i, j: (i, j)),
        ],
        out_specs=pl.BlockSpec((tm, dg), lambda i, j: (i, j)),
        out_shape=jax.ShapeDtypeStruct((t, d), F32),
        scratch_shapes=[pltpu.VMEM((hr + tm, dg), F32), pltpu.VMEM((tm, dg), F32)],
        compiler_params=_cparams("parallel", "arbitrary"),
        name="pool_mixer",
    )(h, hist, cnt, pool_w, scale.reshape(1, d), x)


def _gdn_kernel(pc_ref, pa_ref, pb_ref, z_ref, cb_ref, s0_ref, cw_ref, alog_ref, dtb_ref, on_ref,
                ex_ref, o_ref, sn_ref, ext_s, qkv_s, gcb_s, bb_s, gct_s, s_s,
                *, heads, dk, dv, chunk, valid, nchunks, taps):
    c = pl.program_id(1)
    off = SUBLANES
    qk_w = heads * dk

    @pl.when(c == 0)
    def _():
        ext_s[0:off, :] = jnp.zeros((off, ext_s.shape[1]), F32)
        ext_s[off - (taps - 1):off, :] = cb_ref[...]
        s_s[...] = s0_ref[...]

    ext_s[off:off + chunk, :] = pc_ref[...]
    conv = cw_ref[0:1, :] * ext_s[off - taps + 1:off - taps + 1 + chunk, :]
    for j in range(1, taps):
        s = off - taps + 1 + j
        conv = conv + cw_ref[j:j + 1, :] * ext_s[s:s + chunk, :]
    rows = lax.broadcasted_iota(jnp.int32, (chunk, 1), 0)
    live = rows < valid
    qkv_s[...] = jnp.where(live, _silu(conv), 0.0)
    if nchunks > 1:
        ext_s[0:off, :] = ext_s[chunk:chunk + off, :]

    a = pa_ref[...] + dtb_ref[...]
    softplus = jnp.maximum(a, 0.0) + jnp.log1p(jnp.exp(-jnp.abs(a)))
    glog = jnp.where(live, -jnp.exp(alog_ref[...]) * softplus, 0.0)
    beta = jnp.where(live, _sigmoid(pb_ref[...]), 0.0)
    ri = lax.broadcasted_iota(jnp.int32, (chunk, chunk), 0)
    ci = lax.broadcasted_iota(jnp.int32, (chunk, chunk), 1)
    tri = (ci <= ri).astype(F32)
    gc = _dot(tri, glog, precision=HIGHEST)
    gcb_s[...] = _dot(gc, ex_ref[...], precision=HIGHEST)
    bb_s[...] = _dot(beta, ex_ref[...], precision=HIGHEST)
    gct_s[...] = gc.T
    causal = ci <= ri
    strict = ci < ri

    def head(h, carry):
        lo = pl.multiple_of(h * dk, dk)
        q = qkv_s[:, pl.ds(lo, dk)]
        k = qkv_s[:, pl.ds(qk_w + lo, dk)]
        v = qkv_s[:, pl.ds(2 * qk_w + pl.multiple_of(h * dv, dv), dv)]
        q = q * lax.rsqrt(jnp.sum(q * q, axis=-1, keepdims=True) + 1e-6) * (dk ** -0.5)
        k = k * lax.rsqrt(jnp.sum(k * k, axis=-1, keepdims=True) + 1e-6)
        gcb = gcb_s[:, pl.ds(lo, dk)]
        bb = bb_s[:, pl.ds(lo, dk)]
        glast = gcb_s[chunk - 1:chunk, pl.ds(lo, dk)]
        egc = jnp.exp(gcb)
        kend_scale = jnp.exp(glast - gcb)
        g_end = jnp.exp(glast)
        diff = gcb[:, :chunk] - gct_s[pl.ds(h, 1), :]
        decay = jnp.where(causal, jnp.exp(jnp.where(causal, diff, 0.0)), 0.0)
        qkk = _dot_nt(jnp.concatenate([q, k], axis=0), k, precision=HIGHEST)
        qk = qkk[:chunk] * decay
        amat = jnp.where(strict, bb[:, :chunk] * decay * qkk[chunk:], 0.0)
        sol = jnp.concatenate([bb * v, bb * egc * k], axis=1)
        pw = -amat
        steps = max(1, int(math.ceil(math.log2(chunk))))
        for it in range(steps):
            sol = sol + _dot(pw, sol, precision=HIGHEST)
            if it < steps - 1:
                pw = _dot(pw, pw, precision=HIGHEST)
        w_v, w_k = sol[:, :dv], sol[:, dv:]
        s = s_s[h]
        s_bf = s.astype(BF16)
        u = w_v - _dot(w_k.astype(BF16), s_bf)
        u_bf = u.astype(BF16)
        o = _dot((q * egc).astype(BF16), s_bf) + _dot(qk.astype(BF16), u_bf)
        s_s[h] = g_end * s + _dot_tn((k * kend_scale).astype(BF16), u_bf)
        zh = z_ref[:, pl.ds(pl.multiple_of(h * dv, dv), dv)]
        o = o * lax.rsqrt(jnp.mean(o * o, axis=-1, keepdims=True) + EPS) * on_ref[...] * _silu(zh)
        o_ref[:, pl.ds(pl.multiple_of(h * dv, dv), dv)] = o.astype(o_ref.dtype)
        return carry

    lax.fori_loop(0, heads, head, 0)

    @pl.when(c == nchunks - 1)
    def _():
        sn_ref[...] = s_s[...]


def _gdn(pc, pa, pb, z, conv_buf, s0, conv_w, a_log, dt_bias, out_norm, *, valid):
    b, lp, cch = pc.shape
    _, heads, dk, dv = s0.shape
    taps = conv_w.shape[0]
    nchunks = lp // CHUNK
    assert dk == LANES and dv == LANES and heads <= LANES
    expand = np.zeros((LANES, heads * dk), np.float32)
    for h in range(heads):
        expand[h, h * dk:(h + 1) * dk] = 1.0
    lane_pad = lambda t: jnp.pad(t.reshape(1, heads).astype(F32), ((0, 0), (0, LANES - heads)))
    kern = functools.partial(_gdn_kernel, heads=heads, dk=dk, dv=dv, chunk=CHUNK, valid=valid,
                             nchunks=nchunks, taps=taps)
    row = lambda w: pl.BlockSpec((None, CHUNK, w), lambda i, c: (i, c, 0))
    full = lambda shape: pl.BlockSpec(shape, lambda i, c: (0,) * len(shape))
    return pl.pallas_call(
        kern,
        grid=(b, nchunks),
        in_specs=[
            row(cch), row(LANES), row(LANES), row(heads * dv),
            pl.BlockSpec((None, taps - 1, cch), lambda i, c: (i, 0, 0)),
            pl.BlockSpec((None, heads, dk, dv), lambda i, c: (i, 0, 0, 0)),
            full((taps, cch)), full((1, LANES)), full((1, LANES)), full((1, dv)),
            full((LANES, heads * dk)),
        ],
        out_specs=[row(heads * dv), pl.BlockSpec((None, heads, dk, dv), lambda i, c: (i, 0, 0, 0))],
        out_shape=[jax.ShapeDtypeStruct((b, lp, heads * dv), BF16),
                   jax.ShapeDtypeStruct((b, heads, dk, dv), F32)],
        scratch_shapes=[
            pltpu.VMEM((SUBLANES + CHUNK, cch), F32),
            pltpu.VMEM((CHUNK, cch), F32),
            pltpu.VMEM((CHUNK, heads * dk), F32),
            pltpu.VMEM((CHUNK, heads * dk), F32),
            pltpu.VMEM((LANES, CHUNK), F32),
            pltpu.VMEM((heads, dk, dv), F32),
        ],
        compiler_params=_cparams("parallel", "arbitrary"),
        name="gated_deltanet",
    )(pc, pa, pb, z, conv_buf, s0, conv_w, lane_pad(a_log), lane_pad(dt_bias),
      out_norm.reshape(1, dv).astype(F32), jnp.asarray(expand))


def _ret_kernel(q_ref, k_ref, v_ref, g_ref, cos_ref, sin_ref, s0_ref, dec_ref, qs_ref, ks_ref, ge_ref,
                o_ref, sn_ref, s_s, *, heads, dk, dv, chunk, valid, nchunks):
    c = pl.program_id(1)
    half = dk // 2

    @pl.when(c == 0)
    def _():
        s_s[...] = s0_ref[...]

    cos = cos_ref[...]
    sin = sin_ref[...]
    live = lax.broadcasted_iota(jnp.int32, (chunk, 1), 0) < valid

    def rot(x):
        x1, x2 = x[:, :half], x[:, half:]
        return jnp.concatenate([x1 * cos - x2 * sin, x1 * sin + x2 * cos], axis=1)

    def head(h, carry):
        lo = pl.multiple_of(h * dk, dk)
        lov = pl.multiple_of(h * dv, dv)
        q = rot(q_ref[:, pl.ds(lo, dk)])
        k = jnp.where(live, rot(k_ref[:, pl.ds(lo, dk)]) * (dk ** -0.5), 0.0)
        v = jnp.where(live, v_ref[:, pl.ds(lov, dv)], 0.0)
        q_bf, k_bf, v_bf = q.astype(BF16), k.astype(BF16), v.astype(BF16)
        qk = _dot_nt(q_bf, k_bf) * dec_ref[h]
        s = s_s[h]
        o = _dot((q * qs_ref[:, pl.ds(lo, dk)]).astype(BF16), s.astype(BF16)) + _dot(qk.astype(BF16), v_bf)
        s_s[h] = ge_ref[h] * s + _dot_tn((k * ks_ref[:, pl.ds(lo, dk)]).astype(BF16), v_bf)
        mu = jnp.mean(o, axis=-1, keepdims=True)
        var = jnp.mean(jnp.square(o - mu), axis=-1, keepdims=True)
        o = (o - mu) * lax.rsqrt(var + EPS) * _silu(g_ref[:, pl.ds(lov, dv)])
        o_ref[:, pl.ds(lov, dv)] = o.astype(o_ref.dtype)
        return carry

    lax.fori_loop(0, heads, head, 0)

    @pl.when(c == nchunks - 1)
    def _():
        sn_ref[...] = s_s[...]


def _retention(pr, s0, *, valid, pos0):
    b, lp, _ = pr.shape
    _, heads, dk, dv = s0.shape
    nchunks = lp // CHUNK
    half = dk // 2
    inv_freq = ROPE_BASE ** (-np.arange(half, dtype=np.float64) / half)
    ang = (pos0 + np.arange(lp, dtype=np.float64))[:, None] * inv_freq[None, :]
    cos, sin = np.cos(ang).astype(np.float32), np.sin(ang).astype(np.float32)
    log_gamma = np.log1p(-np.exp2(-5.0 - np.arange(heads, dtype=np.float64)))
    steps = np.minimum(np.arange(CHUNK) + 1, valid).astype(np.float64)
    gc = log_gamma[:, None] * steps[None, :]
    diff = gc[:, :, None] - gc[:, None, :]
    causal = np.tril(np.ones((CHUNK, CHUNK), bool))
    dec = np.where(causal[None], np.exp(np.where(causal[None], diff, 0.0)), 0.0).astype(np.float32)
    qs = np.repeat(np.exp(gc).T, dk, axis=1).astype(np.float32)
    ks = np.repeat(np.exp(gc[:, -1:] - gc).T, dk, axis=1).astype(np.float32)
    ge = np.broadcast_to(np.exp(gc[:, -1])[:, None, None], (heads, 1, dv)).astype(np.float32)
    kern = functools.partial(_ret_kernel, heads=heads, dk=dk, dv=dv, chunk=CHUNK, valid=valid,
                             nchunks=nchunks)
    hw = heads * dk
    col = lambda j: pl.BlockSpec((None, CHUNK, hw), lambda i, c, j=j: (i, c, j))
    full = lambda shape: pl.BlockSpec(shape, lambda i, c: (0,) * len(shape))
    return pl.pallas_call(
        kern,
        grid=(b, nchunks),
        in_specs=[
            col(0), col(1), col(2), col(3),
            pl.BlockSpec((CHUNK, half), lambda i, c: (c, 0)),
            pl.BlockSpec((CHUNK, half), lambda i, c: (c, 0)),
            pl.BlockSpec((None, heads, dk, dv), lambda i, c: (i, 0, 0, 0)),
            full((heads, CHUNK, CHUNK)), full((CHUNK, hw)), full((CHUNK, hw)), full((heads, 1, dv)),
        ],
        out_specs=[pl.BlockSpec((None, CHUNK, heads * dv), lambda i, c: (i, c, 0)),
                   pl.BlockSpec((None, heads, dk, dv), lambda i, c: (i, 0, 0, 0))],
        out_shape=[jax.ShapeDtypeStruct((b, lp, heads * dv), BF16),
                   jax.ShapeDtypeStruct((b, heads, dk, dv), F32)],
        scratch_shapes=[pltpu.VMEM((heads, dk, dv), F32)],
        compiler_params=_cparams("parallel", "arbitrary"),
        name="retention",
    )(pr, pr, pr, pr, jnp.asarray(cos), jnp.asarray(sin), s0, jnp.asarray(dec), jnp.asarray(qs),
      jnp.asarray(ks), jnp.asarray(ge))


def _prep_weights(w_in, w_out, pool_w, ffn_w_up, ffn_w_down, ple_w_gate, ple_w_proj, dims):
    gdn_c, heads_a, gdn_v, ret_w = dims
    off_a = gdn_c
    off_b = off_a + heads_a
    off_z = off_b + heads_a
    off_r = off_z + gdn_v
    w_in_p = []
    for w in w_in:
        d = w.shape[0]
        zpad = jnp.zeros((d, LANES - heads_a), w.dtype)
        w_in_p.append(jnp.concatenate(
            [w[:, :off_a], w[:, off_a:off_b], zpad, w[:, off_b:off_z], zpad, w[:, off_z:off_r], w[:, off_r:]],
            axis=1).astype(BF16))
    cast = lambda t: t.astype(BF16)
    return (w_in_p, cast(w_out), cast(pool_w), cast(ffn_w_up), cast(ffn_w_down), cast(ple_w_gate),
            cast(ple_w_proj))


def _trunk(x, p, st, wts, prm, *, position_major, pos0):
    (w_in_p, w_out, pool_w, w_up, w_down, w_gate, w_proj) = wts
    (norm_mix, norm_ffn, norm_ple, norm_final, gdn_conv_w, gdn_a_log, gdn_dt_bias, gdn_out_norm,
     pool_scale, ffn_conv_w, ffn_conv_b) = prm
    gdn_conv_st, gdn_st, ret_st, pool_st, ffn_st = st
    b, l, d = x.shape
    depth = p.shape[0]
    t = b * l
    _, _, heads_a, dk_a, dv_a = gdn_st.shape
    _, _, heads_r, dk_r, dv_r = ret_st.shape
    gdn_c = gdn_conv_st.shape[-1]
    gdn_v = heads_a * dv_a
    ret_w = heads_r * dk_r
    gtaps = gdn_conv_w.shape[1]
    ftaps = ffn_conv_w.shape[1]
    pool_past = pool_st.shape[2]
    assert l >= gtaps - 1 and l >= ftaps - 1 and pool_past == POOL_HIST - 1

    if position_major:
        to_rows = lambda a: jnp.swapaxes(a, 0, 1).reshape((t,) + a.shape[2:])
        from_rows = lambda a: jnp.swapaxes(a.reshape((l, b) + a.shape[1:]), 0, 1)
        tm, stride, ntile = t, b, 1
    else:
        to_rows = lambda a: a.reshape((t,) + a.shape[2:])
        from_rows = lambda a: a.reshape((b, l) + a.shape[1:])
        tm = _pick(l, 512, SUBLANES)
        stride, ntile = 1, t // tm
    tiles_per_seq = l // tm if not position_major else 1

    def history(rows_bl, state, nrows):
        if position_major:
            return jnp.swapaxes(state[:, -nrows:], 0, 1).reshape(1, nrows * b, state.shape[-1])
        tl = rows_bl.reshape(b, tiles_per_seq, tm, rows_bl.shape[-1])[:, :, tm - nrows:, :]
        prev = jnp.concatenate([state[:, None, -nrows:, :], tl[:, :-1]], axis=1)
        return prev.reshape(ntile, nrows, rows_bl.shape[-1])

    xr = to_rows(x)
    new_gdn_conv, new_gdn, new_ret, new_pool, new_ffn = [], [], [], [], []
    for i in range(depth):
        j = i // 2
        if i % 2 == 0:
            (h,) = _rmsnorm(xr, norm_mix[i], [BF16])
            proj = _matmul(h, w_in_p[j], tn=1024, name="in_proj")
            pb_l = from_rows(proj)
            o_a = gdn_c
            pc, pa, pb = pb_l[..., :o_a], pb_l[..., o_a:o_a + LANES], pb_l[..., o_a + LANES:o_a + 2 * LANES]
            o_z = o_a + 2 * LANES
            z = pb_l[..., o_z:o_z + gdn_v]
            pr = pb_l[..., o_z + gdn_v:]
            lp = -(-l // CHUNK) * CHUNK
            valid = CHUNK if l % CHUNK == 0 else l
            assert l % CHUNK == 0 or l < CHUNK
            padl = lambda a: jnp.pad(a, ((0, 0), (0, lp - l), (0, 0))) if lp != l else a
            o_gdn, s_a = _gdn(padl(pc), padl(pa), padl(pb), padl(z), gdn_conv_st[j], gdn_st[j],
                              gdn_conv_w[j], gdn_a_log[j], gdn_dt_bias[j], gdn_out_norm[j], valid=valid)
            o_ret, s_b = _retention(padl(pr), ret_st[j], valid=valid, pos0=pos0)
            mixed = to_rows(jnp.concatenate([o_gdn[:, :l], o_ret[:, :l]], axis=-1))
            xr = _matmul(mixed, w_out[j], res=xr, tn=1024, name="out_proj")
            new_gdn_conv.append(jnp.concatenate([gdn_conv_st[j], pc], axis=1)[:, -(gtaps - 1):])
            new_gdn.append(s_a)
            new_ret.append(s_b)
        else:
            h32, = _rmsnorm(xr, norm_mix[i], [F32])
            h_bl = from_rows(h32)
            zero_row = jnp.zeros((b, 1, d), F32)
            hist = history(h_bl, jnp.concatenate([zero_row, pool_st[j]], axis=1), POOL_HIST)
            pos = pos0 + jnp.arange(l, dtype=jnp.int32)
            cnt = jnp.stack([jnp.minimum(w, pos + 1).astype(F32) for w in POOL_WINDOWS])
            cnt = to_rows(jnp.broadcast_to(cnt.T[None, :, :], (b, l, len(POOL_WINDOWS))))
            cnt = cnt.T[:, :, None]
            xr = _pool_mixer(h32, hist, cnt, pool_w[j], pool_scale[j], xr, tm=tm, stride=stride)
            new_pool.append(jnp.concatenate([pool_st[j], h_bl], axis=1)[:, -pool_past:])
        (h,) = _rmsnorm(xr, norm_ffn[i], [BF16])
        up = _matmul(h, w_up[i], tn=512, name="ffn_up")
        up_bl = from_rows(up)
        act = _ffn_act(up, history(up_bl, ffn_st[i], ftaps - 1), ffn_conv_w[i], ffn_conv_b[i],
                       tm=tm, stride=stride)
        nk = 2 if (act.shape[1] % (2 * LANES) == 0) else 1
        xr = _matmul(act, w_down[i], res=xr, tn=512, nk=nk, name="ffn_down")
        new_ffn.append(jnp.concatenate([ffn_st[i], up_bl], axis=1)[:, -(ftaps - 1):])
        (h,) = _rmsnorm(xr, norm_ple[i], [BF16])
        xr = _matmul(h, w_gate[i], res=xr, p=to_rows(p[i]).astype(BF16), wp=w_proj[i], tn=1024, name="ple")
    (out,) = _rmsnorm(xr, norm_final, [F32])
    return from_rows(out), (jnp.stack(new_gdn_conv), jnp.stack(new_gdn), jnp.stack(new_ret),
                            jnp.stack(new_pool), jnp.stack(new_ffn))


def kernel(x_prompt, x_sample, state_gdn_conv, state_gdn, state_ret, state_pool, state_ffn_conv, p_prompt, p_sample, norm_mix, norm_ffn, norm_ple, norm_final, w_in, gdn_conv_w, gdn_a_log, gdn_dt_bias, gdn_out_norm, w_out, pool_w, pool_scale, ffn_w_up, ffn_conv_w, ffn_conv_b, ffn_w_down, ple_w_gate, ple_w_proj):
    bp = x_prompt.shape[0]
    heads_a, dv_a = state_gdn.shape[2], state_gdn.shape[4]
    heads_r, dk_r = state_ret.shape[2], state_ret.shape[3]
    dims = (state_gdn_conv.shape[-1], heads_a, heads_a * dv_a, heads_r * dk_r)
    wts = _prep_weights(w_in, w_out, pool_w, ffn_w_up, ffn_w_down, ple_w_gate, ple_w_proj, dims)
    prm = (norm_mix, norm_ffn, norm_ple, norm_final, gdn_conv_w, gdn_a_log, gdn_dt_bias, gdn_out_norm,
           pool_scale, ffn_conv_w, ffn_conv_b)
    states = (state_gdn_conv, state_gdn, state_ret, state_pool, state_ffn_conv)
    zero_states = tuple(jnp.zeros((s.shape[0], bp) + s.shape[2:], s.dtype) for s in states)
    y_p, st_p = _trunk(x_prompt, p_prompt, zero_states, wts, prm, position_major=False, pos0=0)
    y_s, st_s = _trunk(x_sample, p_sample, states, wts, prm, position_major=True, pos0=PAST_LEN)
    return (y_p, y_s) + st_p + st_s
```

```python
import functools
import math

import numpy as np
import jax
import jax.numpy as jnp
from jax import lax
from jax.experimental import pallas as pl
from jax.experimental.pallas import tpu as pltpu

F32 = jnp.float32
BF16 = jnp.bfloat16
HIGHEST = lax.Precision.HIGHEST

EPS = 1e-6
CHUNK = 64
PAST_LEN = 16384
ROPE_BASE = 10000.0
POOL_WINDOWS = (2, 4, 8, 16)
POOL_HIST = 16
LANES = 128
SUBLANES = 8
VMEM_LIMIT = 56 * 1024 * 1024
BASE_BLOCK = 8


def _pick(n, pref, align=LANES):
    if n <= pref:
        return n
    t = (pref // align) * align
    while t >= align:
        if n % t == 0:
            return t
        t -= align
    return n


def _cparams(*sem):
    return pltpu.CompilerParams(dimension_semantics=sem, vmem_limit_bytes=VMEM_LIMIT)


def _silu(x):
    return x * (1.0 / (1.0 + jnp.exp(-x)))


def _sigmoid(x):
    return 1.0 / (1.0 + jnp.exp(-x))


def _dot(a, b, **kw):
    return jnp.dot(a, b, preferred_element_type=F32, **kw)


def _dot_nt(a, b, **kw):
    return lax.dot_general(a, b, (((1,), (1,)), ((), ())), preferred_element_type=F32, **kw)


def _dot_tn(a, b, **kw):
    return lax.dot_general(a, b, (((0,), (0,)), ((), ())), preferred_element_type=F32, **kw)


def _mm(a, b):
    return _dot(a.astype(BF16), b.astype(BF16))


def _expand(x, e_bf, passes):
    acc = None
    for _ in range(passes):
        piece = x.astype(BF16)
        term = _dot(piece, e_bf)
        acc = term if acc is None else acc + term
        x = x - piece.astype(F32)
    return acc


def _rmsnorm_kernel(x_ref, g_ref, *o_refs):
    x = x_ref[...]
    y = x * lax.rsqrt(jnp.mean(x * x, axis=-1, keepdims=True) + EPS) * g_ref[...]
    for o_ref in o_refs:
        o_ref[...] = y.astype(o_ref.dtype)


def _rmsnorm(x, gain, dtypes):
    m, d = x.shape
    tm = _pick(m, 256, SUBLANES)
    return pl.pallas_call(
        _rmsnorm_kernel,
        grid=(m // tm,),
        in_specs=[pl.BlockSpec((tm, d), lambda i: (i, 0)), pl.BlockSpec((1, d), lambda i: (0, 0))],
        out_specs=[pl.BlockSpec((tm, d), lambda i: (i, 0)) for _ in dtypes],
        out_shape=[jax.ShapeDtypeStruct((m, d), dt) for dt in dtypes],
        compiler_params=_cparams("parallel"),
        name="rmsnorm",
    )(x, gain.reshape(1, d))


def _mm_kernel(*refs, nk, epilogue, cast_w):
    x_ref, w_ref = refs[0], refs[1]
    pos = 2
    res_ref = p_ref = wp_ref = None
    if epilogue in ("res", "ple"):
        res_ref = refs[pos]
        pos += 1
    if epilogue == "ple":
        p_ref, wp_ref = refs[pos], refs[pos + 1]
        pos += 2
    o_ref = refs[pos]
    pos += 1
    wb_ref = None
    if cast_w:
        wb_ref = refs[pos]
        pos += 1
    acc_ref = refs[pos] if nk > 1 else None

    if cast_w:
        @pl.when(pl.program_id(1) == 0)
        def _():
            wb_ref[...] = w_ref[...].astype(BF16)
        w = wb_ref[...]
    else:
        w = w_ref[...]

    def finish(acc):
        if epilogue == "res":
            acc = res_ref[...] + acc
        elif epilogue == "ple":
            acc = res_ref[...] + _sigmoid(acc) * _dot(p_ref[...], wp_ref[...])
        o_ref[...] = acc.astype(o_ref.dtype)

    part = _dot(x_ref[...], w)
    if nk == 1:
        finish(part)
    else:
        k = pl.program_id(2)

        @pl.when(k == 0)
        def _():
            acc_ref[...] = part

        @pl.when(jnp.logical_and(k > 0, k < nk - 1))
        def _():
            acc_ref[...] += part

        @pl.when(k == nk - 1)
        def _():
            finish(acc_ref[...] + part)


def _matmul(x, w, *, res=None, p=None, wp=None, tm=512, tn=512, nk=1, n_cols=None, out_dtype=F32,
            name="matmul"):
    m, kdim = x.shape
    n = w.shape[1] if n_cols is None else n_cols
    tm = _pick(m, tm, SUBLANES)
    tn = _pick(n, tn)
    assert kdim % nk == 0 and (nk == 1 or (kdim // nk) % LANES == 0)
    tk = kdim // nk
    cast_w = w.dtype != BF16
    assert not (cast_w and nk > 1)
    epilogue = "none" if res is None else ("res" if p is None else "ple")
    in_specs = [pl.BlockSpec((tm, tk), lambda j, i, k: (i, k)),
                pl.BlockSpec((tk, tn), lambda j, i, k: (k, j))]
    args = [x, w]
    if res is not None:
        in_specs.append(pl.BlockSpec((tm, tn), lambda j, i, k: (i, j)))
        args.append(res)
    if p is not None:
        pd = p.shape[1]
        in_specs += [pl.BlockSpec((tm, pd), lambda j, i, k: (i, 0)),
                     pl.BlockSpec((pd, tn), lambda j, i, k: (0, j))]
        args += [p, wp]
    scratch = []
    if cast_w:
        scratch.append(pltpu.VMEM((tk, tn), BF16))
    if nk > 1:
        scratch.append(pltpu.VMEM((tm, tn), F32))
    return pl.pallas_call(
        functools.partial(_mm_kernel, nk=nk, epilogue=epilogue, cast_w=cast_w),
        grid=(n // tn, m // tm, nk),
        in_specs=in_specs,
        out_specs=pl.BlockSpec((tm, tn), lambda j, i, k: (i, j)),
        out_shape=jax.ShapeDtypeStruct((m, n), out_dtype),
        scratch_shapes=scratch,
        compiler_params=_cparams("parallel", "arbitrary", "arbitrary"),
        name=name,
    )(*args)


def _conv_taps(cur_ref, h_ref, w_ref, e_ref, *, taps, stride, tm, off):
    hist = (taps - 1) * stride
    e_ref[off - hist:off, :] = h_ref[...]
    e_ref[off:off + tm, :] = cur_ref[...]
    acc = w_ref[0:1, :] * e_ref[off - hist:off - hist + tm, :]
    for j in range(1, taps):
        s = off - hist + j * stride
        acc = acc + w_ref[j:j + 1, :] * e_ref[s:s + tm, :]
    return acc


def _ffn_act_kernel(*refs, taps, stride, tm, off, aliased):
    g_ref, v_ref, hg_ref, hv_ref, wg_ref, wv_ref, bg_ref, bv_ref = refs[:8]
    o_ref, eg_ref, ev_ref = refs[8 + aliased:]
    kw = dict(taps=taps, stride=stride, tm=tm, off=off)
    gate = _conv_taps(g_ref, hg_ref, wg_ref, eg_ref, **kw) + bg_ref[...]
    val = _conv_taps(v_ref, hv_ref, wv_ref, ev_ref, **kw) + bv_ref[...]
    o_ref[...] = (_silu(gate) * val).astype(o_ref.dtype)


def _ffn_act(up, hist, conv_w, conv_b, *, tm, stride, t0, nt, prev=None):
    t, f2 = up.shape
    f = f2 // 2
    taps = conv_w.shape[0]
    hr = (taps - 1) * stride
    off = -(-hr // SUBLANES) * SUBLANES
    tc = _pick(f, 512)
    nc = f // tc
    aliased = prev is not None
    kern = functools.partial(_ffn_act_kernel, taps=taps, stride=stride, tm=tm, off=off, aliased=aliased)
    in_specs = [
        pl.BlockSpec((tm, tc), lambda i, j: (i + t0, j)),
        pl.BlockSpec((tm, tc), lambda i, j: (i + t0, j + nc)),
        pl.BlockSpec((None, hr, tc), lambda i, j: (i, 0, j)),
        pl.BlockSpec((None, hr, tc), lambda i, j: (i, 0, j + nc)),
        pl.BlockSpec((taps, tc), lambda i, j: (0, j)),
        pl.BlockSpec((taps, tc), lambda i, j: (0, j + nc)),
        pl.BlockSpec((1, tc), lambda i, j: (0, j)),
        pl.BlockSpec((1, tc), lambda i, j: (0, j + nc)),
    ]
    args = [up, up, hist, hist, conv_w, conv_w, conv_b.reshape(1, f2), conv_b.reshape(1, f2)]
    if aliased:
        in_specs.append(pl.BlockSpec(memory_space=pl.ANY))
        args.append(prev)
    return pl.pallas_call(
        kern,
        grid=(nt, nc),
        in_specs=in_specs,
        out_specs=pl.BlockSpec((tm, tc), lambda i, j: (i + t0, j)),
        out_shape=jax.ShapeDtypeStruct((t, f), BF16),
        scratch_shapes=[pltpu.VMEM((off + tm, tc), F32), pltpu.VMEM((off + tm, tc), F32)],
        input_output_aliases={8: 0} if aliased else {},
        compiler_params=_cparams("parallel", "parallel"),
        name="ffn_conv_act",
    )(*args)


def _qkv_conv_kernel(*refs, taps, stride, tm, off, aliased):
    c_ref, h_ref, w_ref = refs[:3]
    o_ref, e_ref = refs[3 + aliased:]
    o_ref[...] = _silu(_conv_taps(c_ref, h_ref, w_ref, e_ref, taps=taps, stride=stride, tm=tm, off=off))


def _qkv_conv(proj, hist, conv_w, *, tm, stride, t0, nt, prev=None):
    t = proj.shape[0]
    taps, c = conv_w.shape
    hr = (taps - 1) * stride
    off = -(-hr // SUBLANES) * SUBLANES
    tc = _pick(c, 1024)
    aliased = prev is not None
    kern = functools.partial(_qkv_conv_kernel, taps=taps, stride=stride, tm=tm, off=off, aliased=aliased)
    in_specs = [
        pl.BlockSpec((tm, tc), lambda i, j: (i + t0, j)),
        pl.BlockSpec((None, hr, tc), lambda i, j: (i, 0, j)),
        pl.BlockSpec((taps, tc), lambda i, j: (0, j)),
    ]
    args = [proj, hist, conv_w]
    if aliased:
        in_specs.append(pl.BlockSpec(memory_space=pl.ANY))
        args.append(prev)
    return pl.pallas_call(
        kern,
        grid=(nt, c // tc),
        in_specs=in_specs,
        out_specs=pl.BlockSpec((tm, tc), lambda i, j: (i + t0, j)),
        out_shape=jax.ShapeDtypeStruct((t, c), F32),
        scratch_shapes=[pltpu.VMEM((off + tm, tc), F32)],
        input_output_aliases={3: 0} if aliased else {},
        compiler_params=_cparams("parallel", "parallel"),
        name="gdn_conv",
    )(*args)


def _pool_kernel(*refs, stride, tm, aliased):
    h_ref, hist_ref, cnt_ref, w_ref, sc_ref, x_ref = refs[:6]
    o_ref, e_ref, acc_ref = refs[6 + aliased:]
    g = pl.program_id(1)
    off = POOL_HIST * stride
    window = lax.shift_left(jnp.int32(2), g)
    e_ref[0:off, :] = hist_ref[...]
    e_ref[off:off + tm, :] = h_ref[...]
    acc_ref[...] = e_ref[off:off + tm, :] + e_ref[off - stride:off - stride + tm, :]
    for j in range(2, max(POOL_WINDOWS)):
        @pl.when(j < window)
        def _(j=j):
            acc_ref[...] += e_ref[off - j * stride:off - j * stride + tm, :]
    pooled = acc_ref[...] / cnt_ref[...] - h_ref[...]
    y = _dot(pooled.astype(BF16), w_ref[...]) * sc_ref[...]
    o_ref[...] = x_ref[...] + y


def _pool_mixer(h, hist, cnt, pool_w, scale, x, *, tm, stride, t0, nt, prev=None):
    t, d = h.shape
    g = pool_w.shape[0]
    dg = d // g
    hr = POOL_HIST * stride
    aliased = prev is not None
    kern = functools.partial(_pool_kernel, stride=stride, tm=tm, aliased=aliased)
    in_specs = [
        pl.BlockSpec((tm, dg), lambda i, j: (i + t0, j)),
        pl.BlockSpec((None, hr, dg), lambda i, j: (i, 0, j)),
        pl.BlockSpec((None, tm, 1), lambda i, j: (j, i + t0, 0)),
        pl.BlockSpec((None, dg, dg), lambda i, j: (j, 0, 0)),
        pl.BlockSpec((1, dg), lambda i, j: (0, j)),
        pl.BlockSpec((tm, dg), lambda i, j: (i + t0, j)),
    ]
    args = [h, hist, cnt, pool_w, scale.reshape(1, d), x]
    if aliased:
        in_specs.append(pl.BlockSpec(memory_space=pl.ANY))
        args.append(prev)
    return pl.pallas_call(
        kern,
        grid=(nt, g),
        in_specs=in_specs,
        out_specs=pl.BlockSpec((tm, dg), lambda i, j: (i + t0, j)),
        out_shape=jax.ShapeDtypeStruct((t, d), F32),
        scratch_shapes=[pltpu.VMEM((hr + tm, dg), F32), pltpu.VMEM((tm, dg), F32)],
        input_output_aliases={6: 0} if aliased else {},
        compiler_params=_cparams("parallel", "arbitrary"),
        name="pool_mixer",
    )(*args)


def _gdn_kernel(*refs, heads, dk, rows, seg, valid, carry, unroll):
    (q_ref, k_ref, v_ref, pa_ref, pb_ref, z_ref, s0_ref, alog_ref, dtb_ref, on_ref, ex_ref,
     o_ref, sn_ref, gcb_s, bb_s, gct_s) = refs[:16]
    s_s = refs[16] if carry else None
    dv = dk
    nseg = rows // seg
    sh = int(math.log2(seg))
    assert 1 << sh == seg and seg % BASE_BLOCK == 0

    if carry:
        @pl.when(pl.program_id(1) == 0)
        def _():
            s_s[...] = s0_ref[...]

    rid = lax.broadcasted_iota(jnp.int32, (rows, 1), 0)
    live = jnp.bitwise_and(rid, seg - 1) < valid
    rseg = lax.shift_right_logical(rid, sh)
    ri = lax.broadcasted_iota(jnp.int32, (rows, rows), 0)
    ci = lax.broadcasted_iota(jnp.int32, (rows, rows), 1)
    same = lax.shift_right_logical(ri, sh) == lax.shift_right_logical(ci, sh)
    causal = jnp.logical_and(same, ci <= ri)
    strict = jnp.logical_and(same, ci < ri)
    base = lax.shift_right_logical(ri, 3) == lax.shift_right_logical(ci, 3)
    eye = (ri == ci).astype(F32)
    levels = []
    s = BASE_BLOCK
    while s < seg:
        rb = lax.shift_right_logical(ri, int(math.log2(s)))
        cb = lax.shift_right_logical(ci, int(math.log2(s)))
        levels.append(jnp.logical_and(jnp.bitwise_and(rb, 1) == 1, cb == rb - 1))
        s *= 2

    a = pa_ref[...] + dtb_ref[...]
    softplus = jnp.maximum(a, 0.0) + jnp.log1p(jnp.exp(-jnp.abs(a)))
    glog = jnp.where(live, -jnp.exp(alog_ref[...]) * softplus, 0.0)
    beta = jnp.where(live, _sigmoid(pb_ref[...]), 0.0)
    gc = _dot(causal.astype(F32), glog, precision=HIGHEST)
    gcb_s[...] = _expand(gc, ex_ref[...], 3)
    bb_s[...] = _expand(beta, ex_ref[...], 2)
    gct_s[...] = gc.T

    def widen(x):
        if rows <= dk:
            return x[:, :rows]
        return jnp.concatenate([x] * (rows // dk), axis=1)

    def group(gi, carry_unused):
        hs = [gi * unroll + j for j in range(unroll)]
        los = [pl.multiple_of(h * dk, dk) for h in hs]
        each = range(unroll)
        qs = [q_ref[:, pl.ds(lo, dk)] for lo in los]
        ks = [k_ref[:, pl.ds(lo, dk)] for lo in los]
        vs = [v_ref[:, pl.ds(lo, dk)] for lo in los]
        zs = [z_ref[:, pl.ds(lo, dv)] for lo in los]
        gcbs = [gcb_s[:, pl.ds(lo, dk)] for lo in los]
        bbs = [bb_s[:, pl.ds(lo, dk)] for lo in los]
        grow = [gct_s[pl.ds(h, 1), :] for h in hs]
        if carry:
            sts = [s_s[h] for h in hs]
        qs = [q * lax.rsqrt(jnp.sum(q * q, axis=-1, keepdims=True) + 1e-6) * (dk ** -0.5) for q in qs]
        ks = [k * lax.rsqrt(jnp.sum(k * k, axis=-1, keepdims=True) + 1e-6) for k in ks]
        egcs = [jnp.exp(g) for g in gcbs]
        decays = [jnp.where(causal, jnp.exp(jnp.where(causal, widen(gcbs[j]) - grow[j], 0.0)), 0.0)
                  for j in each]
        kbf = [k.astype(BF16) for k in ks]
        qkks = [_dot_nt(jnp.concatenate([qs[j].astype(BF16), kbf[j]], axis=0), kbf[j]) for j in each]
        qks = [qkks[j][:rows] * decays[j] for j in each]
        amats = [jnp.where(strict, widen(bbs[j]) * decays[j] * qkks[j][rows:], 0.0) for j in each]
        pws = [-jnp.where(base, a, 0.0) for a in amats]
        invs = [eye + p for p in pws]
        for _ in range(2):
            pws = [_mm(p, p) for p in pws]
            invs = [invs[j] + _mm(pws[j], invs[j]) for j in each]
        for lvl in levels:
            lows = [_mm(jnp.where(lvl, amats[j], 0.0), invs[j]) for j in each]
            invs = [invs[j] - _mm(invs[j], lows[j]) for j in each]
        sols = [_mm(invs[j], jnp.concatenate([bbs[j] * vs[j], bbs[j] * egcs[j] * ks[j]], axis=1))
                for j in each]
        wvs = [sl[:, :dv] for sl in sols]
        wks = [sl[:, dv:] for sl in sols]
        qds = [qs[j] * egcs[j] for j in each]
        if carry:
            us = [[] for _ in each]
            os_ = [[] for _ in each]
            for sg in range(nseg):
                r = slice(sg * seg, (sg + 1) * seg)
                last = slice((sg + 1) * seg - 1, (sg + 1) * seg)
                both = [_dot(jnp.concatenate([wks[j][r], qds[j][r]], axis=0).astype(BF16), sts[j].astype(BF16))
                        for j in each]
                u = [wvs[j][r] - both[j][:seg] for j in each]
                kend = [ks[j][r] * jnp.exp(gcbs[j][last] - gcbs[j][r]) for j in each]
                sts = [jnp.exp(gcbs[j][last]) * sts[j] + _dot_tn(kend[j].astype(BF16), u[j].astype(BF16))
                       for j in each]
                for j in each:
                    us[j].append(u[j])
                    os_[j].append(both[j][seg:])
            u_all = [x[0] if nseg == 1 else jnp.concatenate(x, axis=0) for x in us]
            o = [x[0] if nseg == 1 else jnp.concatenate(x, axis=0) for x in os_]
        else:
            lhs = [jnp.concatenate([wks[j], qds[j]], axis=0).astype(BF16) for j in each]
            u_all = [jnp.zeros((rows, dv), F32) for _ in each]
            o = [jnp.zeros((rows, dv), F32) for _ in each]
            news = []
            for sg in range(nseg):
                last = slice((sg + 1) * seg - 1, (sg + 1) * seg)
                mine = rseg == sg
                st0 = [s0_ref[sg, h] for h in hs]
                both = [_dot(lhs[j], st0[j].astype(BF16)) for j in each]
                u_m = [jnp.where(mine, wvs[j] - both[j][:rows], 0.0) for j in each]
                u_all = [u_all[j] + u_m[j] for j in each]
                o = [o[j] + jnp.where(mine, both[j][rows:], 0.0) for j in each]
                kend = [jnp.where(mine, ks[j] * jnp.exp(jnp.where(mine, gcbs[j][last] - gcbs[j], 0.0)), 0.0)
                        for j in each]
                news.append([jnp.exp(gcbs[j][last]) * st0[j] + _dot_tn(kend[j].astype(BF16), u_m[j].astype(BF16))
                             for j in each])
        o = [o[j] + _mm(qks[j], u_all[j]) for j in each]
        o = [o[j] * lax.rsqrt(jnp.mean(o[j] * o[j], axis=-1, keepdims=True) + EPS) * on_ref[...] * _silu(zs[j])
             for j in each]
        for j in each:
            o_ref[:, pl.ds(los[j], dv)] = o[j].astype(o_ref.dtype)
            if carry:
                s_s[hs[j]] = sts[j]
            else:
                for sg in range(nseg):
                    sn_ref[sg, hs[j]] = news[sg][j]
        return carry_unused

    assert heads % unroll == 0
    lax.fori_loop(0, heads // unroll, group, 0)

    if carry:
        @pl.when(pl.program_id(1) == pl.num_programs(1) - 1)
        def _():
            sn_ref[...] = s_s[...]


def _gdn(qkv, proj, s0, a_log, dt_bias, out_norm, *, cols, nseq, seqlen, rows, seg, valid, carry):
    _, heads, dk, dv = s0.shape
    assert dk == LANES and dv == LANES and heads <= LANES and rows % seg == 0
    hk = heads * dk
    expand = np.zeros((LANES, hk), np.float32)
    for h in range(heads):
        expand[h, h * dk:(h + 1) * dk] = 1.0
    lane_pad = lambda t: jnp.pad(t.reshape(1, heads).astype(F32), ((0, 0), (0, LANES - heads)))
    col_z, col_a, col_b = cols
    assert col_z % hk == 0 and col_a % LANES == 0 and col_b % LANES == 0
    if carry:
        nblk = seqlen // rows
        grid = (nseq, nblk)
        rmap = lambda i, c: i * nblk + c
        smap = lambda i, c: (i, 0, 0, 0)
        sblock = (None, heads, dk, dv)
        sem = ("parallel", "arbitrary")
    else:
        spb = rows // seg
        grid = (nseq // spb, 1)
        rmap = lambda i, c: i
        smap = lambda i, c: (i, 0, 0, 0)
        sblock = (spb, heads, dk, dv)
        sem = ("parallel", "arbitrary")
    rspec = lambda w, cb: pl.BlockSpec((rows, w), lambda i, c: (rmap(i, c), cb))
    full = lambda shape: pl.BlockSpec(shape, lambda i, c: (0,) * len(shape))
    kern = functools.partial(_gdn_kernel, heads=heads, dk=dk, rows=rows, seg=seg, valid=valid,
                             carry=carry, unroll=min(heads, 8) if carry else 2)
    scratch = [pltpu.VMEM((rows, hk), F32), pltpu.VMEM((rows, hk), F32), pltpu.VMEM((LANES, rows), F32)]
    if carry:
        scratch.append(pltpu.VMEM((heads, dk, dv), F32))
    t = nseq * seqlen
    return pl.pallas_call(
        kern,
        grid=grid,
        in_specs=[
            rspec(hk, 0), rspec(hk, 1), rspec(hk, 2),
            rspec(LANES, col_a // LANES), rspec(LANES, col_b // LANES), rspec(hk, col_z // hk),
            pl.BlockSpec(sblock, smap),
            full((1, LANES)), full((1, LANES)), full((1, dv)), full((LANES, hk)),
        ],
        out_specs=[rspec(hk, 0), pl.BlockSpec(sblock, smap)],
        out_shape=[jax.ShapeDtypeStruct((t, heads * dv), BF16), jax.ShapeDtypeStruct(s0.shape, F32)],
        scratch_shapes=scratch,
        compiler_params=_cparams(*sem),
        name="gated_deltanet",
    )(qkv, qkv, qkv, proj, proj, proj, s0, lane_pad(a_log), lane_pad(dt_bias),
      out_norm.reshape(1, dv).astype(F32), jnp.asarray(expand, dtype=BF16))


def _ret_kernel(*refs, hps, dk, rows, seg, carry):
    (q_ref, k_ref, v_ref, g_ref, cos_ref, sin_ref, s0_ref, dec_ref, qs_ref, ks_ref, ge_ref,
     o_ref, sn_ref) = refs[:13]
    s_s = refs[13] if carry else None
    dv = dk
    half = dk // 2
    nseg = rows // seg
    sh = int(math.log2(seg))
    hstep = pl.program_id(1) if not carry else 0

    if carry:
        @pl.when(pl.program_id(1) == 0)
        def _():
            s_s[...] = s0_ref[...]

    cos = cos_ref[...]
    sin = sin_ref[...]
    rseg = lax.shift_right_logical(lax.broadcasted_iota(jnp.int32, (rows, 1), 0), sh)

    def rot(x):
        x1, x2 = x[:, :half], x[:, half:]
        return jnp.concatenate([x1 * cos - x2 * sin, x1 * sin + x2 * cos], axis=1)

    def head(hh, carry_unused):
        h = hstep * hps + hh
        lo = pl.multiple_of(h * dk, dk)
        q = rot(q_ref[:, pl.ds(lo, dk)])
        k = rot(k_ref[:, pl.ds(lo, dk)]) * (dk ** -0.5)
        v_bf = v_ref[:, pl.ds(lo, dv)].astype(BF16)
        qk = _dot_nt(q.astype(BF16), k.astype(BF16)) * dec_ref[h]
        qd = (q * qs_ref[:, pl.ds(lo, dk)]).astype(BF16)
        ke = k * ks_ref[:, pl.ds(lo, dk)]
        if carry:
            st = s_s[h]
            os_ = []
            for sg in range(nseg):
                r0 = sg * seg
                os_.append(_dot(qd[r0:r0 + seg], st.astype(BF16)))
                st = ge_ref[h] * st + _dot_tn(ke[r0:r0 + seg].astype(BF16), v_bf[r0:r0 + seg])
            s_s[h] = st
            o = os_[0] if nseg == 1 else jnp.concatenate(os_, axis=0)
        else:
            o = jnp.zeros((rows, dv), F32)
            for sg in range(nseg):
                mine = rseg == sg
                st = s0_ref[sg, hh]
                o = o + jnp.where(mine, _dot(qd, st.astype(BF16)), 0.0)
                sn_ref[sg, hh] = ge_ref[h] * st + _dot_tn(jnp.where(mine, ke, 0.0).astype(BF16), v_bf)
        o = o + _dot(qk.astype(BF16), v_bf)
        mu = jnp.mean(o, axis=-1, keepdims=True)
        var = jnp.mean(jnp.square(o - mu), axis=-1, keepdims=True)
        o = (o - mu) * lax.rsqrt(var + EPS) * _silu(g_ref[:, pl.ds(lo, dv)])
        o_ref[:, pl.ds(lo, dv)] = o.astype(o_ref.dtype)
        return carry_unused

    lax.fori_loop(0, hps, head, 0)

    if carry:
        @pl.when(pl.program_id(1) == pl.num_programs(1) - 1)
        def _():
            sn_ref[...] = s_s[...]


def _retention(proj, s0, *, col_q, nseq, seqlen, rows, seg, valid, pos0, carry):
    _, heads, dk, dv = s0.shape
    assert dk == dv and rows % seg == 0
    hw = heads * dk
    half = dk // 2
    assert col_q % hw == 0
    cq = col_q // hw
    npos = seqlen if carry else rows
    offs = np.arange(npos) if carry else np.arange(rows) % seg
    inv_freq = ROPE_BASE ** (-np.arange(half, dtype=np.float64) / half)
    ang = (pos0 + offs.astype(np.float64))[:, None] * inv_freq[None, :]
    cos, sin = np.cos(ang).astype(np.float32), np.sin(ang).astype(np.float32)
    log_gamma = np.log1p(-np.exp2(-5.0 - np.arange(heads, dtype=np.float64)))
    pin = np.arange(rows) % seg
    steps = np.minimum(pin + 1, valid).astype(np.float64)
    gc = log_gamma[:, None] * steps[None, :]
    gl = log_gamma * min(seg, valid)
    sameseg = (np.arange(rows)[:, None] // seg) == (np.arange(rows)[None, :] // seg)
    causal = np.logical_and(sameseg, np.tril(np.ones((rows, rows), bool)))
    diff = gc[:, :, None] - gc[:, None, :]
    dec = np.where(causal[None], np.exp(np.where(causal[None], diff, 0.0)), 0.0).astype(np.float32)
    qs = np.repeat(np.exp(gc).T, dk, axis=1).astype(np.float32)
    ks = np.repeat(np.exp(gl[:, None] - gc).T, dk, axis=1).astype(np.float32)
    ge = np.broadcast_to(np.exp(gl)[:, None, None], (heads, 1, dv)).astype(np.float32)
    if carry:
        nblk = seqlen // rows
        hps = heads
        grid = (nseq, nblk)
        rmap = lambda i, c: i * nblk + c
        tmap = lambda i, c: (c, 0)
        smap = lambda i, c: (i, 0, 0, 0)
        sblock = (None, heads, dk, dv)
    else:
        spb = rows // seg
        hps = max(1, heads // 2)
        grid = (nseq // spb, heads // hps)
        rmap = lambda i, c: i
        tmap = lambda i, c: (0, 0)
        smap = lambda i, c: (i, c, 0, 0)
        sblock = (spb, hps, dk, dv)
    rspec = lambda cb: pl.BlockSpec((rows, hw), lambda i, c: (rmap(i, c), cb))
    full = lambda shape: pl.BlockSpec(shape, lambda i, c: (0,) * len(shape))
    kern = functools.partial(_ret_kernel, hps=hps, dk=dk, rows=rows, seg=seg, carry=carry)
    t = nseq * seqlen
    return pl.pallas_call(
        kern,
        grid=grid,
        in_specs=[
            rspec(cq), rspec(cq + 1), rspec(cq + 2), rspec(cq + 3),
            pl.BlockSpec((rows, half), tmap), pl.BlockSpec((rows, half), tmap),
            pl.BlockSpec(sblock, smap),
            full((heads, rows, rows)), full((rows, hw)), full((rows, hw)), full((heads, 1, dv)),
        ],
        out_specs=[rspec(0), pl.BlockSpec(sblock, smap)],
        out_shape=[jax.ShapeDtypeStruct((t, heads * dv), BF16), jax.ShapeDtypeStruct(s0.shape, F32)],
        scratch_shapes=[pltpu.VMEM((heads, dk, dv), F32)] if carry else [],
        compiler_params=_cparams("parallel", "arbitrary"),
        name="retention",
    )(proj, proj, proj, proj, jnp.asarray(cos), jnp.asarray(sin), s0, jnp.asarray(dec), jnp.asarray(qs),
      jnp.asarray(ks), jnp.asarray(ge))


def _pack_w_in(w, gdn_c, heads_a, gdn_v):
    off_a = gdn_c
    off_b = off_a + heads_a
    off_z = off_b + heads_a
    off_r = off_z + gdn_v
    zpad = jnp.zeros((w.shape[0], LANES - heads_a), w.dtype)
    return jnp.concatenate([w[:, :off_a], w[:, off_z:off_r], w[:, off_r:], w[:, off_a:off_b], zpad,
                            w[:, off_b:off_z], zpad], axis=1)


def kernel(x_prompt, x_sample, state_gdn_conv, state_gdn, state_ret, state_pool, state_ffn_conv, p_prompt, p_sample, norm_mix, norm_ffn, norm_ple, norm_final, w_in, gdn_conv_w, gdn_a_log, gdn_dt_bias, gdn_out_norm, w_out, pool_w, pool_scale, ffn_w_up, ffn_conv_w, ffn_conv_b, ffn_w_down, ple_w_gate, ple_w_proj):
    bp, lp, d = x_prompt.shape
    bs, ls, _ = x_sample.shape
    depth = p_prompt.shape[0]
    tp, ts = bp * lp, bs * ls
    t = tp + ts
    tm = ts
    assert lp % tm == 0 and tm % SUBLANES == 0
    ntp = tp // tm
    _, _, heads_a, dk_a, dv_a = state_gdn.shape
    _, _, heads_r, dk_r, dv_r = state_ret.shape
    gdn_c = state_gdn_conv.shape[-1]
    gdn_v = heads_a * dv_a
    ret_w = heads_r * dk_r
    gtaps = gdn_conv_w.shape[1]
    ftaps = ffn_conv_w.shape[1]
    pool_past = state_pool.shape[2]
    f2 = ffn_w_up.shape[-1]
    assert min(lp, ls) >= max(gtaps, ftaps) - 1 and pool_past == POOL_HIST - 1
    assert lp % CHUNK == 0 and ls <= BASE_BLOCK and gdn_c == 3 * gdn_v and heads_a * dk_a == gdn_v
    col_z, col_r = gdn_c, gdn_c + gdn_v
    col_a = col_r + 4 * ret_w
    col_b = col_a + LANES
    n_proj = col_b + LANES

    rows_p = lambda a: a.reshape((tp,) + a.shape[2:])
    rows_s = lambda a: jnp.swapaxes(a, 0, 1).reshape((ts,) + a.shape[2:])
    join = lambda a, b: jnp.concatenate([rows_p(a), rows_s(b)], axis=0)
    seq_p = lambda a: a[:tp].reshape((bp, lp) + a.shape[1:])
    seq_s = lambda a: jnp.swapaxes(a[tp:].reshape((ls, bs) + a.shape[1:]), 0, 1)

    def hist_p(rows_bl, nrows):
        c = rows_bl.shape[-1]
        tl = rows_bl.reshape(bp, lp // tm, tm, c)[:, :, tm - nrows:, :]
        prev = jnp.concatenate([jnp.zeros((bp, 1, nrows, c), rows_bl.dtype), tl[:, :-1]], axis=1)
        return prev.reshape(ntp, nrows, c)

    def hist_s(state, nrows):
        return jnp.swapaxes(state[:, -nrows:], 0, 1).reshape(1, nrows * bs, state.shape[-1])

    def pad_seq(a):
        a = seq_s(a)
        a = jnp.pad(a, ((0, 0), (0, BASE_BLOCK - ls)) + ((0, 0),) * (a.ndim - 2))
        return a.reshape((bs * BASE_BLOCK,) + a.shape[2:])

    unpad_seq = lambda a: rows_s(a.reshape((bs, BASE_BLOCK) + a.shape[1:])[:, :ls])

    xr = join(x_prompt, x_sample)
    pj = [join(p_prompt[i], p_sample[i]).astype(BF16) for i in range(depth)]
    pool_w_bf = pool_w.astype(BF16)
    w_down_bf = ffn_w_down.astype(BF16)
    w_proj_bf = ple_w_proj.astype(BF16)
    rows_prompt = _pick(lp, 256, CHUNK)
    spb = max(1, CHUNK // BASE_BLOCK)
    assert bs % spb == 0

    new_gdn_conv_p, new_gdn_p, new_ret_p, new_pool_p, new_ffn_p = [], [], [], [], []
    new_gdn_conv_s, new_gdn_s, new_ret_s, new_pool_s, new_ffn_s = [], [], [], [], []
    for i in range(depth):
        j = i // 2
        if i % 2 == 0:
            (h,) = _rmsnorm(xr, norm_mix[i], [BF16])
            proj = _matmul(h, _pack_w_in(w_in[j], gdn_c, heads_a, gdn_v), tn=640, name="in_proj")
            pc_p = seq_p(proj[:, :gdn_c])
            qkv = _qkv_conv(proj, hist_p(pc_p, gtaps - 1), gdn_conv_w[j], tm=tm, stride=1, t0=0, nt=ntp)
            qkv = _qkv_conv(proj, hist_s(state_gdn_conv[j], gtaps - 1), gdn_conv_w[j], tm=tm, stride=bs,
                            t0=ntp, nt=1, prev=qkv)
            cols = (col_z, col_a, col_b)
            o_gdn_p, s_a_p = _gdn(qkv, proj, jnp.zeros((bp,) + state_gdn.shape[2:], F32), gdn_a_log[j],
                                  gdn_dt_bias[j], gdn_out_norm[j], cols=cols, nseq=bp, seqlen=lp,
                                  rows=rows_prompt, seg=CHUNK, valid=CHUNK, carry=True)
            o_ret_p, s_b_p = _retention(proj, jnp.zeros((bp,) + state_ret.shape[2:], F32), col_q=col_r,
                                        nseq=bp, seqlen=lp, rows=rows_prompt, seg=CHUNK, valid=CHUNK, pos0=0,
                                        carry=True)
            proj_s = pad_seq(proj)
            o_gdn_s, s_a_s = _gdn(pad_seq(qkv), proj_s, state_gdn[j], gdn_a_log[j], gdn_dt_bias[j],
                                  gdn_out_norm[j], cols=cols, nseq=bs, seqlen=BASE_BLOCK,
                                  rows=spb * BASE_BLOCK, seg=BASE_BLOCK, valid=ls, carry=False)
            o_ret_s, s_b_s = _retention(proj_s, state_ret[j], col_q=col_r, nseq=bs, seqlen=BASE_BLOCK,
                                        rows=spb * BASE_BLOCK, seg=BASE_BLOCK, valid=ls, pos0=PAST_LEN,
                                        carry=False)
            mixed = jnp.concatenate([
                jnp.concatenate([o_gdn_p, o_ret_p], axis=1),
                jnp.concatenate([unpad_seq(o_gdn_s), unpad_seq(o_ret_s)], axis=1)], axis=0)
            xr = _matmul(mixed, w_out[j], res=xr, tn=512, name="out_proj")
            new_gdn_conv_p.append(pc_p[:, lp - (gtaps - 1):])
            new_gdn_conv_s.append(jnp.concatenate([state_gdn_conv[j], seq_s(proj[:, :gdn_c])],
                                                  axis=1)[:, -(gtaps - 1):])
            new_gdn_p.append(s_a_p)
            new_gdn_s.append(s_a_s)
            new_ret_p.append(s_b_p)
            new_ret_s.append(s_b_s)
        else:
            h32, = _rmsnorm(xr, norm_mix[i], [F32])
            h_p, h_s = seq_p(h32), seq_s(h32)
            pos = jnp.concatenate([jnp.tile(jnp.arange(lp, dtype=jnp.int32), bp),
                                   jnp.repeat(PAST_LEN + jnp.arange(ls, dtype=jnp.int32), bs)])
            cnt = jnp.stack([jnp.minimum(w, pos + 1).astype(F32) for w in POOL_WINDOWS])[:, :, None]
            zero_row = jnp.zeros((bs, 1, d), F32)
            kw = dict(tm=tm)
            xn = _pool_mixer(h32, hist_p(h_p, POOL_HIST), cnt, pool_w_bf[j], pool_scale[j], xr, stride=1,
                             t0=0, nt=ntp, **kw)
            xr = _pool_mixer(h32, hist_s(jnp.concatenate([zero_row, state_pool[j]], axis=1), POOL_HIST), cnt,
                             pool_w_bf[j], pool_scale[j], xr, stride=bs, t0=ntp, nt=1, prev=xn, **kw)
            new_pool_p.append(h_p[:, lp - pool_past:])
            new_pool_s.append(jnp.concatenate([state_pool[j], h_s], axis=1)[:, -pool_past:])
        (h,) = _rmsnorm(xr, norm_ffn[i], [BF16])
        up = _matmul(h, ffn_w_up[i], tn=512, name="ffn_up")
        up_p = seq_p(up)
        act = _ffn_act(up, hist_p(up_p, ftaps - 1), ffn_conv_w[i], ffn_conv_b[i], tm=tm, stride=1, t0=0, nt=ntp)
        act = _ffn_act(up, hist_s(state_ffn_conv[i], ftaps - 1), ffn_conv_w[i], ffn_conv_b[i], tm=tm,
                       stride=bs, t0=ntp, nt=1, prev=act)
        nk = 2 if (act.shape[1] % (2 * LANES) == 0) else 1
        xr = _matmul(act, w_down_bf[i], res=xr, tn=512, nk=nk, name="ffn_down")
        new_ffn_p.append(up_p[:, lp - (ftaps - 1):])
        new_ffn_s.append(jnp.concatenate([state_ffn_conv[i], seq_s(up)], axis=1)[:, -(ftaps - 1):])
        (h,) = _rmsnorm(xr, norm_ple[i], [BF16])
        xr = _matmul(h, ple_w_gate[i], res=xr, p=pj[i], wp=w_proj_bf[i], tn=512, name="ple")
    (out,) = _rmsnorm(xr, norm_final, [F32])
    st = lambda lst: jnp.stack(lst)
    return (seq_p(out), seq_s(out),
            st(new_gdn_conv_p), st(new_gdn_p), st(new_ret_p), st(new_pool_p), st(new_ffn_p),
            st(new_gdn_conv_s), st(new_gdn_s), st(new_ret_s), st(new_pool_s), st(new_ffn_s))
```

```python
import functools
import math

import numpy as np
import jax
import jax.numpy as jnp
from jax import lax
from jax.experimental import pallas as pl
from jax.experimental.pallas import tpu as pltpu

F32 = jnp.float32
BF16 = jnp.bfloat16
HIGHEST = lax.Precision.HIGHEST

EPS = 1e-6
CHUNK = 64
PAST_LEN = 16384
ROPE_BASE = 10000.0
POOL_WINDOWS = (2, 4, 8, 16)
POOL_HIST = 16
LANES = 128
SUBLANES = 8
VMEM_LIMIT = 56 * 1024 * 1024
BASE_BLOCK = 8


def _pick(n, pref, align=LANES):
    if n <= pref:
        return n
    t = (pref // align) * align
    while t >= align:
        if n % t == 0:
            return t
        t -= align
    return n


def _cparams(*sem):
    return pltpu.CompilerParams(dimension_semantics=sem, vmem_limit_bytes=VMEM_LIMIT)


def _silu(x):
    return x * (1.0 / (1.0 + jnp.exp(-x)))


def _sigmoid(x):
    return 1.0 / (1.0 + jnp.exp(-x))


def _dot(a, b, **kw):
    return jnp.dot(a, b, preferred_element_type=F32, **kw)


def _dot_nt(a, b, **kw):
    return lax.dot_general(a, b, (((1,), (1,)), ((), ())), preferred_element_type=F32, **kw)


def _dot_tn(a, b, **kw):
    return lax.dot_general(a, b, (((0,), (0,)), ((), ())), preferred_element_type=F32, **kw)


def _mm(a, b):
    return _dot(a.astype(BF16), b.astype(BF16))


def _expand(x, e_bf, passes):
    acc = None
    for _ in range(passes):
        piece = x.astype(BF16)
        term = _dot(piece, e_bf)
        acc = term if acc is None else acc + term
        x = x - piece.astype(F32)
    return acc


def _rmsnorm_kernel(x_ref, g_ref, *o_refs):
    x = x_ref[...]
    y = x * lax.rsqrt(jnp.mean(x * x, axis=-1, keepdims=True) + EPS) * g_ref[...]
    for o_ref in o_refs:
        o_ref[...] = y.astype(o_ref.dtype)


def _rmsnorm(x, gain, dtypes):
    m, d = x.shape
    tm = _pick(m, 256, SUBLANES)
    return pl.pallas_call(
        _rmsnorm_kernel,
        grid=(m // tm,),
        in_specs=[pl.BlockSpec((tm, d), lambda i: (i, 0)), pl.BlockSpec((1, d), lambda i: (0, 0))],
        out_specs=[pl.BlockSpec((tm, d), lambda i: (i, 0)) for _ in dtypes],
        out_shape=[jax.ShapeDtypeStruct((m, d), dt) for dt in dtypes],
        compiler_params=_cparams("parallel"),
        name="rmsnorm",
    )(x, gain.reshape(1, d))


def _mm_kernel(*refs, epilogue):
    x_ref, w_ref = refs[0], refs[1]
    o_ref = refs[-1]
    acc = _dot(x_ref[...], w_ref[...].astype(BF16))
    if epilogue == "res":
        acc = refs[2][...] + acc
    elif epilogue == "ple":
        res_ref, p_ref, wp_ref = refs[2:5]
        acc = res_ref[...] + _sigmoid(acc) * _dot(p_ref[...], wp_ref[...])
    o_ref[...] = acc.astype(o_ref.dtype)


def _matmul(x, w, *, res=None, p=None, wp=None, tm=2176, tn=256, out_dtype=F32, name="matmul"):
    m, kdim = x.shape
    n = w.shape[1]
    tm = _pick(m, tm, 2 * SUBLANES)
    tn = _pick(n, tn)
    epilogue = "none" if res is None else ("res" if p is None else "ple")
    in_specs = [pl.BlockSpec((tm, kdim), lambda i, j: (i, 0), pipeline_mode=pl.Buffered(1)),
                pl.BlockSpec((kdim, tn), lambda i, j: (0, j))]
    args = [x, w]
    if res is not None:
        in_specs.append(pl.BlockSpec((tm, tn), lambda i, j: (i, j)))
        args.append(res)
    if p is not None:
        pd = p.shape[1]
        in_specs += [pl.BlockSpec((tm, pd), lambda i, j: (i, 0), pipeline_mode=pl.Buffered(1)),
                     pl.BlockSpec((pd, tn), lambda i, j: (0, j))]
        args += [p, wp]
    return pl.pallas_call(
        functools.partial(_mm_kernel, epilogue=epilogue),
        grid=(m // tm, n // tn),
        in_specs=in_specs,
        out_specs=pl.BlockSpec((tm, tn), lambda i, j: (i, j)),
        out_shape=jax.ShapeDtypeStruct((m, n), out_dtype),
        compiler_params=_cparams("parallel", "arbitrary"),
        name=name,
    )(*args)


def _causal_conv(cur, w_ref, h_ref, *, taps, stride, tm):
    acc = w_ref[taps - 1:taps, :] * cur
    if h_ref is not None:
        ext = jnp.concatenate([h_ref[...], cur], axis=0)
        for j in range(taps - 1):
            acc = acc + w_ref[j:j + 1, :] * ext[j * stride:j * stride + tm, :]
    else:
        row = lax.broadcasted_iota(jnp.int32, (tm, 1), 0)
        for j in range(taps - 1):
            back = (taps - 1 - j) * stride
            acc = acc + w_ref[j:j + 1, :] * jnp.where(row >= back, pltpu.roll(cur, back, axis=0), 0.0)
    return acc


def _ffn_up_kernel(*refs, taps, stride, tm, has_hist, aliased):
    x_ref, wg_ref, wv_ref, cg_ref, cv_ref, bg_ref, bv_ref = refs[:7]
    hg_ref, hv_ref = (refs[7], refs[8]) if has_hist else (None, None)
    o_ref, sg_ref, sv_ref = refs[7 + 2 * has_hist + aliased:]
    x = x_ref[...]
    keep = (taps - 1) * stride

    def branch(w_ref, c_ref, b_ref, h_ref, s_ref):
        up = _dot(x, w_ref[...].astype(BF16))
        s_ref[...] = up[tm - keep:, :]
        return _causal_conv(up, c_ref, h_ref, taps=taps, stride=stride, tm=tm) + b_ref[...]

    gate = branch(wg_ref, cg_ref, bg_ref, hg_ref, sg_ref)
    val = branch(wv_ref, cv_ref, bv_ref, hv_ref, sv_ref)
    o_ref[...] = (_silu(gate) * val).astype(o_ref.dtype)


def _ffn_up_act(h, w_up, conv_w, conv_b, *, tm, stride, t0, nt, hist=None, prev=None):
    t, kdim = h.shape
    f2 = w_up.shape[1]
    f = f2 // 2
    taps = conv_w.shape[0]
    keep = (taps - 1) * stride
    tc = _pick(f, 256)
    nc = f // tc
    has_hist = hist is not None
    aliased = prev is not None
    kern = functools.partial(_ffn_up_kernel, taps=taps, stride=stride, tm=tm, has_hist=has_hist,
                             aliased=aliased)
    lo = lambda i, j: (0, j)
    hi = lambda i, j: (0, j + nc)
    in_specs = [
        pl.BlockSpec((tm, kdim), lambda i, j: (i + t0, 0), pipeline_mode=pl.Buffered(1)),
        pl.BlockSpec((kdim, tc), lo), pl.BlockSpec((kdim, tc), hi),
        pl.BlockSpec((taps, tc), lo), pl.BlockSpec((taps, tc), hi),
        pl.BlockSpec((1, tc), lo), pl.BlockSpec((1, tc), hi),
    ]
    args = [h, w_up, w_up, conv_w, conv_w, conv_b.reshape(1, f2), conv_b.reshape(1, f2)]
    if has_hist:
        in_specs += [pl.BlockSpec((None, keep, tc), lambda i, j: (i, 0, j)),
                     pl.BlockSpec((None, keep, tc), lambda i, j: (i, 0, j + nc))]
        args += [hist, hist]
    if aliased:
        in_specs.append(pl.BlockSpec(memory_space=pl.ANY))
        args.append(prev)
    sspec = pl.BlockSpec((None, keep, tc), lambda i, j: (i, 0, j))
    return pl.pallas_call(
        kern,
        grid=(nt, nc),
        in_specs=in_specs,
        out_specs=[pl.BlockSpec((tm, tc), lambda i, j: (i + t0, j)), sspec, sspec],
        out_shape=[jax.ShapeDtypeStruct((t, f), BF16), jax.ShapeDtypeStruct((nt, keep, f), F32),
                   jax.ShapeDtypeStruct((nt, keep, f), F32)],
        input_output_aliases={len(args) - 1: 0} if aliased else {},
        compiler_params=_cparams("parallel", "arbitrary"),
        name="ffn_up_conv_act",
    )(*args)


def _qkv_conv_kernel(*refs, taps, stride, tm, has_hist, aliased):
    c_ref, w_ref = refs[:2]
    h_ref = refs[2] if has_hist else None
    o_ref = refs[-1]
    o_ref[...] = _silu(_causal_conv(c_ref[...], w_ref, h_ref, taps=taps, stride=stride, tm=tm))


def _qkv_conv(proj, conv_w, *, tm, stride, t0, nt, hist=None, prev=None):
    t = proj.shape[0]
    taps, c = conv_w.shape
    keep = (taps - 1) * stride
    tc = _pick(c, 512)
    has_hist = hist is not None
    aliased = prev is not None
    kern = functools.partial(_qkv_conv_kernel, taps=taps, stride=stride, tm=tm, has_hist=has_hist,
                             aliased=aliased)
    in_specs = [pl.BlockSpec((tm, tc), lambda i, j: (i + t0, j)), pl.BlockSpec((taps, tc), lambda i, j: (0, j))]
    args = [proj, conv_w]
    if has_hist:
        in_specs.append(pl.BlockSpec((None, keep, tc), lambda i, j: (i, 0, j)))
        args.append(hist)
    if aliased:
        in_specs.append(pl.BlockSpec(memory_space=pl.ANY))
        args.append(prev)
    return pl.pallas_call(
        kern,
        grid=(nt, c // tc),
        in_specs=in_specs,
        out_specs=pl.BlockSpec((tm, tc), lambda i, j: (i + t0, j)),
        out_shape=jax.ShapeDtypeStruct((t, c), F32),
        input_output_aliases={len(args) - 1: 0} if aliased else {},
        compiler_params=_cparams("parallel", "parallel"),
        name="gdn_conv",
    )(*args)


def _pool_kernel(*refs, stride, tm, aliased):
    h_ref, hist_ref, cnt_ref, w_ref, sc_ref, x_ref = refs[:6]
    o_ref, e_ref, acc_ref = refs[6 + aliased:]
    g = pl.program_id(1)
    off = POOL_HIST * stride
    window = lax.shift_left(jnp.int32(2), g)
    e_ref[0:off, :] = hist_ref[...]
    e_ref[off:off + tm, :] = h_ref[...]
    acc_ref[...] = e_ref[off:off + tm, :] + e_ref[off - stride:off - stride + tm, :]
    for j in range(2, max(POOL_WINDOWS)):
        @pl.when(j < window)
        def _(j=j):
            acc_ref[...] += e_ref[off - j * stride:off - j * stride + tm, :]
    pooled = acc_ref[...] / cnt_ref[...] - h_ref[...]
    y = _dot(pooled.astype(BF16), w_ref[...]) * sc_ref[...]
    o_ref[...] = x_ref[...] + y


def _pool_mixer(h, hist, cnt, pool_w, scale, x, *, tm, stride, t0, nt, prev=None):
    t, d = h.shape
    g = pool_w.shape[0]
    dg = d // g
    hr = POOL_HIST * stride
    aliased = prev is not None
    kern = functools.partial(_pool_kernel, stride=stride, tm=tm, aliased=aliased)
    in_specs = [
        pl.BlockSpec((tm, dg), lambda i, j: (i + t0, j)),
        pl.BlockSpec((None, hr, dg), lambda i, j: (i, 0, j)),
        pl.BlockSpec((None, tm, 1), lambda i, j: (j, i + t0, 0)),
        pl.BlockSpec((None, dg, dg), lambda i, j: (j, 0, 0)),
        pl.BlockSpec((1, dg), lambda i, j: (0, j)),
        pl.BlockSpec((tm, dg), lambda i, j: (i + t0, j)),
    ]
    args = [h, hist, cnt, pool_w, scale.reshape(1, d), x]
    if aliased:
        in_specs.append(pl.BlockSpec(memory_space=pl.ANY))
        args.append(prev)
    return pl.pallas_call(
        kern,
        grid=(nt, g),
        in_specs=in_specs,
        out_specs=pl.BlockSpec((tm, dg), lambda i, j: (i + t0, j)),
        out_shape=jax.ShapeDtypeStruct((t, d), F32),
        scratch_shapes=[pltpu.VMEM((hr + tm, dg), F32), pltpu.VMEM((tm, dg), F32)],
        input_output_aliases={6: 0} if aliased else {},
        compiler_params=_cparams("parallel", "arbitrary"),
        name="pool_mixer",
    )(*args)


def _gdn_kernel(*refs, heads, dk, rows, seg, valid, carry, unroll):
    (q_ref, k_ref, v_ref, pa_ref, pb_ref, z_ref, s0_ref, alog_ref, dtb_ref, on_ref, ex_ref,
     o_ref, sn_ref, gcb_s, bb_s, gct_s) = refs[:16]
    s_s = refs[16] if carry else None
    dv = dk
    nseg = rows // seg
    sh = int(math.log2(seg))
    assert 1 << sh == seg and seg % BASE_BLOCK == 0

    if carry:
        @pl.when(pl.program_id(1) == 0)
        def _():
            s_s[...] = s0_ref[...]

    rid = lax.broadcasted_iota(jnp.int32, (rows, 1), 0)
    live = jnp.bitwise_and(rid, seg - 1) < valid
    rseg = lax.shift_right_logical(rid, sh)
    ri = lax.broadcasted_iota(jnp.int32, (rows, rows), 0)
    ci = lax.broadcasted_iota(jnp.int32, (rows, rows), 1)
    same = lax.shift_right_logical(ri, sh) == lax.shift_right_logical(ci, sh)
    causal = jnp.logical_and(same, ci <= ri)
    strict = jnp.logical_and(same, ci < ri)
    base = lax.shift_right_logical(ri, 3) == lax.shift_right_logical(ci, 3)
    eye = (ri == ci).astype(F32)
    levels = []
    s = BASE_BLOCK
    while s < seg:
        rb = lax.shift_right_logical(ri, int(math.log2(s)))
        cb = lax.shift_right_logical(ci, int(math.log2(s)))
        levels.append(jnp.logical_and(jnp.bitwise_and(rb, 1) == 1, cb == rb - 1))
        s *= 2

    a = pa_ref[...] + dtb_ref[...]
    softplus = jnp.maximum(a, 0.0) + jnp.log1p(jnp.exp(-jnp.abs(a)))
    glog = jnp.where(live, -jnp.exp(alog_ref[...]) * softplus, 0.0)
    beta = jnp.where(live, _sigmoid(pb_ref[...]), 0.0)
    gc = _dot(causal.astype(F32), glog, precision=HIGHEST)
    gcb_s[...] = _expand(gc, ex_ref[...], 3)
    bb_s[...] = _expand(beta, ex_ref[...], 2)
    gct_s[...] = gc.T

    def widen(x):
        if rows <= dk:
            return x[:, :rows]
        return jnp.concatenate([x] * (rows // dk), axis=1)

    def group(gi, carry_unused):
        hs = [gi * unroll + j for j in range(unroll)]
        los = [pl.multiple_of(h * dk, dk) for h in hs]
        each = range(unroll)
        qs = [q_ref[:, pl.ds(lo, dk)] for lo in los]
        ks = [k_ref[:, pl.ds(lo, dk)] for lo in los]
        vs = [v_ref[:, pl.ds(lo, dk)] for lo in los]
        zs = [z_ref[:, pl.ds(lo, dv)] for lo in los]
        gcbs = [gcb_s[:, pl.ds(lo, dk)] for lo in los]
        bbs = [bb_s[:, pl.ds(lo, dk)] for lo in los]
        grow = [gct_s[pl.ds(h, 1), :] for h in hs]
        if carry:
            sts = [s_s[h] for h in hs]
        qs = [q * lax.rsqrt(jnp.sum(q * q, axis=-1, keepdims=True) + 1e-6) * (dk ** -0.5) for q in qs]
        ks = [k * lax.rsqrt(jnp.sum(k * k, axis=-1, keepdims=True) + 1e-6) for k in ks]
        egcs = [jnp.exp(g) for g in gcbs]
        decays = [jnp.where(causal, jnp.exp(jnp.where(causal, widen(gcbs[j]) - grow[j], 0.0)), 0.0)
                  for j in each]
        kbf = [k.astype(BF16) for k in ks]
        qkks = [_dot_nt(jnp.concatenate([qs[j].astype(BF16), kbf[j]], axis=0), kbf[j]) for j in each]
        qks = [qkks[j][:rows] * decays[j] for j in each]
        amats = [jnp.where(strict, widen(bbs[j]) * decays[j] * qkks[j][rows:], 0.0) for j in each]
        pws = [-jnp.where(base, a, 0.0) for a in amats]
        invs = [eye + p for p in pws]
        for _ in range(2):
            pws = [_mm(p, p) for p in pws]
            invs = [invs[j] + _mm(pws[j], invs[j]) for j in each]
        for lvl in levels:
            lows = [_mm(jnp.where(lvl, amats[j], 0.0), invs[j]) for j in each]
            invs = [invs[j] - _mm(invs[j], lows[j]) for j in each]
        sols = [_mm(invs[j], jnp.concatenate([bbs[j] * vs[j], bbs[j] * egcs[j] * ks[j]], axis=1))
                for j in each]
        wvs = [sl[:, :dv] for sl in sols]
        wks = [sl[:, dv:] for sl in sols]
        qds = [qs[j] * egcs[j] for j in each]
        if carry:
            us = [[] for _ in each]
            os_ = [[] for _ in each]
            for sg in range(nseg):
                r = slice(sg * seg, (sg + 1) * seg)
                last = slice((sg + 1) * seg - 1, (sg + 1) * seg)
                both = [_dot(jnp.concatenate([wks[j][r], qds[j][r]], axis=0).astype(BF16), sts[j].astype(BF16))
                        for j in each]
                u = [wvs[j][r] - both[j][:seg] for j in each]
                kend = [ks[j][r] * jnp.exp(gcbs[j][last] - gcbs[j][r]) for j in each]
                sts = [jnp.exp(gcbs[j][last]) * sts[j] + _dot_tn(kend[j].astype(BF16), u[j].astype(BF16))
                       for j in each]
                for j in each:
                    us[j].append(u[j])
                    os_[j].append(both[j][seg:])
            u_all = [x[0] if nseg == 1 else jnp.concatenate(x, axis=0) for x in us]
            o = [x[0] if nseg == 1 else jnp.concatenate(x, axis=0) for x in os_]
        else:
            lhs = [jnp.concatenate([wks[j], qds[j]], axis=0).astype(BF16) for j in each]
            u_all = [jnp.zeros((rows, dv), F32) for _ in each]
            o = [jnp.zeros((rows, dv), F32) for _ in each]
            news = []
            for sg in range(nseg):
                last = slice((sg + 1) * seg - 1, (sg + 1) * seg)
                mine = rseg == sg
                st0 = [s0_ref[sg, h] for h in hs]
                both = [_dot(lhs[j], st0[j].astype(BF16)) for j in each]
                u_m = [jnp.where(mine, wvs[j] - both[j][:rows], 0.0) for j in each]
                u_all = [u_all[j] + u_m[j] for j in each]
                o = [o[j] + jnp.where(mine, both[j][rows:], 0.0) for j in each]
                kend = [jnp.where(mine, ks[j] * jnp.exp(jnp.where(mine, gcbs[j][last] - gcbs[j], 0.0)), 0.0)
                        for j in each]
                news.append([jnp.exp(gcbs[j][last]) * st0[j] + _dot_tn(kend[j].astype(BF16), u_m[j].astype(BF16))
                             for j in each])
        o = [o[j] + _mm(qks[j], u_all[j]) for j in each]
        o = [o[j] * lax.rsqrt(jnp.mean(o[j] * o[j], axis=-1, keepdims=True) + EPS) * on_ref[...] * _silu(zs[j])
             for j in each]
        for j in each:
            o_ref[:, pl.ds(los[j], dv)] = o[j].astype(o_ref.dtype)
            if carry:
                s_s[hs[j]] = sts[j]
            else:
                for sg in range(nseg):
                    sn_ref[sg, hs[j]] = news[sg][j]
        return carry_unused

    assert heads % unroll == 0
    lax.fori_loop(0, heads // unroll, group, 0)

    if carry:
        @pl.when(pl.program_id(1) == pl.num_programs(1) - 1)
        def _():
            sn_ref[...] = s_s[...]


def _gdn(qkv, proj, s0, a_log, dt_bias, out_norm, *, cols, nseq, seqlen, rows, seg, valid, carry):
    _, heads, dk, dv = s0.shape
    assert dk == LANES and dv == LANES and heads <= LANES and rows % seg == 0
    hk = heads * dk
    expand = np.zeros((LANES, hk), np.float32)
    for h in range(heads):
        expand[h, h * dk:(h + 1) * dk] = 1.0
    lane_pad = lambda t: jnp.pad(t.reshape(1, heads).astype(F32), ((0, 0), (0, LANES - heads)))
    col_z, col_a, col_b = cols
    assert col_z % hk == 0 and col_a % LANES == 0 and col_b % LANES == 0
    if carry:
        nblk = seqlen // rows
        grid = (nseq, nblk)
        rmap = lambda i, c: i * nblk + c
        smap = lambda i, c: (i, 0, 0, 0)
        sblock = (None, heads, dk, dv)
        sem = ("parallel", "arbitrary")
    else:
        spb = rows // seg
        grid = (nseq // spb, 1)
        rmap = lambda i, c: i
        smap = lambda i, c: (i, 0, 0, 0)
        sblock = (spb, heads, dk, dv)
        sem = ("parallel", "arbitrary")
    rspec = lambda w, cb: pl.BlockSpec((rows, w), lambda i, c: (rmap(i, c), cb))
    full = lambda shape: pl.BlockSpec(shape, lambda i, c: (0,) * len(shape))
    kern = functools.partial(_gdn_kernel, heads=heads, dk=dk, rows=rows, seg=seg, valid=valid,
                             carry=carry, unroll=min(heads, 8) if carry else min(heads, 4))
    scratch = [pltpu.VMEM((rows, hk), F32), pltpu.VMEM((rows, hk), F32), pltpu.VMEM((LANES, rows), F32)]
    if carry:
        scratch.append(pltpu.VMEM((heads, dk, dv), F32))
    t = nseq * seqlen
    return pl.pallas_call(
        kern,
        grid=grid,
        in_specs=[
            rspec(hk, 0), rspec(hk, 1), rspec(hk, 2),
            rspec(LANES, col_a // LANES), rspec(LANES, col_b // LANES), rspec(hk, col_z // hk),
            pl.BlockSpec(sblock, smap),
            full((1, LANES)), full((1, LANES)), full((1, dv)), full((LANES, hk)),
        ],
        out_specs=[rspec(hk, 0), pl.BlockSpec(sblock, smap)],
        out_shape=[jax.ShapeDtypeStruct((t, heads * dv), BF16), jax.ShapeDtypeStruct(s0.shape, F32)],
        scratch_shapes=scratch,
        compiler_params=_cparams(*sem),
        name="gated_deltanet",
    )(qkv, qkv, qkv, proj, proj, proj, s0, lane_pad(a_log), lane_pad(dt_bias),
      out_norm.reshape(1, dv).astype(F32), jnp.asarray(expand, dtype=BF16))


def _ret_kernel(*refs, hps, dk, rows, seg, carry):
    (q_ref, k_ref, v_ref, g_ref, cos_ref, sin_ref, s0_ref, dec_ref, qs_ref, ks_ref, ge_ref,
     o_ref, sn_ref) = refs[:13]
    s_s = refs[13] if carry else None
    dv = dk
    half = dk // 2
    nseg = rows // seg
    sh = int(math.log2(seg))
    hstep = pl.program_id(1) if not carry else 0

    if carry:
        @pl.when(pl.program_id(1) == 0)
        def _():
            s_s[...] = s0_ref[...]

    cos = cos_ref[...]
    sin = sin_ref[...]
    rseg = lax.shift_right_logical(lax.broadcasted_iota(jnp.int32, (rows, 1), 0), sh)

    def rot(x):
        x1, x2 = x[:, :half], x[:, half:]
        return jnp.concatenate([x1 * cos - x2 * sin, x1 * sin + x2 * cos], axis=1)

    def head(hh, carry_unused):
        h = hstep * hps + hh
        lo = pl.multiple_of(h * dk, dk)
        q = rot(q_ref[:, pl.ds(lo, dk)])
        k = rot(k_ref[:, pl.ds(lo, dk)]) * (dk ** -0.5)
        v_bf = v_ref[:, pl.ds(lo, dv)].astype(BF16)
        qk = _dot_nt(q.astype(BF16), k.astype(BF16)) * dec_ref[h]
        qd = (q * qs_ref[:, pl.ds(lo, dk)]).astype(BF16)
        ke = k * ks_ref[:, pl.ds(lo, dk)]
        if carry:
            st = s_s[h]
            os_ = []
            for sg in range(nseg):
                r0 = sg * seg
                os_.append(_dot(qd[r0:r0 + seg], st.astype(BF16)))
                st = ge_ref[h] * st + _dot_tn(ke[r0:r0 + seg].astype(BF16), v_bf[r0:r0 + seg])
            s_s[h] = st
            o = os_[0] if nseg == 1 else jnp.concatenate(os_, axis=0)
        else:
            o = jnp.zeros((rows, dv), F32)
            for sg in range(nseg):
                mine = rseg == sg
                st = s0_ref[sg, hh]
                o = o + jnp.where(mine, _dot(qd, st.astype(BF16)), 0.0)
                sn_ref[sg, hh] = ge_ref[h] * st + _dot_tn(jnp.where(mine, ke, 0.0).astype(BF16), v_bf)
        o = o + _dot(qk.astype(BF16), v_bf)
        mu = jnp.mean(o, axis=-1, keepdims=True)
        var = jnp.mean(jnp.square(o - mu), axis=-1, keepdims=True)
        o = (o - mu) * lax.rsqrt(var + EPS) * _silu(g_ref[:, pl.ds(lo, dv)])
        o_ref[:, pl.ds(lo, dv)] = o.astype(o_ref.dtype)
        return carry_unused

    lax.fori_loop(0, hps, head, 0)

    if carry:
        @pl.when(pl.program_id(1) == pl.num_programs(1) - 1)
        def _():
            sn_ref[...] = s_s[...]


def _retention(proj, s0, *, col_q, nseq, seqlen, rows, seg, valid, pos0, carry):
    _, heads, dk, dv = s0.shape
    assert dk == dv and rows % seg == 0
    hw = heads * dk
    half = dk // 2
    assert col_q % hw == 0
    cq = col_q // hw
    npos = seqlen if carry else rows
    offs = np.arange(npos) if carry else np.arange(rows) % seg
    inv_freq = ROPE_BASE ** (-np.arange(half, dtype=np.float64) / half)
    ang = (pos0 + offs.astype(np.float64))[:, None] * inv_freq[None, :]
    cos, sin = np.cos(ang).astype(np.float32), np.sin(ang).astype(np.float32)
    log_gamma = np.log1p(-np.exp2(-5.0 - np.arange(heads, dtype=np.float64)))
    pin = np.arange(rows) % seg
    steps = np.minimum(pin + 1, valid).astype(np.float64)
    gc = log_gamma[:, None] * steps[None, :]
    gl = log_gamma * min(seg, valid)
    sameseg = (np.arange(rows)[:, None] // seg) == (np.arange(rows)[None, :] // seg)
    causal = np.logical_and(sameseg, np.tril(np.ones((rows, rows), bool)))
    diff = gc[:, :, None] - gc[:, None, :]
    dec = np.where(causal[None], np.exp(np.where(causal[None], diff, 0.0)), 0.0).astype(np.float32)
    qs = np.repeat(np.exp(gc).T, dk, axis=1).astype(np.float32)
    ks = np.repeat(np.exp(gl[:, None] - gc).T, dk, axis=1).astype(np.float32)
    ge = np.broadcast_to(np.exp(gl)[:, None, None], (heads, 1, dv)).astype(np.float32)
    if carry:
        nblk = seqlen // rows
        hps = heads
        grid = (nseq, nblk)
        rmap = lambda i, c: i * nblk + c
        tmap = lambda i, c: (c, 0)
        smap = lambda i, c: (i, 0, 0, 0)
        sblock = (None, heads, dk, dv)
    else:
        spb = rows // seg
        hps = max(1, heads // 2)
        grid = (nseq // spb, heads // hps)
        rmap = lambda i, c: i
        tmap = lambda i, c: (0, 0)
        smap = lambda i, c: (i, c, 0, 0)
        sblock = (spb, hps, dk, dv)
    rspec = lambda cb: pl.BlockSpec((rows, hw), lambda i, c: (rmap(i, c), cb))
    full = lambda shape: pl.BlockSpec(shape, lambda i, c: (0,) * len(shape))
    kern = functools.partial(_ret_kernel, hps=hps, dk=dk, rows=rows, seg=seg, carry=carry)
    t = nseq * seqlen
    return pl.pallas_call(
        kern,
        grid=grid,
        in_specs=[
            rspec(cq), rspec(cq + 1), rspec(cq + 2), rspec(cq + 3),
            pl.BlockSpec((rows, half), tmap), pl.BlockSpec((rows, half), tmap),
            pl.BlockSpec(sblock, smap),
            full((heads, rows, rows)), full((rows, hw)), full((rows, hw)), full((heads, 1, dv)),
        ],
        out_specs=[rspec(0), pl.BlockSpec(sblock, smap)],
        out_shape=[jax.ShapeDtypeStruct((t, heads * dv), BF16), jax.ShapeDtypeStruct(s0.shape, F32)],
        scratch_shapes=[pltpu.VMEM((heads, dk, dv), F32)] if carry else [],
        compiler_params=_cparams("parallel", "arbitrary"),
        name="retention",
    )(proj, proj, proj, proj, jnp.asarray(cos), jnp.asarray(sin), s0, jnp.asarray(dec), jnp.asarray(qs),
      jnp.asarray(ks), jnp.asarray(ge))


def _pack_w_in(w, gdn_c, heads_a, gdn_v):
    off_a = gdn_c
    off_b = off_a + heads_a
    off_z = off_b + heads_a
    off_r = off_z + gdn_v
    zpad = jnp.zeros((w.shape[0], LANES - heads_a), w.dtype)
    return jnp.concatenate([w[:, :off_a], w[:, off_z:off_r], w[:, off_r:], w[:, off_a:off_b], zpad,
                            w[:, off_b:off_z], zpad], axis=1)


def kernel(x_prompt, x_sample, state_gdn_conv, state_gdn, state_ret, state_pool, state_ffn_conv, p_prompt, p_sample, norm_mix, norm_ffn, norm_ple, norm_final, w_in, gdn_conv_w, gdn_a_log, gdn_dt_bias, gdn_out_norm, w_out, pool_w, pool_scale, ffn_w_up, ffn_conv_w, ffn_conv_b, ffn_w_down, ple_w_gate, ple_w_proj):
    bp, lp, d = x_prompt.shape
    bs, ls, _ = x_sample.shape
    depth = p_prompt.shape[0]
    tp, ts = bp * lp, bs * ls
    t = tp + ts
    tm = ts
    assert lp % tm == 0 and tm % SUBLANES == 0
    ntp = tp // tm
    _, _, heads_a, dk_a, dv_a = state_gdn.shape
    _, _, heads_r, dk_r, dv_r = state_ret.shape
    gdn_c = state_gdn_conv.shape[-1]
    gdn_v = heads_a * dv_a
    ret_w = heads_r * dk_r
    gtaps = gdn_conv_w.shape[1]
    ftaps = ffn_conv_w.shape[1]
    pool_past = state_pool.shape[2]
    f2 = ffn_w_up.shape[-1]
    assert min(lp, ls) >= max(gtaps, ftaps) - 1 and pool_past == POOL_HIST - 1 and lp % SUBLANES == 0
    assert lp % CHUNK == 0 and ls <= BASE_BLOCK and gdn_c == 3 * gdn_v and heads_a * dk_a == gdn_v
    col_z, col_r = gdn_c, gdn_c + gdn_v
    col_a = col_r + 4 * ret_w
    col_b = col_a + LANES
    n_proj = col_b + LANES

    rows_p = lambda a: a.reshape((tp,) + a.shape[2:])
    rows_s = lambda a: jnp.swapaxes(a, 0, 1).reshape((ts,) + a.shape[2:])
    join = lambda a, b: jnp.concatenate([rows_p(a), rows_s(b)], axis=0)
    seq_p = lambda a: a[:tp].reshape((bp, lp) + a.shape[1:])
    seq_s = lambda a: jnp.swapaxes(a[tp:].reshape((ls, bs) + a.shape[1:]), 0, 1)

    tps = lp // tm
    tiles = lambda a: a.reshape((t // tm, tm) + a.shape[1:])

    def hist_p(a, nrows):
        tl = tiles(a)[:ntp, tm - nrows:]
        prev = jnp.concatenate([jnp.zeros_like(tl[:1]), tl[:-1]], axis=0)
        first = (jnp.arange(ntp) % tps == 0)[:, None, None]
        return jnp.where(first, 0.0, prev)

    def tail_p(a, nrows):
        return tiles(a)[tps - 1:ntp:tps, tm - nrows:]

    def hist_s(state, nrows):
        return jnp.swapaxes(state[:, -nrows:], 0, 1).reshape(1, nrows * bs, state.shape[-1])

    def pad_seq(a):
        a = seq_s(a)
        a = jnp.pad(a, ((0, 0), (0, BASE_BLOCK - ls)) + ((0, 0),) * (a.ndim - 2))
        return a.reshape((bs * BASE_BLOCK,) + a.shape[2:])

    unpad_seq = lambda a: rows_s(a.reshape((bs, BASE_BLOCK) + a.shape[1:])[:, :ls])

    xr = join(x_prompt, x_sample)
    pj = [join(p_prompt[i], p_sample[i]).astype(BF16) for i in range(depth)]
    pool_w_bf = pool_w.astype(BF16)
    w_down_bf = ffn_w_down.astype(BF16)
    w_proj_bf = ple_w_proj.astype(BF16)
    rows_prompt = _pick(lp, 256, CHUNK)
    spb = max(1, CHUNK // BASE_BLOCK)
    assert bs % spb == 0

    new_gdn_conv_p, new_gdn_p, new_ret_p, new_pool_p, new_ffn_p = [], [], [], [], []
    new_gdn_conv_s, new_gdn_s, new_ret_s, new_pool_s, new_ffn_s = [], [], [], [], []
    for i in range(depth):
        j = i // 2
        if i % 2 == 0:
            (h,) = _rmsnorm(xr, norm_mix[i], [BF16])
            proj = _matmul(h, _pack_w_in(w_in[j], gdn_c, heads_a, gdn_v), name="in_proj")
            qkv = _qkv_conv(proj, gdn_conv_w[j], tm=lp, stride=1, t0=0, nt=bp)
            qkv = _qkv_conv(proj, gdn_conv_w[j], tm=tm, stride=bs, t0=ntp, nt=1,
                            hist=hist_s(state_gdn_conv[j], gtaps - 1), prev=qkv)
            cols = (col_z, col_a, col_b)
            o_gdn_p, s_a_p = _gdn(qkv, proj, jnp.zeros((bp,) + state_gdn.shape[2:], F32), gdn_a_log[j],
                                  gdn_dt_bias[j], gdn_out_norm[j], cols=cols, nseq=bp, seqlen=lp,
                                  rows=rows_prompt, seg=CHUNK, valid=CHUNK, carry=True)
            o_ret_p, s_b_p = _retention(proj, jnp.zeros((bp,) + state_ret.shape[2:], F32), col_q=col_r,
                                        nseq=bp, seqlen=lp, rows=rows_prompt, seg=CHUNK, valid=CHUNK, pos0=0,
                                        carry=True)
            proj_s = pad_seq(proj)
            o_gdn_s, s_a_s = _gdn(pad_seq(qkv), proj_s, state_gdn[j], gdn_a_log[j], gdn_dt_bias[j],
                                  gdn_out_norm[j], cols=cols, nseq=bs, seqlen=BASE_BLOCK,
                                  rows=spb * BASE_BLOCK, seg=BASE_BLOCK, valid=ls, carry=False)
            o_ret_s, s_b_s = _retention(proj_s, state_ret[j], col_q=col_r, nseq=bs, seqlen=BASE_BLOCK,
                                        rows=spb * BASE_BLOCK, seg=BASE_BLOCK, valid=ls, pos0=PAST_LEN,
                                        carry=False)
            mixed = jnp.concatenate([
                jnp.concatenate([o_gdn_p, o_ret_p], axis=1),
                jnp.concatenate([unpad_seq(o_gdn_s), unpad_seq(o_ret_s)], axis=1)], axis=0)
            xr = _matmul(mixed, w_out[j], res=xr, name="out_proj")
            new_gdn_conv_p.append(tail_p(proj, gtaps - 1)[:, :, :gdn_c])
            new_gdn_conv_s.append(jnp.concatenate([state_gdn_conv[j], seq_s(proj[:, :gdn_c])],
                                                  axis=1)[:, -(gtaps - 1):])
            new_gdn_p.append(s_a_p)
            new_gdn_s.append(s_a_s)
            new_ret_p.append(s_b_p)
            new_ret_s.append(s_b_s)
        else:
            h32, = _rmsnorm(xr, norm_mix[i], [F32])
            h_s = seq_s(h32)
            pos = jnp.concatenate([jnp.tile(jnp.arange(lp, dtype=jnp.int32), bp),
                                   jnp.repeat(PAST_LEN + jnp.arange(ls, dtype=jnp.int32), bs)])
            cnt = jnp.stack([jnp.minimum(w, pos + 1).astype(F32) for w in POOL_WINDOWS])[:, :, None]
            zero_row = jnp.zeros((bs, 1, d), F32)
            kw = dict(tm=tm)
            xn = _pool_mixer(h32, hist_p(h32, POOL_HIST), cnt, pool_w_bf[j], pool_scale[j], xr, stride=1,
                             t0=0, nt=ntp, **kw)
            xr = _pool_mixer(h32, hist_s(jnp.concatenate([zero_row, state_pool[j]], axis=1), POOL_HIST), cnt,
                             pool_w_bf[j], pool_scale[j], xr, stride=bs, t0=ntp, nt=1, prev=xn, **kw)
            new_pool_p.append(tail_p(h32, pool_past))
            new_pool_s.append(jnp.concatenate([state_pool[j], h_s], axis=1)[:, -pool_past:])
        (h,) = _rmsnorm(xr, norm_ffn[i], [BF16])
        act, sg_p, sv_p = _ffn_up_act(h, ffn_w_up[i], ffn_conv_w[i], ffn_conv_b[i], tm=lp, stride=1, t0=0, nt=bp)
        act, sg_s, sv_s = _ffn_up_act(h, ffn_w_up[i], ffn_conv_w[i], ffn_conv_b[i], tm=tm, stride=bs, t0=ntp,
                                      nt=1, hist=hist_s(state_ffn_conv[i], ftaps - 1), prev=act)
        xr = _matmul(act, w_down_bf[i], res=xr, tm=1088, name="ffn_down")
        new_ffn_p.append(jnp.concatenate([sg_p, sv_p], axis=-1))
        up_s = jnp.concatenate([sg_s, sv_s], axis=-1).reshape(ftaps - 1, bs, f2)
        new_ffn_s.append(jnp.swapaxes(up_s, 0, 1))
        (h,) = _rmsnorm(xr, norm_ple[i], [BF16])
        xr = _matmul(h, ple_w_gate[i], res=xr, p=pj[i], wp=w_proj_bf[i], name="ple")
    (out,) = _rmsnorm(xr, norm_final, [F32])
    st = lambda lst: jnp.stack(lst)
    return (seq_p(out), seq_s(out),
            st(new_gdn_conv_p), st(new_gdn_p), st(new_ret_p), st(new_pool_p), st(new_ffn_p),
            st(new_gdn_conv_s), st(new_gdn_s), st(new_ret_s), st(new_pool_s), st(new_ffn_s))
```

```python
import functools
import math

import numpy as np
import jax
import jax.numpy as jnp
from jax import lax
from jax.experimental import pallas as pl
from jax.experimental.pallas import tpu as pltpu

F32 = jnp.float32
BF16 = jnp.bfloat16
HIGHEST = lax.Precision.HIGHEST

EPS = 1e-6
CHUNK = 64
PAST_LEN = 16384
ROPE_BASE = 10000.0
POOL_WINDOWS = (2, 4, 8, 16)
POOL_HIST = 16
LANES = 128
SUBLANES = 8
VMEM_LIMIT = 56 * 1024 * 1024
BASE_BLOCK = 8


def _pick(n, pref, align=LANES):
    if n <= pref:
        return n
    t = (pref // align) * align
    while t >= align:
        if n % t == 0:
            return t
        t -= align
    return n


def _cparams(*sem):
    return pltpu.CompilerParams(dimension_semantics=sem, vmem_limit_bytes=VMEM_LIMIT)


def _silu(x):
    return x * (1.0 / (1.0 + jnp.exp(-x)))


def _sigmoid(x):
    return 1.0 / (1.0 + jnp.exp(-x))


def _dot(a, b, **kw):
    return jnp.dot(a, b, preferred_element_type=F32, **kw)


def _dot_nt(a, b, **kw):
    return lax.dot_general(a, b, (((1,), (1,)), ((), ())), preferred_element_type=F32, **kw)


def _dot_tn(a, b, **kw):
    return lax.dot_general(a, b, (((0,), (0,)), ((), ())), preferred_element_type=F32, **kw)


def _mm(a, b):
    return _dot(a.astype(BF16), b.astype(BF16))


def _expand(x, e_bf, passes):
    acc = None
    for _ in range(passes):
        piece = x.astype(BF16)
        term = _dot(piece, e_bf)
        acc = term if acc is None else acc + term
        x = x - piece.astype(F32)
    return acc


def _rmsnorm_kernel(x_ref, g_ref, *o_refs):
    x = x_ref[...]
    y = x * lax.rsqrt(jnp.mean(x * x, axis=-1, keepdims=True) + EPS) * g_ref[...]
    for o_ref in o_refs:
        o_ref[...] = y.astype(o_ref.dtype)


def _rmsnorm(x, gain, dtypes):
    m, d = x.shape
    tm = _pick(m, 256, SUBLANES)
    return pl.pallas_call(
        _rmsnorm_kernel,
        grid=(m // tm,),
        in_specs=[pl.BlockSpec((tm, d), lambda i: (i, 0)), pl.BlockSpec((1, d), lambda i: (0, 0))],
        out_specs=[pl.BlockSpec((tm, d), lambda i: (i, 0)) for _ in dtypes],
        out_shape=[jax.ShapeDtypeStruct((m, d), dt) for dt in dtypes],
        compiler_params=_cparams("parallel"),
        name="rmsnorm",
    )(x, gain.reshape(1, d))


def _mm_kernel(*refs, epilogue):
    x_ref, w_ref = refs[0], refs[1]
    o_ref = refs[-1]
    acc = _dot(x_ref[...], w_ref[...].astype(BF16))
    if epilogue == "res":
        acc = refs[2][...] + acc
    elif epilogue == "ple":
        res_ref, p_ref, wp_ref = refs[2:5]
        acc = res_ref[...] + _sigmoid(acc) * _dot(p_ref[...], wp_ref[...])
    o_ref[...] = acc.astype(o_ref.dtype)


def _wspec(w, layer, rows, tc, cmap):
    if w.ndim == 2:
        return pl.BlockSpec((rows, tc), lambda i, j: (0, cmap(j)))
    return pl.BlockSpec((None, rows, tc), lambda i, j: (layer, 0, cmap(j)))


def _matmul(x, w, *, layer=0, res=None, p=None, wp=None, tm=2176, tn=256, out_dtype=F32, name="matmul"):
    m, kdim = x.shape
    n = w.shape[-1]
    tm = _pick(m, tm, 2 * SUBLANES)
    tn = _pick(n, tn)
    epilogue = "none" if res is None else ("res" if p is None else "ple")
    in_specs = [pl.BlockSpec((tm, kdim), lambda i, j: (i, 0), pipeline_mode=pl.Buffered(1)),
                _wspec(w, layer, kdim, tn, lambda j: j)]
    args = [x, w]
    if res is not None:
        in_specs.append(pl.BlockSpec((tm, tn), lambda i, j: (i, j)))
        args.append(res)
    if p is not None:
        pd = p.shape[1]
        in_specs += [pl.BlockSpec((tm, pd), lambda i, j: (i, 0), pipeline_mode=pl.Buffered(1)),
                     _wspec(wp, layer, pd, tn, lambda j: j)]
        args += [p, wp]
    return pl.pallas_call(
        functools.partial(_mm_kernel, epilogue=epilogue),
        grid=(m // tm, n // tn),
        in_specs=in_specs,
        out_specs=pl.BlockSpec((tm, tn), lambda i, j: (i, j)),
        out_shape=jax.ShapeDtypeStruct((m, n), out_dtype),
        compiler_params=_cparams("parallel", "arbitrary"),
        name=name,
    )(*args)


def _causal_conv(cur, w_ref, h_ref, *, taps, stride, tm):
    acc = w_ref[taps - 1:taps, :] * cur
    if h_ref is not None:
        ext = jnp.concatenate([h_ref[...], cur], axis=0)
        for j in range(taps - 1):
            acc = acc + w_ref[j:j + 1, :] * ext[j * stride:j * stride + tm, :]
    else:
        row = lax.broadcasted_iota(jnp.int32, (tm, 1), 0)
        for j in range(taps - 1):
            back = (taps - 1 - j) * stride
            acc = acc + w_ref[j:j + 1, :] * jnp.where(row >= back, pltpu.roll(cur, back, axis=0), 0.0)
    return acc


def _ffn_up_kernel(*refs, taps, stride, tm, has_hist, aliased):
    x_ref, wg_ref, wv_ref, cg_ref, cv_ref, bg_ref, bv_ref = refs[:7]
    hg_ref, hv_ref = (refs[7], refs[8]) if has_hist else (None, None)
    o_ref, sg_ref, sv_ref = refs[7 + 2 * has_hist + aliased:]
    x = x_ref[...]
    keep = (taps - 1) * stride

    def branch(w_ref, c_ref, b_ref, h_ref, s_ref):
        up = _dot(x, w_ref[...].astype(BF16))
        s_ref[...] = up[tm - keep:, :]
        return _causal_conv(up, c_ref, h_ref, taps=taps, stride=stride, tm=tm) + b_ref[...]

    gate = branch(wg_ref, cg_ref, bg_ref, hg_ref, sg_ref)
    val = branch(wv_ref, cv_ref, bv_ref, hv_ref, sv_ref)
    o_ref[...] = (_silu(gate) * val).astype(o_ref.dtype)


def _ffn_up_act(h, w_up, conv_w, conv_b, *, layer, tm, stride, t0, nt, hist=None, prev=None):
    t, kdim = h.shape
    f2 = w_up.shape[-1]
    f = f2 // 2
    taps = conv_w.shape[1]
    keep = (taps - 1) * stride
    tc = _pick(f, 256)
    nc = f // tc
    has_hist = hist is not None
    aliased = prev is not None
    kern = functools.partial(_ffn_up_kernel, taps=taps, stride=stride, tm=tm, has_hist=has_hist,
                             aliased=aliased)
    lo = lambda j: j
    hi = lambda j: j + nc
    conv_b = conv_b.reshape(conv_b.shape[0], 1, f2)
    in_specs = [
        pl.BlockSpec((tm, kdim), lambda i, j: (i + t0, 0), pipeline_mode=pl.Buffered(1)),
        _wspec(w_up, layer, kdim, tc, lo), _wspec(w_up, layer, kdim, tc, hi),
        _wspec(conv_w, layer, taps, tc, lo), _wspec(conv_w, layer, taps, tc, hi),
        _wspec(conv_b, layer, 1, tc, lo), _wspec(conv_b, layer, 1, tc, hi),
    ]
    args = [h, w_up, w_up, conv_w, conv_w, conv_b, conv_b]
    if has_hist:
        in_specs += [pl.BlockSpec((None, keep, tc), lambda i, j: (i, 0, j)),
                     pl.BlockSpec((None, keep, tc), lambda i, j: (i, 0, j + nc))]
        args += [hist, hist]
    if aliased:
        in_specs.append(pl.BlockSpec(memory_space=pl.ANY))
        args.append(prev)
    sspec = pl.BlockSpec((None, keep, tc), lambda i, j: (i, 0, j))
    return pl.pallas_call(
        kern,
        grid=(nt, nc),
        in_specs=in_specs,
        out_specs=[pl.BlockSpec((tm, tc), lambda i, j: (i + t0, j)), sspec, sspec],
        out_shape=[jax.ShapeDtypeStruct((t, f), BF16), jax.ShapeDtypeStruct((nt, keep, f), F32),
                   jax.ShapeDtypeStruct((nt, keep, f), F32)],
        input_output_aliases={len(args) - 1: 0} if aliased else {},
        compiler_params=_cparams("parallel", "arbitrary"),
        name="ffn_up_conv_act",
    )(*args)


def _proj_conv_kernel(*refs, taps, stride, tm, has_hist, aliased):
    x_ref, w_ref, c_ref = refs[:3]
    h_ref = refs[3] if has_hist else None
    o_ref, s_ref = refs[3 + has_hist + aliased:]
    up = _dot(x_ref[...], w_ref[...].astype(BF16))
    s_ref[...] = up[tm - (taps - 1) * stride:, :]
    o_ref[...] = _silu(_causal_conv(up, c_ref, h_ref, taps=taps, stride=stride, tm=tm))


def _proj_conv(h, w, conv_w, *, layer, tm, stride, t0, nt, hist=None, prev=None):
    t, kdim = h.shape
    taps, c = conv_w.shape[-2:]
    keep = (taps - 1) * stride
    tc = _pick(c, 256)
    has_hist = hist is not None
    aliased = prev is not None
    kern = functools.partial(_proj_conv_kernel, taps=taps, stride=stride, tm=tm, has_hist=has_hist,
                             aliased=aliased)
    same = lambda j: j
    in_specs = [pl.BlockSpec((tm, kdim), lambda i, j: (i + t0, 0), pipeline_mode=pl.Buffered(1)),
                _wspec(w, layer, kdim, tc, same), _wspec(conv_w, layer, taps, tc, same)]
    args = [h, w, conv_w]
    if has_hist:
        in_specs.append(pl.BlockSpec((None, keep, tc), lambda i, j: (i, 0, j)))
        args.append(hist)
    if aliased:
        in_specs.append(pl.BlockSpec(memory_space=pl.ANY))
        args.append(prev)
    return pl.pallas_call(
        kern,
        grid=(nt, c // tc),
        in_specs=in_specs,
        out_specs=[pl.BlockSpec((tm, tc), lambda i, j: (i + t0, j)),
                   pl.BlockSpec((None, keep, tc), lambda i, j: (i, 0, j))],
        out_shape=[jax.ShapeDtypeStruct((t, c), F32), jax.ShapeDtypeStruct((nt, keep, c), F32)],
        input_output_aliases={len(args) - 1: 0} if aliased else {},
        compiler_params=_cparams("parallel", "arbitrary"),
        name="qkv_proj_conv",
    )(*args)


def _pool_kernel(*refs, stride, tm, aliased):
    h_ref, hist_ref, cnt_ref, w_ref, sc_ref, x_ref = refs[:6]
    o_ref, e_ref, acc_ref = refs[6 + aliased:]
    g = pl.program_id(1)
    off = POOL_HIST * stride
    window = lax.shift_left(jnp.int32(2), g)
    e_ref[0:off, :] = hist_ref[...]
    e_ref[off:off + tm, :] = h_ref[...]
    acc_ref[...] = e_ref[off:off + tm, :] + e_ref[off - stride:off - stride + tm, :]
    for j in range(2, max(POOL_WINDOWS)):
        @pl.when(j < window)
        def _(j=j):
            acc_ref[...] += e_ref[off - j * stride:off - j * stride + tm, :]
    pooled = acc_ref[...] / cnt_ref[...] - h_ref[...]
    y = _dot(pooled.astype(BF16), w_ref[...]) * sc_ref[...]
    o_ref[...] = x_ref[...] + y


def _pool_mixer(h, hist, cnt, pool_w, scale, x, *, tm, stride, t0, nt, prev=None):
    t, d = h.shape
    g = pool_w.shape[0]
    dg = d // g
    hr = POOL_HIST * stride
    aliased = prev is not None
    kern = functools.partial(_pool_kernel, stride=stride, tm=tm, aliased=aliased)
    in_specs = [
        pl.BlockSpec((tm, dg), lambda i, j: (i + t0, j)),
        pl.BlockSpec((None, hr, dg), lambda i, j: (i, 0, j)),
        pl.BlockSpec((None, tm, 1), lambda i, j: (j, i + t0, 0)),
        pl.BlockSpec((None, dg, dg), lambda i, j: (j, 0, 0)),
        pl.BlockSpec((1, dg), lambda i, j: (0, j)),
        pl.BlockSpec((tm, dg), lambda i, j: (i + t0, j)),
    ]
    args = [h, hist, cnt, pool_w, scale.reshape(1, d), x]
    if aliased:
        in_specs.append(pl.BlockSpec(memory_space=pl.ANY))
        args.append(prev)
    return pl.pallas_call(
        kern,
        grid=(nt, g),
        in_specs=in_specs,
        out_specs=pl.BlockSpec((tm, dg), lambda i, j: (i + t0, j)),
        out_shape=jax.ShapeDtypeStruct((t, d), F32),
        scratch_shapes=[pltpu.VMEM((hr + tm, dg), F32), pltpu.VMEM((tm, dg), F32)],
        input_output_aliases={6: 0} if aliased else {},
        compiler_params=_cparams("parallel", "arbitrary"),
        name="pool_mixer",
    )(*args)


def _gdn_kernel(*refs, heads, dk, rows, seg, valid, carry, unroll, aliased):
    q_ref, k_ref, v_ref, pa_ref, pb_ref, z_ref, s0_ref, alog_ref, dtb_ref, on_ref, ex_ref = refs[:11]
    o_ref, sn_ref, gcb_s, bb_s, gct_s = refs[11 + aliased:16 + aliased]
    s_s = refs[16 + aliased] if carry else None
    dv = dk
    nseg = rows // seg
    sh = int(math.log2(seg))
    assert 1 << sh == seg and seg % BASE_BLOCK == 0

    if carry:
        @pl.when(pl.program_id(1) == 0)
        def _():
            s_s[...] = s0_ref[...]

    rid = lax.broadcasted_iota(jnp.int32, (rows, 1), 0)
    live = jnp.bitwise_and(rid, seg - 1) < valid
    rseg = lax.shift_right_logical(rid, sh)
    ri = lax.broadcasted_iota(jnp.int32, (rows, rows), 0)
    ci = lax.broadcasted_iota(jnp.int32, (rows, rows), 1)
    same = lax.shift_right_logical(ri, sh) == lax.shift_right_logical(ci, sh)
    causal = jnp.logical_and(same, ci <= ri)
    strict = jnp.logical_and(same, ci < ri)
    base = lax.shift_right_logical(ri, 3) == lax.shift_right_logical(ci, 3)
    eye = (ri == ci).astype(F32)
    levels = []
    s = BASE_BLOCK
    while s < seg:
        rb = lax.shift_right_logical(ri, int(math.log2(s)))
        cb = lax.shift_right_logical(ci, int(math.log2(s)))
        levels.append(jnp.logical_and(jnp.bitwise_and(rb, 1) == 1, cb == rb - 1))
        s *= 2

    a = pa_ref[...] + dtb_ref[...]
    softplus = jnp.maximum(a, 0.0) + jnp.log1p(jnp.exp(-jnp.abs(a)))
    glog = jnp.where(live, -jnp.exp(alog_ref[...]) * softplus, 0.0)
    beta = jnp.where(live, _sigmoid(pb_ref[...]), 0.0)
    gc = _dot(causal.astype(F32), glog, precision=HIGHEST)
    gcb_s[...] = _expand(gc, ex_ref[...], 3)
    bb_s[...] = _expand(beta, ex_ref[...], 2)
    gct_s[...] = gc.T

    def widen(x):
        if rows <= dk:
            return x[:, :rows]
        return jnp.concatenate([x] * (rows // dk), axis=1)

    def group(gi, carry_unused):
        hs = [gi * unroll + j for j in range(unroll)]
        los = [pl.multiple_of(h * dk, dk) for h in hs]
        each = range(unroll)
        qs = [q_ref[:, pl.ds(lo, dk)] for lo in los]
        ks = [k_ref[:, pl.ds(lo, dk)] for lo in los]
        vs = [v_ref[:, pl.ds(lo, dk)] for lo in los]
        zs = [z_ref[:, pl.ds(lo, dv)] for lo in los]
        gcbs = [gcb_s[:, pl.ds(lo, dk)] for lo in los]
        bbs = [bb_s[:, pl.ds(lo, dk)] for lo in los]
        grow = [gct_s[pl.ds(h, 1), :] for h in hs]
        if carry:
            sts = [s_s[h] for h in hs]
        qs = [q * lax.rsqrt(jnp.sum(q * q, axis=-1, keepdims=True) + 1e-6) * (dk ** -0.5) for q in qs]
        ks = [k * lax.rsqrt(jnp.sum(k * k, axis=-1, keepdims=True) + 1e-6) for k in ks]
        egcs = [jnp.exp(g) for g in gcbs]
        decays = [jnp.where(causal, jnp.exp(jnp.where(causal, widen(gcbs[j]) - grow[j], 0.0)), 0.0)
                  for j in each]
        kbf = [k.astype(BF16) for k in ks]
        qkks = [_dot_nt(jnp.concatenate([qs[j].astype(BF16), kbf[j]], axis=0), kbf[j]) for j in each]
        qks = [qkks[j][:rows] * decays[j] for j in each]
        amats = [jnp.where(strict, widen(bbs[j]) * decays[j] * qkks[j][rows:], 0.0) for j in each]
        pws = [-jnp.where(base, a, 0.0) for a in amats]
        invs = [eye + p for p in pws]
        for _ in range(2):
            pws = [_mm(p, p) for p in pws]
            invs = [invs[j] + _mm(pws[j], invs[j]) for j in each]
        for lvl in levels:
            lows = [_mm(jnp.where(lvl, amats[j], 0.0), invs[j]) for j in each]
            invs = [invs[j] - _mm(invs[j], lows[j]) for j in each]
        sols = [_mm(invs[j], jnp.concatenate([bbs[j] * vs[j], bbs[j] * egcs[j] * ks[j]], axis=1))
                for j in each]
        wvs = [sl[:, :dv] for sl in sols]
        wks = [sl[:, dv:] for sl in sols]
        qds = [qs[j] * egcs[j] for j in each]
        if carry:
            us = [[] for _ in each]
            os_ = [[] for _ in each]
            for sg in range(nseg):
                r = slice(sg * seg, (sg + 1) * seg)
                last = slice((sg + 1) * seg - 1, (sg + 1) * seg)
                both = [_dot(jnp.concatenate([wks[j][r], qds[j][r]], axis=0).astype(BF16), sts[j].astype(BF16))
                        for j in each]
                u = [wvs[j][r] - both[j][:seg] for j in each]
                kend = [ks[j][r] * jnp.exp(gcbs[j][last] - gcbs[j][r]) for j in each]
                sts = [jnp.exp(gcbs[j][last]) * sts[j] + _dot_tn(kend[j].astype(BF16), u[j].astype(BF16))
                       for j in each]
                for j in each:
                    us[j].append(u[j])
                    os_[j].append(both[j][seg:])
            u_all = [x[0] if nseg == 1 else jnp.concatenate(x, axis=0) for x in us]
            o = [x[0] if nseg == 1 else jnp.concatenate(x, axis=0) for x in os_]
        else:
            lhs = [jnp.concatenate([wks[j], qds[j]], axis=0).astype(BF16) for j in each]
            u_all = [jnp.zeros((rows, dv), F32) for _ in each]
            o = [jnp.zeros((rows, dv), F32) for _ in each]
            news = []
            for sg in range(nseg):
                last = slice((sg + 1) * seg - 1, (sg + 1) * seg)
                mine = rseg == sg
                st0 = [s0_ref[sg, h] for h in hs]
                both = [_dot(lhs[j], st0[j].astype(BF16)) for j in each]
                u_m = [jnp.where(mine, wvs[j] - both[j][:rows], 0.0) for j in each]
                u_all = [u_all[j] + u_m[j] for j in each]
                o = [o[j] + jnp.where(mine, both[j][rows:], 0.0) for j in each]
                kend = [jnp.where(mine, ks[j] * jnp.exp(jnp.where(mine, gcbs[j][last] - gcbs[j], 0.0)), 0.0)
                        for j in each]
                news.append([jnp.exp(gcbs[j][last]) * st0[j] + _dot_tn(kend[j].astype(BF16), u_m[j].astype(BF16))
                             for j in each])
        o = [o[j] + _mm(qks[j], u_all[j]) for j in each]
        o = [o[j] * lax.rsqrt(jnp.mean(o[j] * o[j], axis=-1, keepdims=True) + EPS) * on_ref[...] * _silu(zs[j])
             for j in each]
        for j in each:
            o_ref[:, pl.ds(los[j], dv)] = o[j].astype(o_ref.dtype)
            if carry:
                s_s[hs[j]] = sts[j]
            else:
                for sg in range(nseg):
                    sn_ref[sg, hs[j]] = news[sg][j]
        return carry_unused

    assert heads % unroll == 0
    lax.fori_loop(0, heads // unroll, group, 0)

    if carry:
        @pl.when(pl.program_id(1) == pl.num_programs(1) - 1)
        def _():
            sn_ref[...] = s_s[...]


def _gdn(qkv, proj, s0, a_log, dt_bias, out_norm, *, cols, nseq, seqlen, rows, seg, valid, carry,
         out_shape=None, out_block=0, prev=None):
    _, heads, dk, dv = s0.shape
    assert dk == LANES and dv == LANES and heads <= LANES and rows % seg == 0
    hk = heads * dk
    expand = np.zeros((LANES, hk), np.float32)
    for h in range(heads):
        expand[h, h * dk:(h + 1) * dk] = 1.0
    lane_pad = lambda t: jnp.pad(t.reshape(1, heads).astype(F32), ((0, 0), (0, LANES - heads)))
    col_z, col_a, col_b = cols
    assert col_z % hk == 0 and col_a % LANES == 0 and col_b % LANES == 0
    if carry:
        nblk = seqlen // rows
        grid = (nseq, nblk)
        rmap = lambda i, c: i * nblk + c
        smap = lambda i, c: (i, 0, 0, 0)
        sblock = (None, heads, dk, dv)
        sem = ("parallel", "arbitrary")
    else:
        spb = rows // seg
        grid = (nseq // spb, 1)
        rmap = lambda i, c: i
        smap = lambda i, c: (i, 0, 0, 0)
        sblock = (spb, heads, dk, dv)
        sem = ("parallel", "arbitrary")
    rspec = lambda w, cb: pl.BlockSpec((rows, w), lambda i, c: (rmap(i, c), cb))
    full = lambda shape: pl.BlockSpec(shape, lambda i, c: (0,) * len(shape))
    aliased = prev is not None
    kern = functools.partial(_gdn_kernel, heads=heads, dk=dk, rows=rows, seg=seg, valid=valid, carry=carry,
                             unroll=min(heads, 8) if carry else min(heads, 4), aliased=aliased)
    scratch = [pltpu.VMEM((rows, hk), F32), pltpu.VMEM((rows, hk), F32), pltpu.VMEM((LANES, rows), F32)]
    if carry:
        scratch.append(pltpu.VMEM((heads, dk, dv), F32))
    if out_shape is None:
        out_shape = (nseq * seqlen, heads * dv)
    in_specs = [
        rspec(hk, 0), rspec(hk, 1), rspec(hk, 2),
        rspec(LANES, col_a // LANES), rspec(LANES, col_b // LANES), rspec(hk, col_z // hk),
        pl.BlockSpec(sblock, smap),
        full((1, LANES)), full((1, LANES)), full((1, dv)), full((LANES, hk)),
    ]
    args = [qkv, qkv, qkv, proj, proj, proj, s0, lane_pad(a_log), lane_pad(dt_bias),
            out_norm.reshape(1, dv).astype(F32), jnp.asarray(expand, dtype=BF16)]
    if aliased:
        in_specs.append(pl.BlockSpec(memory_space=pl.ANY))
        args.append(prev)
    return pl.pallas_call(
        kern,
        grid=grid,
        in_specs=in_specs,
        out_specs=[rspec(hk, out_block), pl.BlockSpec(sblock, smap)],
        out_shape=[jax.ShapeDtypeStruct(out_shape, BF16), jax.ShapeDtypeStruct(s0.shape, F32)],
        scratch_shapes=scratch,
        input_output_aliases={len(args) - 1: 0} if aliased else {},
        compiler_params=_cparams(*sem),
        name="gated_deltanet",
    )(*args)


def _ret_kernel(*refs, hps, dk, rows, seg, carry, aliased):
    q_ref, k_ref, v_ref, g_ref, cos_ref, sin_ref, s0_ref, dec_ref, qs_ref, ks_ref, ge_ref = refs[:11]
    o_ref, sn_ref = refs[11 + aliased:13 + aliased]
    s_s = refs[13 + aliased] if carry else None
    dv = dk
    half = dk // 2
    nseg = rows // seg
    sh = int(math.log2(seg))
    hstep = pl.program_id(1) if not carry else 0

    if carry:
        @pl.when(pl.program_id(1) == 0)
        def _():
            s_s[...] = s0_ref[...]

    cos = cos_ref[...]
    sin = sin_ref[...]
    rseg = lax.shift_right_logical(lax.broadcasted_iota(jnp.int32, (rows, 1), 0), sh)

    def rot(x):
        x1, x2 = x[:, :half], x[:, half:]
        return jnp.concatenate([x1 * cos - x2 * sin, x1 * sin + x2 * cos], axis=1)

    def head(hh, carry_unused):
        h = hstep * hps + hh
        lo = pl.multiple_of(h * dk, dk)
        q = rot(q_ref[:, pl.ds(lo, dk)])
        k = rot(k_ref[:, pl.ds(lo, dk)]) * (dk ** -0.5)
        v_bf = v_ref[:, pl.ds(lo, dv)].astype(BF16)
        qk = _dot_nt(q.astype(BF16), k.astype(BF16)) * dec_ref[h]
        qd = (q * qs_ref[:, pl.ds(lo, dk)]).astype(BF16)
        ke = k * ks_ref[:, pl.ds(lo, dk)]
        if carry:
            st = s_s[h]
            os_ = []
            for sg in range(nseg):
                r0 = sg * seg
                os_.append(_dot(qd[r0:r0 + seg], st.astype(BF16)))
                st = ge_ref[h] * st + _dot_tn(ke[r0:r0 + seg].astype(BF16), v_bf[r0:r0 + seg])
            s_s[h] = st
            o = os_[0] if nseg == 1 else jnp.concatenate(os_, axis=0)
        else:
            o = jnp.zeros((rows, dv), F32)
            for sg in range(nseg):
                mine = rseg == sg
                st = s0_ref[sg, hh]
                o = o + jnp.where(mine, _dot(qd, st.astype(BF16)), 0.0)
                sn_ref[sg, hh] = ge_ref[h] * st + _dot_tn(jnp.where(mine, ke, 0.0).astype(BF16), v_bf)
        o = o + _dot(qk.astype(BF16), v_bf)
        mu = jnp.mean(o, axis=-1, keepdims=True)
        var = jnp.mean(jnp.square(o - mu), axis=-1, keepdims=True)
        o = (o - mu) * lax.rsqrt(var + EPS) * _silu(g_ref[:, pl.ds(lo, dv)])
        o_ref[:, pl.ds(lo, dv)] = o.astype(o_ref.dtype)
        return carry_unused

    lax.fori_loop(0, hps, head, 0)

    if carry:
        @pl.when(pl.program_id(1) == pl.num_programs(1) - 1)
        def _():
            sn_ref[...] = s_s[...]


def _retention(proj, s0, *, col_q, nseq, seqlen, rows, seg, valid, pos0, carry,
               out_shape=None, out_block=0, prev=None):
    _, heads, dk, dv = s0.shape
    assert dk == dv and rows % seg == 0
    hw = heads * dk
    half = dk // 2
    assert col_q % hw == 0
    cq = col_q // hw
    npos = seqlen if carry else rows
    offs = np.arange(npos) if carry else np.arange(rows) % seg
    inv_freq = ROPE_BASE ** (-np.arange(half, dtype=np.float64) / half)
    ang = (pos0 + offs.astype(np.float64))[:, None] * inv_freq[None, :]
    cos, sin = np.cos(ang).astype(np.float32), np.sin(ang).astype(np.float32)
    log_gamma = np.log1p(-np.exp2(-5.0 - np.arange(heads, dtype=np.float64)))
    pin = np.arange(rows) % seg
    steps = np.minimum(pin + 1, valid).astype(np.float64)
    gc = log_gamma[:, None] * steps[None, :]
    gl = log_gamma * min(seg, valid)
    sameseg = (np.arange(rows)[:, None] // seg) == (np.arange(rows)[None, :] // seg)
    causal = np.logical_and(sameseg, np.tril(np.ones((rows, rows), bool)))
    diff = gc[:, :, None] - gc[:, None, :]
    dec = np.where(causal[None], np.exp(np.where(causal[None], diff, 0.0)), 0.0).astype(np.float32)
    qs = np.repeat(np.exp(gc).T, dk, axis=1).astype(np.float32)
    ks = np.repeat(np.exp(gl[:, None] - gc).T, dk, axis=1).astype(np.float32)
    ge = np.broadcast_to(np.exp(gl)[:, None, None], (heads, 1, dv)).astype(np.float32)
    if carry:
        nblk = seqlen // rows
        hps = heads
        grid = (nseq, nblk)
        rmap = lambda i, c: i * nblk + c
        tmap = lambda i, c: (c, 0)
        smap = lambda i, c: (i, 0, 0, 0)
        sblock = (None, heads, dk, dv)
    else:
        spb = rows // seg
        hps = max(1, heads // 2)
        grid = (nseq // spb, heads // hps)
        rmap = lambda i, c: i
        tmap = lambda i, c: (0, 0)
        smap = lambda i, c: (i, c, 0, 0)
        sblock = (spb, hps, dk, dv)
    rspec = lambda cb: pl.BlockSpec((rows, hw), lambda i, c: (rmap(i, c), cb))
    full = lambda shape: pl.BlockSpec(shape, lambda i, c: (0,) * len(shape))
    aliased = prev is not None
    kern = functools.partial(_ret_kernel, hps=hps, dk=dk, rows=rows, seg=seg, carry=carry, aliased=aliased)
    if out_shape is None:
        out_shape = (nseq * seqlen, heads * dv)
    in_specs = [
        rspec(cq), rspec(cq + 1), rspec(cq + 2), rspec(cq + 3),
        pl.BlockSpec((rows, half), tmap), pl.BlockSpec((rows, half), tmap),
        pl.BlockSpec(sblock, smap),
        full((heads, rows, rows)), full((rows, hw)), full((rows, hw)), full((heads, 1, dv)),
    ]
    args = [proj, proj, proj, proj, jnp.asarray(cos), jnp.asarray(sin), s0, jnp.asarray(dec), jnp.asarray(qs),
            jnp.asarray(ks), jnp.asarray(ge)]
    if aliased:
        in_specs.append(pl.BlockSpec(memory_space=pl.ANY))
        args.append(prev)
    return pl.pallas_call(
        kern,
        grid=grid,
        in_specs=in_specs,
        out_specs=[rspec(out_block), pl.BlockSpec(sblock, smap)],
        out_shape=[jax.ShapeDtypeStruct(out_shape, BF16), jax.ShapeDtypeStruct(s0.shape, F32)],
        scratch_shapes=[pltpu.VMEM((heads, dk, dv), F32)] if carry else [],
        input_output_aliases={len(args) - 1: 0} if aliased else {},
        compiler_params=_cparams("parallel", "arbitrary"),
        name="retention",
    )(*args)


def _pack_w_rest(w, gdn_c, heads_a, gdn_v):
    off_a = gdn_c
    off_b = off_a + heads_a
    off_z = off_b + heads_a
    off_r = off_z + gdn_v
    zpad = jnp.zeros((w.shape[0], LANES - heads_a), w.dtype)
    return jnp.concatenate([w[:, off_z:off_r], w[:, off_r:], w[:, off_a:off_b], zpad, w[:, off_b:off_z], zpad],
                           axis=1)


def kernel(x_prompt, x_sample, state_gdn_conv, state_gdn, state_ret, state_pool, state_ffn_conv, p_prompt, p_sample, norm_mix, norm_ffn, norm_ple, norm_final, w_in, gdn_conv_w, gdn_a_log, gdn_dt_bias, gdn_out_norm, w_out, pool_w, pool_scale, ffn_w_up, ffn_conv_w, ffn_conv_b, ffn_w_down, ple_w_gate, ple_w_proj):
    bp, lp, d = x_prompt.shape
    bs, ls, _ = x_sample.shape
    depth = p_prompt.shape[0]
    tp, ts = bp * lp, bs * ls
    t = tp + ts
    tm = ts
    assert lp % tm == 0 and tm % SUBLANES == 0
    ntp = tp // tm
    _, _, heads_a, dk_a, dv_a = state_gdn.shape
    _, _, heads_r, dk_r, dv_r = state_ret.shape
    gdn_c = state_gdn_conv.shape[-1]
    gdn_v = heads_a * dv_a
    ret_w = heads_r * dk_r
    gtaps = gdn_conv_w.shape[1]
    ftaps = ffn_conv_w.shape[1]
    pool_past = state_pool.shape[2]
    f2 = ffn_w_up.shape[-1]
    assert min(lp, ls) >= max(gtaps, ftaps) - 1 and pool_past == POOL_HIST - 1 and lp % SUBLANES == 0
    assert lp % CHUNK == 0 and ls <= BASE_BLOCK and gdn_c == 3 * gdn_v and heads_a * dk_a == gdn_v
    assert gdn_v % ret_w == 0 and heads_r * dv_r == ret_w
    col_z, col_r = 0, gdn_v
    col_a = col_r + 4 * ret_w
    col_b = col_a + LANES

    rows_p = lambda a: a.reshape((tp,) + a.shape[2:])
    rows_s = lambda a: jnp.swapaxes(a, 0, 1).reshape((ts,) + a.shape[2:])
    join = lambda a, b: jnp.concatenate([rows_p(a), rows_s(b)], axis=0)
    seq_p = lambda a: a[:tp].reshape((bp, lp) + a.shape[1:])
    seq_s = lambda a: jnp.swapaxes(a[tp:].reshape((ls, bs) + a.shape[1:]), 0, 1)

    tps = lp // tm
    tiles = lambda a: a.reshape((t // tm, tm) + a.shape[1:])

    def hist_p(a, nrows):
        tl = tiles(a)[:ntp, tm - nrows:]
        prev = jnp.concatenate([jnp.zeros_like(tl[:1]), tl[:-1]], axis=0)
        first = (jnp.arange(ntp) % tps == 0)[:, None, None]
        return jnp.where(first, 0.0, prev)

    def tail_p(a, nrows):
        return tiles(a)[tps - 1:ntp:tps, tm - nrows:]

    def hist_s(state, nrows):
        return jnp.swapaxes(state[:, -nrows:], 0, 1).reshape(1, nrows * bs, state.shape[-1])

    def pad_seq(a):
        a = seq_s(a)
        a = jnp.pad(a, ((0, 0), (0, BASE_BLOCK - ls)) + ((0, 0),) * (a.ndim - 2))
        return a.reshape((bs * BASE_BLOCK,) + a.shape[2:])

    unpad_seq = lambda a: rows_s(a.reshape((bs, BASE_BLOCK) + a.shape[1:])[:, :ls])

    xr = join(x_prompt, x_sample)
    pj = [join(p_prompt[i], p_sample[i]).astype(BF16) for i in range(depth)]
    pool_w_bf = pool_w.astype(BF16)
    w_down_bf = ffn_w_down.astype(BF16)
    w_proj_bf = ple_w_proj.astype(BF16)
    rows_prompt = _pick(lp, 256, CHUNK)
    spb = max(1, CHUNK // BASE_BLOCK)
    assert bs % spb == 0

    new_gdn_conv_p, new_gdn_p, new_ret_p, new_pool_p, new_ffn_p = [], [], [], [], []
    new_gdn_conv_s, new_gdn_s, new_ret_s, new_pool_s, new_ffn_s = [], [], [], [], []
    for i in range(depth):
        j = i // 2
        if i % 2 == 0:
            (h,) = _rmsnorm(xr, norm_mix[i], [BF16])
            qkv, cs_p = _proj_conv(h, w_in, gdn_conv_w, layer=j, tm=lp, stride=1, t0=0, nt=bp)
            qkv, cs_s = _proj_conv(h, w_in, gdn_conv_w, layer=j, tm=tm, stride=bs, t0=ntp, nt=1,
                                   hist=hist_s(state_gdn_conv[j], gtaps - 1), prev=qkv)
            proj = _matmul(h, _pack_w_rest(w_in[j], gdn_c, heads_a, gdn_v), name="in_proj")
            cols = (col_z, col_a, col_b)
            mshape = (t, gdn_v + heads_r * dv_r)
            mixed, s_a_p = _gdn(qkv, proj, jnp.zeros((bp,) + state_gdn.shape[2:], F32), gdn_a_log[j],
                                gdn_dt_bias[j], gdn_out_norm[j], cols=cols, nseq=bp, seqlen=lp,
                                rows=rows_prompt, seg=CHUNK, valid=CHUNK, carry=True, out_shape=mshape)
            mixed, s_b_p = _retention(proj, jnp.zeros((bp,) + state_ret.shape[2:], F32), col_q=col_r,
                                      nseq=bp, seqlen=lp, rows=rows_prompt, seg=CHUNK, valid=CHUNK, pos0=0,
                                      carry=True, out_shape=mshape, out_block=gdn_v // ret_w, prev=mixed)
            proj_s = pad_seq(proj)
            o_gdn_s, s_a_s = _gdn(pad_seq(qkv), proj_s, state_gdn[j], gdn_a_log[j], gdn_dt_bias[j],
                                  gdn_out_norm[j], cols=cols, nseq=bs, seqlen=BASE_BLOCK,
                                  rows=spb * BASE_BLOCK, seg=BASE_BLOCK, valid=ls, carry=False)
            o_ret_s, s_b_s = _retention(proj_s, state_ret[j], col_q=col_r, nseq=bs, seqlen=BASE_BLOCK,
                                        rows=spb * BASE_BLOCK, seg=BASE_BLOCK, valid=ls, pos0=PAST_LEN,
                                        carry=False)
            mixed_s = jnp.concatenate([unpad_seq(o_gdn_s), unpad_seq(o_ret_s)], axis=1)
            mixed = lax.dynamic_update_slice(mixed, mixed_s, (tp, 0))
            xr = _matmul(mixed, w_out, layer=j, res=xr, name="out_proj")
            new_gdn_conv_p.append(cs_p)
            new_gdn_conv_s.append(jnp.swapaxes(cs_s.reshape(gtaps - 1, bs, gdn_c), 0, 1))
            new_gdn_p.append(s_a_p)
            new_gdn_s.append(s_a_s)
            new_ret_p.append(s_b_p)
            new_ret_s.append(s_b_s)
        else:
            h32, = _rmsnorm(xr, norm_mix[i], [F32])
            h_s = seq_s(h32)
            pos = jnp.concatenate([jnp.tile(jnp.arange(lp, dtype=jnp.int32), bp),
                                   jnp.repeat(PAST_LEN + jnp.arange(ls, dtype=jnp.int32), bs)])
            cnt = jnp.stack([jnp.minimum(w, pos + 1).astype(F32) for w in POOL_WINDOWS])[:, :, None]
            zero_row = jnp.zeros((bs, 1, d), F32)
            kw = dict(tm=tm)
            xn = _pool_mixer(h32, hist_p(h32, POOL_HIST), cnt, pool_w_bf[j], pool_scale[j], xr, stride=1,
                             t0=0, nt=ntp, **kw)
            xr = _pool_mixer(h32, hist_s(jnp.concatenate([zero_row, state_pool[j]], axis=1), POOL_HIST), cnt,
                             pool_w_bf[j], pool_scale[j], xr, stride=bs, t0=ntp, nt=1, prev=xn, **kw)
            new_pool_p.append(tail_p(h32, pool_past))
            new_pool_s.append(jnp.concatenate([state_pool[j], h_s], axis=1)[:, -pool_past:])
        (h,) = _rmsnorm(xr, norm_ffn[i], [BF16])
        act, sg_p, sv_p = _ffn_up_act(h, ffn_w_up, ffn_conv_w, ffn_conv_b, layer=i, tm=lp, stride=1, t0=0, nt=bp)
        act, sg_s, sv_s = _ffn_up_act(h, ffn_w_up, ffn_conv_w, ffn_conv_b, layer=i, tm=tm, stride=bs, t0=ntp,
                                      nt=1, hist=hist_s(state_ffn_conv[i], ftaps - 1), prev=act)
        xr = _matmul(act, w_down_bf, layer=i, res=xr, tm=1088, name="ffn_down")
        new_ffn_p.append(jnp.concatenate([sg_p, sv_p], axis=-1))
        up_s = jnp.concatenate([sg_s, sv_s], axis=-1).reshape(ftaps - 1, bs, f2)
        new_ffn_s.append(jnp.swapaxes(up_s, 0, 1))
        (h,) = _rmsnorm(xr, norm_ple[i], [BF16])
        xr = _matmul(h, ple_w_gate, layer=i, res=xr, p=pj[i], wp=w_proj_bf, name="ple")
    (out,) = _rmsnorm(xr, norm_final, [F32])
    st = lambda lst: jnp.stack(lst)
    return (seq_p(out), seq_s(out),
            st(new_gdn_conv_p), st(new_gdn_p), st(new_ret_p), st(new_pool_p), st(new_ffn_p),
            st(new_gdn_conv_s), st(new_gdn_s), st(new_ret_s), st(new_pool_s), st(new_ffn_s))
```

```python
import functools
import math

import numpy as np
import jax
import jax.numpy as jnp
from jax import lax
from jax.experimental import pallas as pl
from jax.experimental.pallas import tpu as pltpu

F32 = jnp.float32
BF16 = jnp.bfloat16
HIGHEST = lax.Precision.HIGHEST

EPS = 1e-6
CHUNK = 64
PAST_LEN = 16384
ROPE_BASE = 10000.0
POOL_WINDOWS = (2, 4, 8, 16)
POOL_HIST = 16
LANES = 128
SUBLANES = 8
VMEM_LIMIT = 56 * 1024 * 1024
BASE_BLOCK = 8
ROW_CHUNK = 256


def _pick(n, pref, align=LANES):
    if n <= pref:
        return n
    t = (pref // align) * align
    while t >= align:
        if n % t == 0:
            return t
        t -= align
    return n


def _cparams(*sem):
    return pltpu.CompilerParams(dimension_semantics=sem, vmem_limit_bytes=VMEM_LIMIT)


def _silu(x):
    return x * (1.0 / (1.0 + jnp.exp(-x)))


def _sigmoid(x):
    return 1.0 / (1.0 + jnp.exp(-x))


def _dot(a, b, **kw):
    return jnp.dot(a, b, preferred_element_type=F32, **kw)


def _dot_nt(a, b, **kw):
    return lax.dot_general(a, b, (((1,), (1,)), ((), ())), preferred_element_type=F32, **kw)


def _dot_tn(a, b, **kw):
    return lax.dot_general(a, b, (((0,), (0,)), ((), ())), preferred_element_type=F32, **kw)


def _mm(a, b):
    return _dot(a.astype(BF16), b.astype(BF16))


def _expand(x, e_bf, passes):
    acc = None
    for _ in range(passes):
        piece = x.astype(BF16)
        term = _dot(piece, e_bf)
        acc = term if acc is None else acc + term
        x = x - piece.astype(F32)
    return acc


def _rmsnorm_kernel(x_ref, g_ref, *o_refs):
    x = x_ref[...]
    y = x * lax.rsqrt(jnp.mean(x * x, axis=-1, keepdims=True) + EPS) * g_ref[...]
    for o_ref in o_refs:
        o_ref[...] = y.astype(o_ref.dtype)


def _rmsnorm(x, gain, dtypes):
    m, d = x.shape
    tm = _pick(m, 256, SUBLANES)
    return pl.pallas_call(
        _rmsnorm_kernel,
        grid=(m // tm,),
        in_specs=[pl.BlockSpec((tm, d), lambda i: (i, 0)), pl.BlockSpec((1, d), lambda i: (0, 0))],
        out_specs=[pl.BlockSpec((tm, d), lambda i: (i, 0)) for _ in dtypes],
        out_shape=[jax.ShapeDtypeStruct((m, d), dt) for dt in dtypes],
        compiler_params=_cparams("parallel"),
        name="rmsnorm",
    )(x, gain.reshape(1, d))


def _mm_kernel(*refs, epilogue, w_t):
    x_ref, w_ref = refs[0], refs[1]
    o_ref = refs[-1]
    acc = (_dot_nt if w_t else _dot)(x_ref[...], w_ref[...].astype(BF16))
    if epilogue == "res":
        acc = refs[2][...] + acc
    elif epilogue == "ple":
        res_ref, p_ref, wp_ref = refs[2:5]
        acc = res_ref[...] + _sigmoid(acc) * _dot(p_ref[...], wp_ref[...])
    o_ref[...] = acc.astype(o_ref.dtype)


def _wspec(w, layer, rows, tc, cmap, w_t=False):
    block, index = ((tc, rows), lambda j: (cmap(j), 0)) if w_t else ((rows, tc), lambda j: (0, cmap(j)))
    if w.ndim == 2:
        return pl.BlockSpec(block, lambda i, j: index(j))
    return pl.BlockSpec((None,) + block, lambda i, j: (layer,) + index(j))


def _matmul(x, w, *, layer=0, w_t=False, res=None, p=None, wp=None, tm=2176, tn=256, out_dtype=F32,
            name="matmul"):
    m, kdim = x.shape
    n = w.shape[-2] if w_t else w.shape[-1]
    tm = _pick(m, tm, 2 * SUBLANES)
    tn = _pick(n, tn)
    epilogue = "none" if res is None else ("res" if p is None else "ple")
    in_specs = [pl.BlockSpec((tm, kdim), lambda i, j: (i, 0), pipeline_mode=pl.Buffered(1)),
                _wspec(w, layer, kdim, tn, lambda j: j, w_t)]
    args = [x, w]
    if res is not None:
        in_specs.append(pl.BlockSpec((tm, tn), lambda i, j: (i, j)))
        args.append(res)
    if p is not None:
        pd = p.shape[1]
        in_specs += [pl.BlockSpec((tm, pd), lambda i, j: (i, 0), pipeline_mode=pl.Buffered(1)),
                     _wspec(wp, layer, pd, tn, lambda j: j)]
        args += [p, wp]
    return pl.pallas_call(
        functools.partial(_mm_kernel, epilogue=epilogue, w_t=w_t),
        grid=(m // tm, n // tn),
        in_specs=in_specs,
        out_specs=pl.BlockSpec((tm, tn), lambda i, j: (i, j)),
        out_shape=jax.ShapeDtypeStruct((m, n), out_dtype),
        compiler_params=_cparams("parallel", "arbitrary"),
        name=name,
    )(*args)


def _row_chunk(tm, keep):
    chunk = _pick(tm, ROW_CHUNK, 2 * SUBLANES)
    return chunk if chunk >= keep else tm


def _causal_conv(cur, w_ref, hist, *, taps, stride):
    rows, hrows = cur.shape[0], hist.shape[0]
    ext = jnp.concatenate([hist, cur], axis=0)
    acc = w_ref[taps - 1:taps, :] * cur
    for j in range(taps - 1):
        back = (taps - 1 - j) * stride
        if back % SUBLANES == 0:
            piece = ext[hrows - back:hrows - back + rows, :]
        else:
            piece = pltpu.roll(ext, back, axis=0)[hrows:, :]
        acc = acc + w_ref[j:j + 1, :] * piece
    return acc


def _proj_conv_chunks(x_ref, ws, c_refs, h_refs, emit, *, taps, stride, tm, chunk, w_t=False):
    keep = (taps - 1) * stride
    hrows = -(-keep // SUBLANES) * SUBLANES
    mm = _dot_nt if w_t else _dot
    tails = [jnp.zeros((hrows, w.shape[0 if w_t else 1]), F32) if h is None else h[...]
             for h, w in zip(h_refs, ws)]
    assert all(tl.shape[0] == hrows for tl in tails) and chunk >= hrows
    for c in range(tm // chunk):
        r = slice(c * chunk, (c + 1) * chunk)
        ups = [mm(x_ref[r, :], w) for w in ws]
        emit(r, [_causal_conv(u, cr, tl, taps=taps, stride=stride) for u, cr, tl in zip(ups, c_refs, tails)])
        tails = [u[chunk - hrows:, :] for u in ups]
    return [tl[hrows - keep:, :] for tl in tails]


def _ffn_up_kernel(*refs, taps, stride, tm, chunk, has_hist, aliased):
    x_ref, wg_ref, wv_ref, cg_ref, cv_ref, bg_ref, bv_ref = refs[:7]
    h_refs = (refs[7], refs[8]) if has_hist else (None, None)
    o_ref, sg_ref, sv_ref = refs[7 + 2 * has_hist + aliased:]

    tc = wg_ref.shape[1]
    side = lambda a, b: jnp.concatenate([a[...], b[...]], axis=1)
    bias = side(bg_ref, bv_ref)

    def emit(r, convs):
        y = convs[0] + bias
        o_ref[r, :] = (_silu(y[:, :tc]) * y[:, tc:]).astype(o_ref.dtype)

    w = side(wg_ref, wv_ref).astype(BF16)
    hist = side(*h_refs) if has_hist else None
    (tail,) = _proj_conv_chunks(x_ref, [w], (side(cg_ref, cv_ref),), (hist,), emit, taps=taps,
                                stride=stride, tm=tm, chunk=chunk)
    sg_ref[...] = tail[:, :tc]
    sv_ref[...] = tail[:, tc:]


def _ffn_up_act(h, w_up, conv_w, conv_b, *, layer, tm, stride, t0, nt, hist=None, prev=None):
    t, kdim = h.shape
    f2 = w_up.shape[-1]
    f = f2 // 2
    taps = conv_w.shape[1]
    keep = (taps - 1) * stride
    tc = _pick(f, 256)
    nc = f // tc
    has_hist = hist is not None
    aliased = prev is not None
    kern = functools.partial(_ffn_up_kernel, taps=taps, stride=stride, tm=tm, chunk=_row_chunk(tm, keep),
                             has_hist=has_hist, aliased=aliased)
    lo = lambda j: j
    hi = lambda j: j + nc
    conv_b = conv_b.reshape(conv_b.shape[0], 1, f2)
    in_specs = [
        pl.BlockSpec((tm, kdim), lambda i, j: (i + t0, 0), pipeline_mode=pl.Buffered(1)),
        _wspec(w_up, layer, kdim, tc, lo), _wspec(w_up, layer, kdim, tc, hi),
        _wspec(conv_w, layer, taps, tc, lo), _wspec(conv_w, layer, taps, tc, hi),
        _wspec(conv_b, layer, 1, tc, lo), _wspec(conv_b, layer, 1, tc, hi),
    ]
    args = [h, w_up, w_up, conv_w, conv_w, conv_b, conv_b]
    if has_hist:
        in_specs += [pl.BlockSpec((None, keep, tc), lambda i, j: (i, 0, j)),
                     pl.BlockSpec((None, keep, tc), lambda i, j: (i, 0, j + nc))]
        args += [hist, hist]
    if aliased:
        in_specs.append(pl.BlockSpec(memory_space=pl.ANY))
        args.append(prev)
    sspec = pl.BlockSpec((None, keep, tc), lambda i, j: (i, 0, j))
    return pl.pallas_call(
        kern,
        grid=(nt, nc),
        in_specs=in_specs,
        out_specs=[pl.BlockSpec((tm, tc), lambda i, j: (i + t0, j)), sspec, sspec],
        out_shape=[jax.ShapeDtypeStruct((t, f), BF16), jax.ShapeDtypeStruct((nt, keep, f), F32),
                   jax.ShapeDtypeStruct((nt, keep, f), F32)],
        input_output_aliases={len(args) - 1: 0} if aliased else {},
        compiler_params=_cparams("parallel", "arbitrary"),
        name="ffn_up_conv_act",
    )(*args)


def _proj_conv_kernel(*refs, taps, stride, tm, chunk, has_hist, aliased, w_t):
    x_ref, w_ref, c_ref = refs[:3]
    h_ref = refs[3] if has_hist else None
    o_ref, s_ref = refs[3 + has_hist + aliased:]

    def emit(r, convs):
        o_ref[r, :] = _silu(convs[0])

    (s_ref[...],) = _proj_conv_chunks(x_ref, [w_ref[...].astype(BF16)], (c_ref,), (h_ref,), emit, taps=taps,
                                      stride=stride, tm=tm, chunk=chunk, w_t=w_t)


def _proj_conv(h, w, conv_w, *, layer, tm, stride, t0, nt, w_t=False, hist=None, prev=None):
    t, kdim = h.shape
    taps, c = conv_w.shape[-2:]
    keep = (taps - 1) * stride
    tc = _pick(c, 256)
    has_hist = hist is not None
    aliased = prev is not None
    kern = functools.partial(_proj_conv_kernel, taps=taps, stride=stride, tm=tm, chunk=_row_chunk(tm, keep),
                             has_hist=has_hist, aliased=aliased, w_t=w_t)
    same = lambda j: j
    in_specs = [pl.BlockSpec((tm, kdim), lambda i, j: (i + t0, 0), pipeline_mode=pl.Buffered(1)),
                _wspec(w, layer, kdim, tc, same, w_t), _wspec(conv_w, layer, taps, tc, same)]
    args = [h, w, conv_w]
    if has_hist:
        in_specs.append(pl.BlockSpec((None, keep, tc), lambda i, j: (i, 0, j)))
        args.append(hist)
    if aliased:
        in_specs.append(pl.BlockSpec(memory_space=pl.ANY))
        args.append(prev)
    return pl.pallas_call(
        kern,
        grid=(nt, c // tc),
        in_specs=in_specs,
        out_specs=[pl.BlockSpec((tm, tc), lambda i, j: (i + t0, j)),
                   pl.BlockSpec((None, keep, tc), lambda i, j: (i, 0, j))],
        out_shape=[jax.ShapeDtypeStruct((t, c), F32), jax.ShapeDtypeStruct((nt, keep, c), F32)],
        input_output_aliases={len(args) - 1: 0} if aliased else {},
        compiler_params=_cparams("parallel", "arbitrary"),
        name="qkv_proj_conv",
    )(*args)


def _pool_kernel(*refs, stride, tm, aliased):
    h_ref, hist_ref, cnt_ref, w_ref, sc_ref, x_ref = refs[:6]
    o_ref, e_ref, acc_ref = refs[6 + aliased:]
    g = pl.program_id(1)
    off = POOL_HIST * stride
    window = lax.shift_left(jnp.int32(2), g)
    e_ref[0:off, :] = hist_ref[...]
    e_ref[off:off + tm, :] = h_ref[...]
    acc_ref[...] = e_ref[off:off + tm, :] + e_ref[off - stride:off - stride + tm, :]
    for j in range(2, max(POOL_WINDOWS)):
        @pl.when(j < window)
        def _(j=j):
            acc_ref[...] += e_ref[off - j * stride:off - j * stride + tm, :]
    pooled = acc_ref[...] / cnt_ref[...] - h_ref[...]
    y = _dot(pooled.astype(BF16), w_ref[...]) * sc_ref[...]
    o_ref[...] = x_ref[...] + y


def _pool_mixer(h, hist, cnt, pool_w, scale, x, *, tm, stride, t0, nt, prev=None):
    t, d = h.shape
    g = pool_w.shape[0]
    dg = d // g
    hr = POOL_HIST * stride
    aliased = prev is not None
    kern = functools.partial(_pool_kernel, stride=stride, tm=tm, aliased=aliased)
    in_specs = [
        pl.BlockSpec((tm, dg), lambda i, j: (i + t0, j)),
        pl.BlockSpec((None, hr, dg), lambda i, j: (i, 0, j)),
        pl.BlockSpec((None, tm, 1), lambda i, j: (j, i + t0, 0)),
        pl.BlockSpec((None, dg, dg), lambda i, j: (j, 0, 0)),
        pl.BlockSpec((1, dg), lambda i, j: (0, j)),
        pl.BlockSpec((tm, dg), lambda i, j: (i + t0, j)),
    ]
    args = [h, hist, cnt, pool_w, scale.reshape(1, d), x]
    if aliased:
        in_specs.append(pl.BlockSpec(memory_space=pl.ANY))
        args.append(prev)
    return pl.pallas_call(
        kern,
        grid=(nt, g),
        in_specs=in_specs,
        out_specs=pl.BlockSpec((tm, dg), lambda i, j: (i + t0, j)),
        out_shape=jax.ShapeDtypeStruct((t, d), F32),
        scratch_shapes=[pltpu.VMEM((hr + tm, dg), F32), pltpu.VMEM((tm, dg), F32)],
        input_output_aliases={6: 0} if aliased else {},
        compiler_params=_cparams("parallel", "arbitrary"),
        name="pool_mixer",
    )(*args)


def _gdn_kernel(*refs, heads, dk, rows, seg, valid, carry, unroll, aliased):
    q_ref, k_ref, v_ref, pa_ref, pb_ref, z_ref, s0_ref, alog_ref, dtb_ref, on_ref, ex_ref = refs[:11]
    o_ref, sn_ref, gcb_s, bb_s, gct_s = refs[11 + aliased:16 + aliased]
    s_s = refs[16 + aliased] if carry else None
    dv = dk
    nseg = rows // seg
    sh = int(math.log2(seg))
    assert 1 << sh == seg and seg % BASE_BLOCK == 0

    if carry:
        @pl.when(pl.program_id(1) == 0)
        def _():
            s_s[...] = s0_ref[...]

    rid = lax.broadcasted_iota(jnp.int32, (rows, 1), 0)
    live = jnp.bitwise_and(rid, seg - 1) < valid
    rseg = lax.shift_right_logical(rid, sh)
    ri = lax.broadcasted_iota(jnp.int32, (rows, rows), 0)
    ci = lax.broadcasted_iota(jnp.int32, (rows, rows), 1)
    same = lax.shift_right_logical(ri, sh) == lax.shift_right_logical(ci, sh)
    causal = jnp.logical_and(same, ci <= ri)
    strict = jnp.logical_and(same, ci < ri)
    base = lax.shift_right_logical(ri, 3) == lax.shift_right_logical(ci, 3)
    eye = (ri == ci).astype(F32)
    levels = []
    s = BASE_BLOCK
    while s < seg:
        rb = lax.shift_right_logical(ri, int(math.log2(s)))
        cb = lax.shift_right_logical(ci, int(math.log2(s)))
        levels.append(jnp.logical_and(jnp.bitwise_and(rb, 1) == 1, cb == rb - 1))
        s *= 2

    a = pa_ref[...] + dtb_ref[...]
    softplus = jnp.maximum(a, 0.0) + jnp.log1p(jnp.exp(-jnp.abs(a)))
    glog = jnp.where(live, -jnp.exp(alog_ref[...]) * softplus, 0.0)
    beta = jnp.where(live, _sigmoid(pb_ref[...]), 0.0)
    gc = _dot(causal.astype(F32), glog, precision=HIGHEST)
    gcb_s[...] = _expand(gc, ex_ref[...], 3)
    bb_s[...] = _expand(beta, ex_ref[...], 2)
    gct_s[...] = gc.T

    def widen(x):
        if rows <= dk:
            return x[:, :rows]
        return jnp.concatenate([x] * (rows // dk), axis=1)

    def group(gi, carry_unused):
        hs = [gi * unroll + j for j in range(unroll)]
        los = [pl.multiple_of(h * dk, dk) for h in hs]
        each = range(unroll)
        qs = [q_ref[:, pl.ds(lo, dk)] for lo in los]
        ks = [k_ref[:, pl.ds(lo, dk)] for lo in los]
        vs = [v_ref[:, pl.ds(lo, dk)] for lo in los]
        zs = [z_ref[:, pl.ds(lo, dv)] for lo in los]
        gcbs = [gcb_s[:, pl.ds(lo, dk)] for lo in los]
        bbs = [bb_s[:, pl.ds(lo, dk)] for lo in los]
        grow = [gct_s[pl.ds(h, 1), :] for h in hs]
        if carry:
            sts = [s_s[h] for h in hs]
        qs = [q * lax.rsqrt(jnp.sum(q * q, axis=-1, keepdims=True) + 1e-6) * (dk ** -0.5) for q in qs]
        ks = [k * lax.rsqrt(jnp.sum(k * k, axis=-1, keepdims=True) + 1e-6) for k in ks]
        egcs = [jnp.exp(g) for g in gcbs]
        decays = [jnp.where(causal, jnp.exp(jnp.where(causal, widen(gcbs[j]) - grow[j], 0.0)), 0.0)
                  for j in each]
        kbf = [k.astype(BF16) for k in ks]
        qkks = [_dot_nt(jnp.concatenate([qs[j].astype(BF16), kbf[j]], axis=0), kbf[j]) for j in each]
        qks = [qkks[j][:rows] * decays[j] for j in each]
        amats = [jnp.where(strict, widen(bbs[j]) * decays[j] * qkks[j][rows:], 0.0) for j in each]
        pws = [-jnp.where(base, a, 0.0) for a in amats]
        invs = [eye + p for p in pws]
        for _ in range(2):
            pws = [_mm(p, p) for p in pws]
            invs = [invs[j] + _mm(pws[j], invs[j]) for j in each]
        for lvl in levels:
            lows = [_mm(jnp.where(lvl, amats[j], 0.0), invs[j]) for j in each]
            invs = [invs[j] - _mm(invs[j], lows[j]) for j in each]
        sols = [_mm(invs[j], jnp.concatenate([bbs[j] * vs[j], bbs[j] * egcs[j] * ks[j]], axis=1))
                for j in each]
        wvs = [sl[:, :dv] for sl in sols]
        wks = [sl[:, dv:] for sl in sols]
        qds = [qs[j] * egcs[j] for j in each]
        if carry:
            us = [[] for _ in each]
            os_ = [[] for _ in each]
            for sg in range(nseg):
                r = slice(sg * seg, (sg + 1) * seg)
                last = slice((sg + 1) * seg - 1, (sg + 1) * seg)
                both = [_dot(jnp.concatenate([wks[j][r], qds[j][r]], axis=0).astype(BF16), sts[j].astype(BF16))
                        for j in each]
                u = [wvs[j][r] - both[j][:seg] for j in each]
                kend = [ks[j][r] * jnp.exp(gcbs[j][last] - gcbs[j][r]) for j in each]
                sts = [jnp.exp(gcbs[j][last]) * sts[j] + _dot_tn(kend[j].astype(BF16), u[j].astype(BF16))
                       for j in each]
                for j in each:
                    us[j].append(u[j])
                    os_[j].append(both[j][seg:])
            u_all = [x[0] if nseg == 1 else jnp.concatenate(x, axis=0) for x in us]
            o = [x[0] if nseg == 1 else jnp.concatenate(x, axis=0) for x in os_]
        else:
            lhs = [jnp.concatenate([wks[j], qds[j]], axis=0).astype(BF16) for j in each]
            u_all = [jnp.zeros((rows, dv), F32) for _ in each]
            o = [jnp.zeros((rows, dv), F32) for _ in each]
            news = []
            for sg in range(nseg):
                last = slice((sg + 1) * seg - 1, (sg + 1) * seg)
                mine = rseg == sg
                st0 = [s0_ref[sg, h] for h in hs]
                both = [_dot(lhs[j], st0[j].astype(BF16)) for j in each]
                u_m = [jnp.where(mine, wvs[j] - both[j][:rows], 0.0) for j in each]
                u_all = [u_all[j] + u_m[j] for j in each]
                o = [o[j] + jnp.where(mine, both[j][rows:], 0.0) for j in each]
                kend = [jnp.where(mine, ks[j] * jnp.exp(jnp.where(mine, gcbs[j][last] - gcbs[j], 0.0)), 0.0)
                        for j in each]
                news.append([jnp.exp(gcbs[j][last]) * st0[j] + _dot_tn(kend[j].astype(BF16), u_m[j].astype(BF16))
                             for j in each])
        o = [o[j] + _mm(qks[j], u_all[j]) for j in each]
        o = [o[j] * lax.rsqrt(jnp.mean(o[j] * o[j], axis=-1, keepdims=True) + EPS) * on_ref[...] * _silu(zs[j])
             for j in each]
        for j in each:
            o_ref[:, pl.ds(los[j], dv)] = o[j].astype(o_ref.dtype)
            if carry:
                s_s[hs[j]] = sts[j]
            else:
                for sg in range(nseg):
                    sn_ref[sg, hs[j]] = news[sg][j]
        return carry_unused

    assert heads % unroll == 0
    lax.fori_loop(0, heads // unroll, group, 0)

    if carry:
        @pl.when(pl.program_id(1) == pl.num_programs(1) - 1)
        def _():
            sn_ref[...] = s_s[...]


def _gdn(qkv, proj, s0, a_log, dt_bias, out_norm, *, cols, nseq, seqlen, rows, seg, valid, carry,
         out_shape=None, out_block=0, prev=None):
    _, heads, dk, dv = s0.shape
    assert dk == LANES and dv == LANES and heads <= LANES and rows % seg == 0
    hk = heads * dk
    expand = np.zeros((LANES, hk), np.float32)
    for h in range(heads):
        expand[h, h * dk:(h + 1) * dk] = 1.0
    lane_pad = lambda t: jnp.pad(t.reshape(1, heads).astype(F32), ((0, 0), (0, LANES - heads)))
    col_z, col_a, col_b = cols
    assert col_z % hk == 0 and col_a % LANES == 0 and col_b % LANES == 0
    if carry:
        nblk = seqlen // rows
        grid = (nseq, nblk)
        rmap = lambda i, c: i * nblk + c
        smap = lambda i, c: (i, 0, 0, 0)
        sblock = (None, heads, dk, dv)
        sem = ("parallel", "arbitrary")
    else:
        spb = rows // seg
        grid = (nseq // spb, 1)
        rmap = lambda i, c: i
        smap = lambda i, c: (i, 0, 0, 0)
        sblock = (spb, heads, dk, dv)
        sem = ("parallel", "arbitrary")
    rspec = lambda w, cb: pl.BlockSpec((rows, w), lambda i, c: (rmap(i, c), cb))
    full = lambda shape: pl.BlockSpec(shape, lambda i, c: (0,) * len(shape))
    aliased = prev is not None
    kern = functools.partial(_gdn_kernel, heads=heads, dk=dk, rows=rows, seg=seg, valid=valid, carry=carry,
                             unroll=min(heads, 8) if carry else min(heads, 4), aliased=aliased)
    scratch = [pltpu.VMEM((rows, hk), F32), pltpu.VMEM((rows, hk), F32), pltpu.VMEM((LANES, rows), F32)]
    if carry:
        scratch.append(pltpu.VMEM((heads, dk, dv), F32))
    if out_shape is None:
        out_shape = (nseq * seqlen, heads * dv)
    in_specs = [
        rspec(hk, 0), rspec(hk, 1), rspec(hk, 2),
        rspec(LANES, col_a // LANES), rspec(LANES, col_b // LANES), rspec(hk, col_z // hk),
        pl.BlockSpec(sblock, smap),
        full((1, LANES)), full((1, LANES)), full((1, dv)), full((LANES, hk)),
    ]
    args = [qkv, qkv, qkv, proj, proj, proj, s0, lane_pad(a_log), lane_pad(dt_bias),
            out_norm.reshape(1, dv).astype(F32), jnp.asarray(expand, dtype=BF16)]
    if aliased:
        in_specs.append(pl.BlockSpec(memory_space=pl.ANY))
        args.append(prev)
    return pl.pallas_call(
        kern,
        grid=grid,
        in_specs=in_specs,
        out_specs=[rspec(hk, out_block), pl.BlockSpec(sblock, smap)],
        out_shape=[jax.ShapeDtypeStruct(out_shape, BF16), jax.ShapeDtypeStruct(s0.shape, F32)],
        scratch_shapes=scratch,
        input_output_aliases={len(args) - 1: 0} if aliased else {},
        compiler_params=_cparams(*sem),
        name="gated_deltanet",
    )(*args)


def _ret_kernel(*refs, hps, dk, rows, seg, carry, aliased):
    q_ref, k_ref, v_ref, g_ref, cos_ref, sin_ref, s0_ref, dec_ref, qs_ref, ks_ref, ge_ref = refs[:11]
    o_ref, sn_ref = refs[11 + aliased:13 + aliased]
    s_s = refs[13 + aliased] if carry else None
    dv = dk
    half = dk // 2
    nseg = rows // seg
    sh = int(math.log2(seg))
    hstep = pl.program_id(1) if not carry else 0

    if carry:
        @pl.when(pl.program_id(1) == 0)
        def _():
            s_s[...] = s0_ref[...]

    cos = cos_ref[...]
    sin = sin_ref[...]
    rseg = lax.shift_right_logical(lax.broadcasted_iota(jnp.int32, (rows, 1), 0), sh)

    def rot(x):
        x1, x2 = x[:, :half], x[:, half:]
        return jnp.concatenate([x1 * cos - x2 * sin, x1 * sin + x2 * cos], axis=1)

    def head(hh, carry_unused):
        h = hstep * hps + hh
        lo = pl.multiple_of(h * dk, dk)
        q = rot(q_ref[:, pl.ds(lo, dk)])
        k = rot(k_ref[:, pl.ds(lo, dk)]) * (dk ** -0.5)
        v_bf = v_ref[:, pl.ds(lo, dv)].astype(BF16)
        qk = _dot_nt(q.astype(BF16), k.astype(BF16)) * dec_ref[h]
        qd = (q * qs_ref[:, pl.ds(lo, dk)]).astype(BF16)
        ke = k * ks_ref[:, pl.ds(lo, dk)]
        if carry:
            st = s_s[h]
            os_ = []
            for sg in range(nseg):
                r0 = sg * seg
                os_.append(_dot(qd[r0:r0 + seg], st.astype(BF16)))
                st = ge_ref[h] * st + _dot_tn(ke[r0:r0 + seg].astype(BF16), v_bf[r0:r0 + seg])
            s_s[h] = st
            o = os_[0] if nseg == 1 else jnp.concatenate(os_, axis=0)
        else:
            o = jnp.zeros((rows, dv), F32)
            for sg in range(nseg):
                mine = rseg == sg
                st = s0_ref[sg, hh]
                o = o + jnp.where(mine, _dot(qd, st.astype(BF16)), 0.0)
                sn_ref[sg, hh] = ge_ref[h] * st + _dot_tn(jnp.where(mine, ke, 0.0).astype(BF16), v_bf)
        o = o + _dot(qk.astype(BF16), v_bf)
        mu = jnp.mean(o, axis=-1, keepdims=True)
        var = jnp.mean(jnp.square(o - mu), axis=-1, keepdims=True)
        o = (o - mu) * lax.rsqrt(var + EPS) * _silu(g_ref[:, pl.ds(lo, dv)])
        o_ref[:, pl.ds(lo, dv)] = o.astype(o_ref.dtype)
        return carry_unused

    lax.fori_loop(0, hps, head, 0)

    if carry:
        @pl.when(pl.program_id(1) == pl.num_programs(1) - 1)
        def _():
            sn_ref[...] = s_s[...]


def _retention(proj, s0, *, col_q, nseq, seqlen, rows, seg, valid, pos0, carry,
               out_shape=None, out_block=0, prev=None):
    _, heads, dk, dv = s0.shape
    assert dk == dv and rows % seg == 0
    hw = heads * dk
    half = dk // 2
    assert col_q % hw == 0
    cq = col_q // hw
    npos = seqlen if carry else rows
    offs = np.arange(npos) if carry else np.arange(rows) % seg
    inv_freq = ROPE_BASE ** (-np.arange(half, dtype=np.float64) / half)
    ang = (pos0 + offs.astype(np.float64))[:, None] * inv_freq[None, :]
    cos, sin = np.cos(ang).astype(np.float32), np.sin(ang).astype(np.float32)
    log_gamma = np.log1p(-np.exp2(-5.0 - np.arange(heads, dtype=np.float64)))
    pin = np.arange(rows) % seg
    steps = np.minimum(pin + 1, valid).astype(np.float64)
    gc = log_gamma[:, None] * steps[None, :]
    gl = log_gamma * min(seg, valid)
    sameseg = (np.arange(rows)[:, None] // seg) == (np.arange(rows)[None, :] // seg)
    causal = np.logical_and(sameseg, np.tril(np.ones((rows, rows), bool)))
    diff = gc[:, :, None] - gc[:, None, :]
    dec = np.where(causal[None], np.exp(np.where(causal[None], diff, 0.0)), 0.0).astype(np.float32)
    qs = np.repeat(np.exp(gc).T, dk, axis=1).astype(np.float32)
    ks = np.repeat(np.exp(gl[:, None] - gc).T, dk, axis=1).astype(np.float32)
    ge = np.broadcast_to(np.exp(gl)[:, None, None], (heads, 1, dv)).astype(np.float32)
    if carry:
        nblk = seqlen // rows
        hps = heads
        grid = (nseq, nblk)
        rmap = lambda i, c: i * nblk + c
        tmap = lambda i, c: (c, 0)
        smap = lambda i, c: (i, 0, 0, 0)
        sblock = (None, heads, dk, dv)
    else:
        spb = rows // seg
        hps = max(1, heads // 2)
        grid = (nseq // spb, heads // hps)
        rmap = lambda i, c: i
        tmap = lambda i, c: (0, 0)
        smap = lambda i, c: (i, c, 0, 0)
        sblock = (spb, hps, dk, dv)
    rspec = lambda cb: pl.BlockSpec((rows, hw), lambda i, c: (rmap(i, c), cb))
    full = lambda shape: pl.BlockSpec(shape, lambda i, c: (0,) * len(shape))
    aliased = prev is not None
    kern = functools.partial(_ret_kernel, hps=hps, dk=dk, rows=rows, seg=seg, carry=carry, aliased=aliased)
    if out_shape is None:
        out_shape = (nseq * seqlen, heads * dv)
    in_specs = [
        rspec(cq), rspec(cq + 1), rspec(cq + 2), rspec(cq + 3),
        pl.BlockSpec((rows, half), tmap), pl.BlockSpec((rows, half), tmap),
        pl.BlockSpec(sblock, smap),
        full((heads, rows, rows)), full((rows, hw)), full((rows, hw)), full((heads, 1, dv)),
    ]
    args = [proj, proj, proj, proj, jnp.asarray(cos), jnp.asarray(sin), s0, jnp.asarray(dec), jnp.asarray(qs),
            jnp.asarray(ks), jnp.asarray(ge)]
    if aliased:
        in_specs.append(pl.BlockSpec(memory_space=pl.ANY))
        args.append(prev)
    return pl.pallas_call(
        kern,
        grid=grid,
        in_specs=in_specs,
        out_specs=[rspec(out_block), pl.BlockSpec(sblock, smap)],
        out_shape=[jax.ShapeDtypeStruct(out_shape, BF16), jax.ShapeDtypeStruct(s0.shape, F32)],
        scratch_shapes=[pltpu.VMEM((heads, dk, dv), F32)] if carry else [],
        input_output_aliases={len(args) - 1: 0} if aliased else {},
        compiler_params=_cparams("parallel", "arbitrary"),
        name="retention",
    )(*args)


def _pack_w_rest(wt, gdn_c, heads_a, gdn_v):
    off_a = gdn_c
    off_b = off_a + heads_a
    off_z = off_b + heads_a
    off_r = off_z + gdn_v
    zpad = jnp.zeros((LANES - heads_a, wt.shape[1]), wt.dtype)
    return jnp.concatenate([wt[off_z:off_r], wt[off_r:], wt[off_a:off_b], zpad, wt[off_b:off_z], zpad], axis=0)


def kernel(x_prompt, x_sample, state_gdn_conv, state_gdn, state_ret, state_pool, state_ffn_conv, p_prompt, p_sample, norm_mix, norm_ffn, norm_ple, norm_final, w_in, gdn_conv_w, gdn_a_log, gdn_dt_bias, gdn_out_norm, w_out, pool_w, pool_scale, ffn_w_up, ffn_conv_w, ffn_conv_b, ffn_w_down, ple_w_gate, ple_w_proj):
    bp, lp, d = x_prompt.shape
    bs, ls, _ = x_sample.shape
    depth = p_prompt.shape[0]
    tp, ts = bp * lp, bs * ls
    t = tp + ts
    tm = ts
    assert lp % tm == 0 and tm % SUBLANES == 0
    ntp = tp // tm
    _, _, heads_a, dk_a, dv_a = state_gdn.shape
    _, _, heads_r, dk_r, dv_r = state_ret.shape
    gdn_c = state_gdn_conv.shape[-1]
    gdn_v = heads_a * dv_a
    ret_w = heads_r * dk_r
    gtaps = gdn_conv_w.shape[1]
    ftaps = ffn_conv_w.shape[1]
    pool_past = state_pool.shape[2]
    f2 = ffn_w_up.shape[-1]
    assert min(lp, ls) >= max(gtaps, ftaps) - 1 and pool_past == POOL_HIST - 1 and lp % SUBLANES == 0
    assert lp % CHUNK == 0 and ls <= BASE_BLOCK and gdn_c == 3 * gdn_v and heads_a * dk_a == gdn_v
    assert gdn_v % ret_w == 0 and heads_r * dv_r == ret_w
    col_z, col_r = 0, gdn_v
    col_a = col_r + 4 * ret_w
    col_b = col_a + LANES

    rows_p = lambda a: a.reshape((tp,) + a.shape[2:])
    rows_s = lambda a: jnp.swapaxes(a, 0, 1).reshape((ts,) + a.shape[2:])
    join = lambda a, b: jnp.concatenate([rows_p(a), rows_s(b)], axis=0)
    seq_p = lambda a: a[:tp].reshape((bp, lp) + a.shape[1:])
    seq_s = lambda a: jnp.swapaxes(a[tp:].reshape((ls, bs) + a.shape[1:]), 0, 1)

    tps = lp // tm
    tiles = lambda a: a.reshape((t // tm, tm) + a.shape[1:])

    def hist_p(a, nrows):
        tl = tiles(a)[:ntp, tm - nrows:]
        prev = jnp.concatenate([jnp.zeros_like(tl[:1]), tl[:-1]], axis=0)
        first = (jnp.arange(ntp) % tps == 0)[:, None, None]
        return jnp.where(first, 0.0, prev)

    def tail_p(a, nrows):
        return tiles(a)[tps - 1:ntp:tps, tm - nrows:]

    def hist_s(state, nrows):
        return jnp.swapaxes(state[:, -nrows:], 0, 1).reshape(1, nrows * bs, state.shape[-1])

    def pad_seq(a):
        a = seq_s(a)
        a = jnp.pad(a, ((0, 0), (0, BASE_BLOCK - ls)) + ((0, 0),) * (a.ndim - 2))
        return a.reshape((bs * BASE_BLOCK,) + a.shape[2:])

    unpad_seq = lambda a: rows_s(a.reshape((bs, BASE_BLOCK) + a.shape[1:])[:, :ls])

    xr = join(x_prompt, x_sample)
    pj = [join(p_prompt[i], p_sample[i]).astype(BF16) for i in range(depth)]
    pool_w_bf = pool_w.astype(BF16)
    w_down_bf = ffn_w_down.astype(BF16)
    w_proj_bf = ple_w_proj.astype(BF16)
    w_in_t = jnp.swapaxes(w_in, 1, 2)
    rows_prompt = _pick(lp, 256, CHUNK)
    spb = max(1, CHUNK // BASE_BLOCK)
    assert bs % spb == 0

    new_gdn_conv_p, new_gdn_p, new_ret_p, new_pool_p, new_ffn_p = [], [], [], [], []
    new_gdn_conv_s, new_gdn_s, new_ret_s, new_pool_s, new_ffn_s = [], [], [], [], []
    for i in range(depth):
        j = i // 2
        if i % 2 == 0:
            (h,) = _rmsnorm(xr, norm_mix[i], [BF16])
            qkv, cs_p = _proj_conv(h, w_in_t, gdn_conv_w, layer=j, tm=lp, stride=1, t0=0, nt=bp, w_t=True)
            qkv, cs_s = _proj_conv(h, w_in_t, gdn_conv_w, layer=j, tm=tm, stride=bs, t0=ntp, nt=1, w_t=True,
                                   hist=hist_s(state_gdn_conv[j], gtaps - 1), prev=qkv)
            proj = _matmul(h, _pack_w_rest(w_in_t[j], gdn_c, heads_a, gdn_v), w_t=True, name="in_proj")
            cols = (col_z, col_a, col_b)
            mshape = (t, gdn_v + heads_r * dv_r)
            mixed, s_a_p = _gdn(qkv, proj, jnp.zeros((bp,) + state_gdn.shape[2:], F32), gdn_a_log[j],
                                gdn_dt_bias[j], gdn_out_norm[j], cols=cols, nseq=bp, seqlen=lp,
                                rows=rows_prompt, seg=CHUNK, valid=CHUNK, carry=True, out_shape=mshape)
            mixed, s_b_p = _retention(proj, jnp.zeros((bp,) + state_ret.shape[2:], F32), col_q=col_r,
                                      nseq=bp, seqlen=lp, rows=rows_prompt, seg=CHUNK, valid=CHUNK, pos0=0,
                                      carry=True, out_shape=mshape, out_block=gdn_v // ret_w, prev=mixed)
            proj_s = pad_seq(proj)
            o_gdn_s, s_a_s = _gdn(pad_seq(qkv), proj_s, state_gdn[j], gdn_a_log[j], gdn_dt_bias[j],
                                  gdn_out_norm[j], cols=cols, nseq=bs, seqlen=BASE_BLOCK,
                                  rows=spb * BASE_BLOCK, seg=BASE_BLOCK, valid=ls, carry=False)
            o_ret_s, s_b_s = _retention(proj_s, state_ret[j], col_q=col_r, nseq=bs, seqlen=BASE_BLOCK,
                                        rows=spb * BASE_BLOCK, seg=BASE_BLOCK, valid=ls, pos0=PAST_LEN,
                                        carry=False)
            mixed_s = jnp.concatenate([unpad_seq(o_gdn_s), unpad_seq(o_ret_s)], axis=1)
            mixed = lax.dynamic_update_slice(mixed, mixed_s, (tp, 0))
            xr = _matmul(mixed, w_out, layer=j, res=xr, name="out_proj")
            new_gdn_conv_p.append(cs_p)
            new_gdn_conv_s.append(jnp.swapaxes(cs_s.reshape(gtaps - 1, bs, gdn_c), 0, 1))
            new_gdn_p.append(s_a_p)
            new_gdn_s.append(s_a_s)
            new_ret_p.append(s_b_p)
            new_ret_s.append(s_b_s)
        else:
            h32, = _rmsnorm(xr, norm_mix[i], [F32])
            h_s = seq_s(h32)
            pos = jnp.concatenate([jnp.tile(jnp.arange(lp, dtype=jnp.int32), bp),
                                   jnp.repeat(PAST_LEN + jnp.arange(ls, dtype=jnp.int32), bs)])
            cnt = jnp.stack([jnp.minimum(w, pos + 1).astype(F32) for w in POOL_WINDOWS])[:, :, None]
            zero_row = jnp.zeros((bs, 1, d), F32)
            kw = dict(tm=tm)
            xn = _pool_mixer(h32, hist_p(h32, POOL_HIST), cnt, pool_w_bf[j], pool_scale[j], xr, stride=1,
                             t0=0, nt=ntp, **kw)
            xr = _pool_mixer(h32, hist_s(jnp.concatenate([zero_row, state_pool[j]], axis=1), POOL_HIST), cnt,
                             pool_w_bf[j], pool_scale[j], xr, stride=bs, t0=ntp, nt=1, prev=xn, **kw)
            new_pool_p.append(tail_p(h32, pool_past))
            new_pool_s.append(jnp.concatenate([state_pool[j], h_s], axis=1)[:, -pool_past:])
        (h,) = _rmsnorm(xr, norm_ffn[i], [BF16])
        act, sg_p, sv_p = _ffn_up_act(h, ffn_w_up, ffn_conv_w, ffn_conv_b, layer=i, tm=lp, stride=1, t0=0, nt=bp)
        act, sg_s, sv_s = _ffn_up_act(h, ffn_w_up, ffn_conv_w, ffn_conv_b, layer=i, tm=tm, stride=bs, t0=ntp,
                                      nt=1, hist=hist_s(state_ffn_conv[i], ftaps - 1), prev=act)
        xr = _matmul(act, w_down_bf, layer=i, res=xr, tm=1088, name="ffn_down")
        new_ffn_p.append(jnp.concatenate([sg_p, sv_p], axis=-1))
        up_s = jnp.concatenate([sg_s, sv_s], axis=-1).reshape(ftaps - 1, bs, f2)
        new_ffn_s.append(jnp.swapaxes(up_s, 0, 1))
        (h,) = _rmsnorm(xr, norm_ple[i], [BF16])
        xr = _matmul(h, ple_w_gate, layer=i, res=xr, p=pj[i], wp=w_proj_bf, name="ple")
    (out,) = _rmsnorm(xr, norm_final, [F32])
    st = lambda lst: jnp.stack(lst)
    return (seq_p(out), seq_s(out),
            st(new_gdn_conv_p), st(new_gdn_p), st(new_ret_p), st(new_pool_p), st(new_ffn_p),
            st(new_gdn_conv_s), st(new_gdn_s), st(new_ret_s), st(new_pool_s), st(new_ffn_s))
```

```python
import functools
import math

import numpy as np
import jax
import jax.numpy as jnp
from jax import lax
from jax.experimental import pallas as pl
from jax.experimental.pallas import tpu as pltpu

F32 = jnp.float32
BF16 = jnp.bfloat16
HIGHEST = lax.Precision.HIGHEST

EPS = 1e-6
CHUNK = 64
PAST_LEN = 16384
ROPE_BASE = 10000.0
POOL_WINDOWS = (2, 4, 8, 16)
POOL_HIST = 16
LANES = 128
SUBLANES = 8
VMEM_LIMIT = 56 * 1024 * 1024
BASE_BLOCK = 8
ROW_CHUNK = 256


def _pick(n, pref, align=LANES):
    if n <= pref:
        return n
    t = (pref // align) * align
    while t >= align:
        if n % t == 0:
            return t
        t -= align
    return n


def _cparams(*sem):
    return pltpu.CompilerParams(dimension_semantics=sem, vmem_limit_bytes=VMEM_LIMIT)


def _silu(x):
    return x * (1.0 / (1.0 + jnp.exp(-x)))


def _sigmoid(x):
    return 1.0 / (1.0 + jnp.exp(-x))


def _dot(a, b, **kw):
    return jnp.dot(a, b, preferred_element_type=F32, **kw)


def _dot_nt(a, b, **kw):
    return lax.dot_general(a, b, (((1,), (1,)), ((), ())), preferred_element_type=F32, **kw)


def _dot_tn(a, b, **kw):
    return lax.dot_general(a, b, (((0,), (0,)), ((), ())), preferred_element_type=F32, **kw)


def _mm(a, b):
    return _dot(a.astype(BF16), b.astype(BF16))


def _expand(x, e_bf, passes):
    acc = None
    for _ in range(passes):
        piece = x.astype(BF16)
        term = _dot(piece, e_bf)
        acc = term if acc is None else acc + term
        x = x - piece.astype(F32)
    return acc


def _rmsnorm_kernel(x_ref, g_ref, *o_refs):
    x = x_ref[...]
    y = x * lax.rsqrt(jnp.mean(x * x, axis=-1, keepdims=True) + EPS) * g_ref[...]
    for o_ref in o_refs:
        o_ref[...] = y.astype(o_ref.dtype)


def _rmsnorm(x, gain, dtypes, row0=0, nrows=None):
    d = x.shape[1]
    m = x.shape[0] if nrows is None else nrows
    tm = _pick(math.gcd(m, row0) if row0 else m, 256, SUBLANES)
    t0 = row0 // tm
    return pl.pallas_call(
        _rmsnorm_kernel,
        grid=(m // tm,),
        in_specs=[pl.BlockSpec((tm, d), lambda i: (i + t0, 0)), pl.BlockSpec((1, d), lambda i: (0, 0))],
        out_specs=[pl.BlockSpec((tm, d), lambda i: (i, 0)) for _ in dtypes],
        out_shape=[jax.ShapeDtypeStruct((m, d), dt) for dt in dtypes],
        compiler_params=_cparams("parallel"),
        name="rmsnorm",
    )(x, gain.reshape(1, d))


def _rstd(ss_ref, width):
    return lax.rsqrt(jnp.sum(ss_ref[...], axis=-1, keepdims=True) * (1.0 / width) + EPS)


def _mm_kernel(*refs, epilogue, w_t, scaled, emit_norm):
    it = iter(refs)
    x_ref, w_ref = next(it), next(it)
    ss_in = next(it) if scaled else None
    res_ref = next(it) if epilogue in ("res", "ple") else None
    p_ref, wp_ref = (next(it), next(it)) if epilogue == "ple" else (None, None)
    gn_ref = next(it) if emit_norm else None
    o_ref = next(it)
    acc = (_dot_nt if w_t else _dot)(x_ref[...], w_ref[...].astype(BF16))
    if scaled:
        acc = acc * _rstd(ss_in, x_ref.shape[1])
    if epilogue == "res":
        acc = res_ref[...] + acc
    elif epilogue == "ple":
        acc = res_ref[...] + _sigmoid(acc) * _dot(p_ref[...], wp_ref[...])
    o_ref[...] = acc.astype(o_ref.dtype)
    if emit_norm:
        xb_ref, ss_ref = next(it), next(it)
        xb_ref[...] = (acc * gn_ref[...]).astype(BF16)

        @pl.when(pl.program_id(1) == 0)
        def _():
            ss_ref[...] = jnp.zeros(ss_ref.shape, F32)

        sq = acc * acc
        part = sq[:, :LANES]
        for k in range(1, sq.shape[1] // LANES):
            part = part + sq[:, k * LANES:(k + 1) * LANES]
        ss_ref[...] += part


def _wspec(w, layer, rows, tc, cmap, w_t=False):
    block, index = ((tc, rows), lambda j: (cmap(j), 0)) if w_t else ((rows, tc), lambda j: (0, cmap(j)))
    if w.ndim == 2:
        return pl.BlockSpec(block, lambda i, j: index(j))
    return pl.BlockSpec((None,) + block, lambda i, j: (layer,) + index(j))


def _matmul(x, w, *, layer=0, w_t=False, ss=None, res=None, p=None, wp=None, gain=None, tm=2176, tn=256,
            out_dtype=F32, name="matmul"):
    m, kdim = x.shape
    n = w.shape[-2] if w_t else w.shape[-1]
    tm = _pick(m, tm, 2 * SUBLANES)
    tn = _pick(n, tn)
    epilogue = "none" if res is None else ("res" if p is None else "ple")
    rows = lambda width: pl.BlockSpec((tm, width), lambda i, j: (i, 0), pipeline_mode=pl.Buffered(1))
    in_specs = [rows(kdim), _wspec(w, layer, kdim, tn, lambda j: j, w_t)]
    args = [x, w]
    if ss is not None:
        in_specs.append(rows(LANES))
        args.append(ss)
    if res is not None:
        in_specs.append(pl.BlockSpec((tm, tn), lambda i, j: (i, j)))
        args.append(res)
    if p is not None:
        in_specs += [rows(p.shape[1]), _wspec(wp, layer, p.shape[1], tn, lambda j: j)]
        args += [p, wp]
    tile = pl.BlockSpec((tm, tn), lambda i, j: (i, j))
    out_specs, out_shape = tile, jax.ShapeDtypeStruct((m, n), out_dtype)
    if gain is not None:
        in_specs.append(pl.BlockSpec((1, tn), lambda i, j: (0, j)))
        args.append(gain.reshape(1, n))
        out_specs = [tile, tile, pl.BlockSpec((tm, LANES), lambda i, j: (i, 0))]
        out_shape = [out_shape, jax.ShapeDtypeStruct((m, n), BF16), jax.ShapeDtypeStruct((m, LANES), F32)]
    return pl.pallas_call(
        functools.partial(_mm_kernel, epilogue=epilogue, w_t=w_t, scaled=ss is not None,
                          emit_norm=gain is not None),
        grid=(m // tm, n // tn),
        in_specs=in_specs,
        out_specs=out_specs,
        out_shape=out_shape,
        compiler_params=_cparams("parallel", "arbitrary"),
        name=name,
    )(*args)


def _row_chunk(tm, keep):
    chunk = _pick(tm, ROW_CHUNK, 2 * SUBLANES)
    return chunk if chunk >= keep else tm


def _causal_conv(cur, w_ref, hist, *, taps, stride):
    rows, hrows = cur.shape[0], hist.shape[0]
    ext = jnp.concatenate([hist, cur], axis=0)
    acc = w_ref[taps - 1:taps, :] * cur
    for j in range(taps - 1):
        back = (taps - 1 - j) * stride
        if back % SUBLANES == 0:
            piece = ext[hrows - back:hrows - back + rows, :]
        else:
            piece = pltpu.roll(ext, back, axis=0)[hrows:, :]
        acc = acc + w_ref[j:j + 1, :] * piece
    return acc


def _proj_conv_chunks(x_ref, ws, c_refs, h_refs, emit, *, taps, stride, tm, chunk, w_t=False, scale=None):
    keep = (taps - 1) * stride
    hrows = -(-keep // SUBLANES) * SUBLANES
    mm = _dot_nt if w_t else _dot
    tails = [jnp.zeros((hrows, w.shape[0 if w_t else 1]), F32) if h is None else h[...]
             for h, w in zip(h_refs, ws)]
    assert all(tl.shape[0] == hrows for tl in tails) and chunk >= hrows
    for c in range(tm // chunk):
        r = slice(c * chunk, (c + 1) * chunk)
        ups = [mm(x_ref[r, :], w) for w in ws]
        if scale is not None:
            ups = [u * scale[r, :] for u in ups]
        emit(r, [_causal_conv(u, cr, tl, taps=taps, stride=stride) for u, cr, tl in zip(ups, c_refs, tails)])
        tails = [u[chunk - hrows:, :] for u in ups]
    return [tl[hrows - keep:, :] for tl in tails]


def _ffn_up_kernel(*refs, taps, stride, tm, chunk, scaled, has_hist, aliased):
    x_ref, wg_ref, wv_ref, cg_ref, cv_ref, bg_ref, bv_ref = refs[:7]
    scale = _rstd(refs[7], x_ref.shape[1]) if scaled else None
    h_refs = (refs[7 + scaled], refs[8 + scaled]) if has_hist else (None, None)
    o_ref, sg_ref, sv_ref = refs[7 + scaled + 2 * has_hist + aliased:]

    tc = wg_ref.shape[1]
    side = lambda a, b: jnp.concatenate([a[...], b[...]], axis=1)
    bias = side(bg_ref, bv_ref)

    def emit(r, convs):
        y = convs[0] + bias
        o_ref[r, :] = (_silu(y[:, :tc]) * y[:, tc:]).astype(o_ref.dtype)

    w = side(wg_ref, wv_ref).astype(BF16)
    hist = side(*h_refs) if has_hist else None
    (tail,) = _proj_conv_chunks(x_ref, [w], (side(cg_ref, cv_ref),), (hist,), emit, taps=taps,
                                stride=stride, tm=tm, chunk=chunk, scale=scale)
    sg_ref[...] = tail[:, :tc]
    sv_ref[...] = tail[:, tc:]


def _ffn_up_act(h, w_up, conv_w, conv_b, *, layer, tm, stride, t0, nt, ss=None, hist=None, prev=None):
    t, kdim = h.shape
    f2 = w_up.shape[-1]
    f = f2 // 2
    taps = conv_w.shape[1]
    keep = (taps - 1) * stride
    tc = _pick(f, 256)
    nc = f // tc
    has_hist = hist is not None
    aliased = prev is not None
    kern = functools.partial(_ffn_up_kernel, taps=taps, stride=stride, tm=tm, chunk=_row_chunk(tm, keep),
                             scaled=ss is not None, has_hist=has_hist, aliased=aliased)
    lo = lambda j: j
    hi = lambda j: j + nc
    conv_b = conv_b.reshape(conv_b.shape[0], 1, f2)
    in_specs = [
        pl.BlockSpec((tm, kdim), lambda i, j: (i + t0, 0), pipeline_mode=pl.Buffered(1)),
        _wspec(w_up, layer, kdim, tc, lo), _wspec(w_up, layer, kdim, tc, hi),
        _wspec(conv_w, layer, taps, tc, lo), _wspec(conv_w, layer, taps, tc, hi),
        _wspec(conv_b, layer, 1, tc, lo), _wspec(conv_b, layer, 1, tc, hi),
    ]
    args = [h, w_up, w_up, conv_w, conv_w, conv_b, conv_b]
    if ss is not None:
        in_specs.append(pl.BlockSpec((tm, LANES), lambda i, j: (i + t0, 0), pipeline_mode=pl.Buffered(1)))
        args.append(ss)
    if has_hist:
        in_specs += [pl.BlockSpec((None, keep, tc), lambda i, j: (i, 0, j)),
                     pl.BlockSpec((None, keep, tc), lambda i, j: (i, 0, j + nc))]
        args += [hist, hist]
    if aliased:
        in_specs.append(pl.BlockSpec(memory_space=pl.ANY))
        args.append(prev)
    sspec = pl.BlockSpec((None, keep, tc), lambda i, j: (i, 0, j))
    return pl.pallas_call(
        kern,
        grid=(nt, nc),
        in_specs=in_specs,
        out_specs=[pl.BlockSpec((tm, tc), lambda i, j: (i + t0, j)), sspec, sspec],
        out_shape=[jax.ShapeDtypeStruct((t, f), BF16), jax.ShapeDtypeStruct((nt, keep, f), F32),
                   jax.ShapeDtypeStruct((nt, keep, f), F32)],
        input_output_aliases={len(args) - 1: 0} if aliased else {},
        compiler_params=_cparams("parallel", "arbitrary"),
        name="ffn_up_conv_act",
    )(*args)


def _proj_conv_kernel(*refs, taps, stride, tm, chunk, has_hist, aliased, w_t):
    x_ref, w_ref, c_ref = refs[:3]
    h_ref = refs[3] if has_hist else None
    o_ref, s_ref = refs[3 + has_hist + aliased:]

    def emit(r, convs):
        o_ref[r, :] = _silu(convs[0])

    (s_ref[...],) = _proj_conv_chunks(x_ref, [w_ref[...].astype(BF16)], (c_ref,), (h_ref,), emit, taps=taps,
                                      stride=stride, tm=tm, chunk=chunk, w_t=w_t)


def _proj_conv(h, w, conv_w, *, layer, tm, stride, t0, nt, w_t=False, hist=None, prev=None):
    t, kdim = h.shape
    taps, c = conv_w.shape[-2:]
    keep = (taps - 1) * stride
    tc = _pick(c, 256)
    has_hist = hist is not None
    aliased = prev is not None
    kern = functools.partial(_proj_conv_kernel, taps=taps, stride=stride, tm=tm, chunk=_row_chunk(tm, keep),
                             has_hist=has_hist, aliased=aliased, w_t=w_t)
    same = lambda j: j
    in_specs = [pl.BlockSpec((tm, kdim), lambda i, j: (i + t0, 0), pipeline_mode=pl.Buffered(1)),
                _wspec(w, layer, kdim, tc, same, w_t), _wspec(conv_w, layer, taps, tc, same)]
    args = [h, w, conv_w]
    if has_hist:
        in_specs.append(pl.BlockSpec((None, keep, tc), lambda i, j: (i, 0, j)))
        args.append(hist)
    if aliased:
        in_specs.append(pl.BlockSpec(memory_space=pl.ANY))
        args.append(prev)
    return pl.pallas_call(
        kern,
        grid=(nt, c // tc),
        in_specs=in_specs,
        out_specs=[pl.BlockSpec((tm, tc), lambda i, j: (i + t0, j)),
                   pl.BlockSpec((None, keep, tc), lambda i, j: (i, 0, j))],
        out_shape=[jax.ShapeDtypeStruct((t, c), F32), jax.ShapeDtypeStruct((nt, keep, c), F32)],
        input_output_aliases={len(args) - 1: 0} if aliased else {},
        compiler_params=_cparams("parallel", "arbitrary"),
        name="qkv_proj_conv",
    )(*args)


def _pool_kernel(*refs, stride, tm, aliased):
    h_ref, hist_ref, cnt_ref, w_ref, sc_ref, x_ref = refs[:6]
    o_ref, e_ref, acc_ref = refs[6 + aliased:]
    g = pl.program_id(1)
    off = POOL_HIST * stride
    window = lax.shift_left(jnp.int32(2), g)
    e_ref[0:off, :] = hist_ref[...]
    e_ref[off:off + tm, :] = h_ref[...]
    acc_ref[...] = e_ref[off:off + tm, :] + e_ref[off - stride:off - stride + tm, :]
    for j in range(2, max(POOL_WINDOWS)):
        @pl.when(j < window)
        def _(j=j):
            acc_ref[...] += e_ref[off - j * stride:off - j * stride + tm, :]
    pooled = acc_ref[...] / cnt_ref[...] - h_ref[...]
    y = _dot(pooled.astype(BF16), w_ref[...]) * sc_ref[...]
    o_ref[...] = x_ref[...] + y


def _pool_mixer(h, hist, cnt, pool_w, scale, x, *, tm, stride, t0, nt, prev=None):
    t, d = h.shape
    g = pool_w.shape[0]
    dg = d // g
    hr = POOL_HIST * stride
    aliased = prev is not None
    kern = functools.partial(_pool_kernel, stride=stride, tm=tm, aliased=aliased)
    in_specs = [
        pl.BlockSpec((tm, dg), lambda i, j: (i + t0, j)),
        pl.BlockSpec((None, hr, dg), lambda i, j: (i, 0, j)),
        pl.BlockSpec((None, tm, 1), lambda i, j: (j, i + t0, 0)),
        pl.BlockSpec((None, dg, dg), lambda i, j: (j, 0, 0)),
        pl.BlockSpec((1, dg), lambda i, j: (0, j)),
        pl.BlockSpec((tm, dg), lambda i, j: (i + t0, j)),
    ]
    args = [h, hist, cnt, pool_w, scale.reshape(1, d), x]
    if aliased:
        in_specs.append(pl.BlockSpec(memory_space=pl.ANY))
        args.append(prev)
    return pl.pallas_call(
        kern,
        grid=(nt, g),
        in_specs=in_specs,
        out_specs=pl.BlockSpec((tm, dg), lambda i, j: (i + t0, j)),
        out_shape=jax.ShapeDtypeStruct((t, d), F32),
        scratch_shapes=[pltpu.VMEM((hr + tm, dg), F32), pltpu.VMEM((tm, dg), F32)],
        input_output_aliases={6: 0} if aliased else {},
        compiler_params=_cparams("parallel", "arbitrary"),
        name="pool_mixer",
    )(*args)


def _gdn_kernel(*refs, heads, dk, rows, seg, valid, carry, unroll, aliased):
    q_ref, k_ref, v_ref, pa_ref, pb_ref, z_ref, s0_ref, alog_ref, dtb_ref, on_ref, ex_ref = refs[:11]
    o_ref, sn_ref, gcb_s, bb_s, gct_s = refs[11 + aliased:16 + aliased]
    s_s = refs[16 + aliased] if carry else None
    dv = dk
    nseg = rows // seg
    sh = int(math.log2(seg))
    assert 1 << sh == seg and seg % BASE_BLOCK == 0

    if carry:
        @pl.when(pl.program_id(1) == 0)
        def _():
            s_s[...] = s0_ref[...]

    rid = lax.broadcasted_iota(jnp.int32, (rows, 1), 0)
    live = jnp.bitwise_and(rid, seg - 1) < valid
    rseg = lax.shift_right_logical(rid, sh)
    ri = lax.broadcasted_iota(jnp.int32, (rows, rows), 0)
    ci = lax.broadcasted_iota(jnp.int32, (rows, rows), 1)
    same = lax.shift_right_logical(ri, sh) == lax.shift_right_logical(ci, sh)
    causal = jnp.logical_and(same, ci <= ri)
    strict = jnp.logical_and(same, ci < ri)
    base = lax.shift_right_logical(ri, 3) == lax.shift_right_logical(ci, 3)
    eye = (ri == ci).astype(F32)
    levels = []
    s = BASE_BLOCK
    while s < seg:
        rb = lax.shift_right_logical(ri, int(math.log2(s)))
        cb = lax.shift_right_logical(ci, int(math.log2(s)))
        levels.append(jnp.logical_and(jnp.bitwise_and(rb, 1) == 1, cb == rb - 1))
        s *= 2

    a = pa_ref[...] + dtb_ref[...]
    softplus = jnp.maximum(a, 0.0) + jnp.log1p(jnp.exp(-jnp.abs(a)))
    glog = jnp.where(live, -jnp.exp(alog_ref[...]) * softplus, 0.0)
    beta = jnp.where(live, _sigmoid(pb_ref[...]), 0.0)
    gc = _dot(causal.astype(F32), glog, precision=HIGHEST)
    gcb_s[...] = _expand(gc, ex_ref[...], 3)
    bb_s[...] = _expand(beta, ex_ref[...], 2)
    gct_s[...] = gc.T

    def widen(x):
        if rows <= dk:
            return x[:, :rows]
        return jnp.concatenate([x] * (rows // dk), axis=1)

    def group(gi, carry_unused):
        hs = [gi * unroll + j for j in range(unroll)]
        los = [pl.multiple_of(h * dk, dk) for h in hs]
        each = range(unroll)
        qs = [q_ref[:, pl.ds(lo, dk)] for lo in los]
        ks = [k_ref[:, pl.ds(lo, dk)] for lo in los]
        vs = [v_ref[:, pl.ds(lo, dk)] for lo in los]
        zs = [z_ref[:, pl.ds(lo, dv)] for lo in los]
        gcbs = [gcb_s[:, pl.ds(lo, dk)] for lo in los]
        bbs = [bb_s[:, pl.ds(lo, dk)] for lo in los]
        grow = [gct_s[pl.ds(h, 1), :] for h in hs]
        if carry:
            sts = [s_s[h] for h in hs]
        qs = [q * lax.rsqrt(jnp.sum(q * q, axis=-1, keepdims=True) + 1e-6) * (dk ** -0.5) for q in qs]
        ks = [k * lax.rsqrt(jnp.sum(k * k, axis=-1, keepdims=True) + 1e-6) for k in ks]
        egcs = [jnp.exp(g) for g in gcbs]
        decays = [jnp.where(causal, jnp.exp(jnp.where(causal, widen(gcbs[j]) - grow[j], 0.0)), 0.0)
                  for j in each]
        kbf = [k.astype(BF16) for k in ks]
        qkks = [_dot_nt(jnp.concatenate([qs[j].astype(BF16), kbf[j]], axis=0), kbf[j]) for j in each]
        qks = [qkks[j][:rows] * decays[j] for j in each]
        amats = [jnp.where(strict, widen(bbs[j]) * decays[j] * qkks[j][rows:], 0.0) for j in each]
        pws = [-jnp.where(base, a, 0.0) for a in amats]
        invs = [eye + p for p in pws]
        for _ in range(2):
            pws = [_mm(p, p) for p in pws]
            invs = [invs[j] + _mm(pws[j], invs[j]) for j in each]
        for lvl in levels:
            lows = [_mm(jnp.where(lvl, amats[j], 0.0), invs[j]) for j in each]
            invs = [invs[j] - _mm(invs[j], lows[j]) for j in each]
        sols = [_mm(invs[j], jnp.concatenate([bbs[j] * vs[j], bbs[j] * egcs[j] * ks[j]], axis=1))
                for j in each]
        wvs = [sl[:, :dv] for sl in sols]
        wks = [sl[:, dv:] for sl in sols]
        qds = [qs[j] * egcs[j] for j in each]
        if carry:
            us = [[] for _ in each]
            os_ = [[] for _ in each]
            for sg in range(nseg):
                r = slice(sg * seg, (sg + 1) * seg)
                last = slice((sg + 1) * seg - 1, (sg + 1) * seg)
                both = [_dot(jnp.concatenate([wks[j][r], qds[j][r]], axis=0).astype(BF16), sts[j].astype(BF16))
                        for j in each]
                u = [wvs[j][r] - both[j][:seg] for j in each]
                kend = [ks[j][r] * jnp.exp(gcbs[j][last] - gcbs[j][r]) for j in each]
                sts = [jnp.exp(gcbs[j][last]) * sts[j] + _dot_tn(kend[j].astype(BF16), u[j].astype(BF16))
                       for j in each]
                for j in each:
                    us[j].append(u[j])
                    os_[j].append(both[j][seg:])
            u_all = [x[0] if nseg == 1 else jnp.concatenate(x, axis=0) for x in us]
            o = [x[0] if nseg == 1 else jnp.concatenate(x, axis=0) for x in os_]
        else:
            lhs = [jnp.concatenate([wks[j], qds[j]], axis=0).astype(BF16) for j in each]
            u_all = [jnp.zeros((rows, dv), F32) for _ in each]
            o = [jnp.zeros((rows, dv), F32) for _ in each]
            news = []
            for sg in range(nseg):
                last = slice((sg + 1) * seg - 1, (sg + 1) * seg)
                mine = rseg == sg
                st0 = [s0_ref[sg, h] for h in hs]
                both = [_dot(lhs[j], st0[j].astype(BF16)) for j in each]
                u_m = [jnp.where(mine, wvs[j] - both[j][:rows], 0.0) for j in each]
                u_all = [u_all[j] + u_m[j] for j in each]
                o = [o[j] + jnp.where(mine, both[j][rows:], 0.0) for j in each]
                kend = [jnp.where(mine, ks[j] * jnp.exp(jnp.where(mine, gcbs[j][last] - gcbs[j], 0.0)), 0.0)
                        for j in each]
                news.append([jnp.exp(gcbs[j][last]) * st0[j] + _dot_tn(kend[j].astype(BF16), u_m[j].astype(BF16))
                             for j in each])
        o = [o[j] + _mm(qks[j], u_all[j]) for j in each]
        o = [o[j] * lax.rsqrt(jnp.mean(o[j] * o[j], axis=-1, keepdims=True) + EPS) * on_ref[...] * _silu(zs[j])
             for j in each]
        for j in each:
            o_ref[:, pl.ds(los[j], dv)] = o[j].astype(o_ref.dtype)
            if carry:
                s_s[hs[j]] = sts[j]
            else:
                for sg in range(nseg):
                    sn_ref[sg, hs[j]] = news[sg][j]
        return carry_unused

    assert heads % unroll == 0
    lax.fori_loop(0, heads // unroll, group, 0)

    if carry:
        @pl.when(pl.program_id(1) == pl.num_programs(1) - 1)
        def _():
            sn_ref[...] = s_s[...]


def _gdn(qkv, proj, s0, a_log, dt_bias, out_norm, *, cols, nseq, seqlen, rows, seg, valid, carry,
         out_shape=None, out_block=0, prev=None):
    _, heads, dk, dv = s0.shape
    assert dk == LANES and dv == LANES and heads <= LANES and rows % seg == 0
    hk = heads * dk
    expand = np.zeros((LANES, hk), np.float32)
    for h in range(heads):
        expand[h, h * dk:(h + 1) * dk] = 1.0
    lane_pad = lambda t: jnp.pad(t.reshape(1, heads).astype(F32), ((0, 0), (0, LANES - heads)))
    col_z, col_a, col_b = cols
    assert col_z % hk == 0 and col_a % LANES == 0 and col_b % LANES == 0
    if carry:
        nblk = seqlen // rows
        grid = (nseq, nblk)
        rmap = lambda i, c: i * nblk + c
        smap = lambda i, c: (i, 0, 0, 0)
        sblock = (None, heads, dk, dv)
        sem = ("parallel", "arbitrary")
    else:
        spb = rows // seg
        grid = (nseq // spb, 1)
        rmap = lambda i, c: i
        smap = lambda i, c: (i, 0, 0, 0)
        sblock = (spb, heads, dk, dv)
        sem = ("parallel", "arbitrary")
    rspec = lambda w, cb: pl.BlockSpec((rows, w), lambda i, c: (rmap(i, c), cb))
    full = lambda shape: pl.BlockSpec(shape, lambda i, c: (0,) * len(shape))
    aliased = prev is not None
    kern = functools.partial(_gdn_kernel, heads=heads, dk=dk, rows=rows, seg=seg, valid=valid, carry=carry,
                             unroll=min(heads, 8) if carry else min(heads, 4), aliased=aliased)
    scratch = [pltpu.VMEM((rows, hk), F32), pltpu.VMEM((rows, hk), F32), pltpu.VMEM((LANES, rows), F32)]
    if carry:
        scratch.append(pltpu.VMEM((heads, dk, dv), F32))
    if out_shape is None:
        out_shape = (nseq * seqlen, heads * dv)
    in_specs = [
        rspec(hk, 0), rspec(hk, 1), rspec(hk, 2),
        rspec(LANES, col_a // LANES), rspec(LANES, col_b // LANES), rspec(hk, col_z // hk),
        pl.BlockSpec(sblock, smap),
        full((1, LANES)), full((1, LANES)), full((1, dv)), full((LANES, hk)),
    ]
    args = [qkv, qkv, qkv, proj, proj, proj, s0, lane_pad(a_log), lane_pad(dt_bias),
            out_norm.reshape(1, dv).astype(F32), jnp.asarray(expand, dtype=BF16)]
    if aliased:
        in_specs.append(pl.BlockSpec(memory_space=pl.ANY))
        args.append(prev)
    return pl.pallas_call(
        kern,
        grid=grid,
        in_specs=in_specs,
        out_specs=[rspec(hk, out_block), pl.BlockSpec(sblock, smap)],
        out_shape=[jax.ShapeDtypeStruct(out_shape, BF16), jax.ShapeDtypeStruct(s0.shape, F32)],
        scratch_shapes=scratch,
        input_output_aliases={len(args) - 1: 0} if aliased else {},
        compiler_params=_cparams(*sem),
        name="gated_deltanet",
    )(*args)


def _ret_kernel(*refs, hps, dk, rows, seg, carry, aliased):
    q_ref, k_ref, v_ref, g_ref, cos_ref, sin_ref, s0_ref, dec_ref, qs_ref, ks_ref, ge_ref = refs[:11]
    o_ref, sn_ref = refs[11 + aliased:13 + aliased]
    s_s = refs[13 + aliased] if carry else None
    dv = dk
    half = dk // 2
    nseg = rows // seg
    sh = int(math.log2(seg))
    hstep = pl.program_id(1) if not carry else 0

    if carry:
        @pl.when(pl.program_id(1) == 0)
        def _():
            s_s[...] = s0_ref[...]

    cos = cos_ref[...]
    sin = sin_ref[...]
    rseg = lax.shift_right_logical(lax.broadcasted_iota(jnp.int32, (rows, 1), 0), sh)

    def rot(x):
        x1, x2 = x[:, :half], x[:, half:]
        return jnp.concatenate([x1 * cos - x2 * sin, x1 * sin + x2 * cos], axis=1)

    def head(hh, carry_unused):
        h = hstep * hps + hh
        lo = pl.multiple_of(h * dk, dk)
        q = rot(q_ref[:, pl.ds(lo, dk)])
        k = rot(k_ref[:, pl.ds(lo, dk)]) * (dk ** -0.5)
        v_bf = v_ref[:, pl.ds(lo, dv)].astype(BF16)
        qk = _dot_nt(q.astype(BF16), k.astype(BF16)) * dec_ref[h]
        qd = (q * qs_ref[:, pl.ds(lo, dk)]).astype(BF16)
        ke = k * ks_ref[:, pl.ds(lo, dk)]
        if carry:
            st = s_s[h]
            os_ = []
            for sg in range(nseg):
                r0 = sg * seg
                os_.append(_dot(qd[r0:r0 + seg], st.astype(BF16)))
                st = ge_ref[h] * st + _dot_tn(ke[r0:r0 + seg].astype(BF16), v_bf[r0:r0 + seg])
            s_s[h] = st
            o = os_[0] if nseg == 1 else jnp.concatenate(os_, axis=0)
        else:
            o = jnp.zeros((rows, dv), F32)
            for sg in range(nseg):
                mine = rseg == sg
                st = s0_ref[sg, hh]
                o = o + jnp.where(mine, _dot(qd, st.astype(BF16)), 0.0)
                sn_ref[sg, hh] = ge_ref[h] * st + _dot_tn(jnp.where(mine, ke, 0.0).astype(BF16), v_bf)
        o = o + _dot(qk.astype(BF16), v_bf)
        mu = jnp.mean(o, axis=-1, keepdims=True)
        var = jnp.mean(jnp.square(o - mu), axis=-1, keepdims=True)
        o = (o - mu) * lax.rsqrt(var + EPS) * _silu(g_ref[:, pl.ds(lo, dv)])
        o_ref[:, pl.ds(lo, dv)] = o.astype(o_ref.dtype)
        return carry_unused

    lax.fori_loop(0, hps, head, 0)

    if carry:
        @pl.when(pl.program_id(1) == pl.num_programs(1) - 1)
        def _():
            sn_ref[...] = s_s[...]


def _retention(proj, s0, *, col_q, nseq, seqlen, rows, seg, valid, pos0, carry,
               out_shape=None, out_block=0, prev=None):
    _, heads, dk, dv = s0.shape
    assert dk == dv and rows % seg == 0
    hw = heads * dk
    half = dk // 2
    assert col_q % hw == 0
    cq = col_q // hw
    npos = seqlen if carry else rows
    offs = np.arange(npos) if carry else np.arange(rows) % seg
    inv_freq = ROPE_BASE ** (-np.arange(half, dtype=np.float64) / half)
    ang = (pos0 + offs.astype(np.float64))[:, None] * inv_freq[None, :]
    cos, sin = np.cos(ang).astype(np.float32), np.sin(ang).astype(np.float32)
    log_gamma = np.log1p(-np.exp2(-5.0 - np.arange(heads, dtype=np.float64)))
    pin = np.arange(rows) % seg
    steps = np.minimum(pin + 1, valid).astype(np.float64)
    gc = log_gamma[:, None] * steps[None, :]
    gl = log_gamma * min(seg, valid)
    sameseg = (np.arange(rows)[:, None] // seg) == (np.arange(rows)[None, :] // seg)
    causal = np.logical_and(sameseg, np.tril(np.ones((rows, rows), bool)))
    diff = gc[:, :, None] - gc[:, None, :]
    dec = np.where(causal[None], np.exp(np.where(causal[None], diff, 0.0)), 0.0).astype(np.float32)
    qs = np.repeat(np.exp(gc).T, dk, axis=1).astype(np.float32)
    ks = np.repeat(np.exp(gl[:, None] - gc).T, dk, axis=1).astype(np.float32)
    ge = np.broadcast_to(np.exp(gl)[:, None, None], (heads, 1, dv)).astype(np.float32)
    if carry:
        nblk = seqlen // rows
        hps = heads
        grid = (nseq, nblk)
        rmap = lambda i, c: i * nblk + c
        tmap = lambda i, c: (c, 0)
        smap = lambda i, c: (i, 0, 0, 0)
        sblock = (None, heads, dk, dv)
    else:
        spb = rows // seg
        hps = max(1, heads // 2)
        grid = (nseq // spb, heads // hps)
        rmap = lambda i, c: i
        tmap = lambda i, c: (0, 0)
        smap = lambda i, c: (i, c, 0, 0)
        sblock = (spb, hps, dk, dv)
    rspec = lambda cb: pl.BlockSpec((rows, hw), lambda i, c: (rmap(i, c), cb))
    full = lambda shape: pl.BlockSpec(shape, lambda i, c: (0,) * len(shape))
    aliased = prev is not None
    kern = functools.partial(_ret_kernel, hps=hps, dk=dk, rows=rows, seg=seg, carry=carry, aliased=aliased)
    if out_shape is None:
        out_shape = (nseq * seqlen, heads * dv)
    in_specs = [
        rspec(cq), rspec(cq + 1), rspec(cq + 2), rspec(cq + 3),
        pl.BlockSpec((rows, half), tmap), pl.BlockSpec((rows, half), tmap),
        pl.BlockSpec(sblock, smap),
        full((heads, rows, rows)), full((rows, hw)), full((rows, hw)), full((heads, 1, dv)),
    ]
    args = [proj, proj, proj, proj, jnp.asarray(cos), jnp.asarray(sin), s0, jnp.asarray(dec), jnp.asarray(qs),
            jnp.asarray(ks), jnp.asarray(ge)]
    if aliased:
        in_specs.append(pl.BlockSpec(memory_space=pl.ANY))
        args.append(prev)
    return pl.pallas_call(
        kern,
        grid=grid,
        in_specs=in_specs,
        out_specs=[rspec(out_block), pl.BlockSpec(sblock, smap)],
        out_shape=[jax.ShapeDtypeStruct(out_shape, BF16), jax.ShapeDtypeStruct(s0.shape, F32)],
        scratch_shapes=[pltpu.VMEM((heads, dk, dv), F32)] if carry else [],
        input_output_aliases={len(args) - 1: 0} if aliased else {},
        compiler_params=_cparams("parallel", "arbitrary"),
        name="retention",
    )(*args)


def _pack_w_rest(wt, gdn_c, heads_a, gdn_v):
    off_a = gdn_c
    off_b = off_a + heads_a
    off_z = off_b + heads_a
    off_r = off_z + gdn_v
    zpad = jnp.zeros((LANES - heads_a, wt.shape[1]), wt.dtype)
    return jnp.concatenate([wt[off_z:off_r], wt[off_r:], wt[off_a:off_b], zpad, wt[off_b:off_z], zpad], axis=0)


def kernel(x_prompt, x_sample, state_gdn_conv, state_gdn, state_ret, state_pool, state_ffn_conv, p_prompt, p_sample, norm_mix, norm_ffn, norm_ple, norm_final, w_in, gdn_conv_w, gdn_a_log, gdn_dt_bias, gdn_out_norm, w_out, pool_w, pool_scale, ffn_w_up, ffn_conv_w, ffn_conv_b, ffn_w_down, ple_w_gate, ple_w_proj):
    bp, lp, d = x_prompt.shape
    bs, ls, _ = x_sample.shape
    depth = p_prompt.shape[0]
    tp, ts = bp * lp, bs * ls
    t = tp + ts
    tm = ts
    assert lp % tm == 0 and tm % SUBLANES == 0
    ntp = tp // tm
    _, _, heads_a, dk_a, dv_a = state_gdn.shape
    _, _, heads_r, dk_r, dv_r = state_ret.shape
    gdn_c = state_gdn_conv.shape[-1]
    gdn_v = heads_a * dv_a
    ret_w = heads_r * dk_r
    gtaps = gdn_conv_w.shape[1]
    ftaps = ffn_conv_w.shape[1]
    pool_past = state_pool.shape[2]
    f2 = ffn_w_up.shape[-1]
    assert min(lp, ls) >= max(gtaps, ftaps) - 1 and pool_past == POOL_HIST - 1 and lp % SUBLANES == 0
    assert lp % CHUNK == 0 and ls <= BASE_BLOCK and gdn_c == 3 * gdn_v and heads_a * dk_a == gdn_v
    assert gdn_v % ret_w == 0 and heads_r * dv_r == ret_w
    col_z, col_r = 0, gdn_v
    col_a = col_r + 4 * ret_w
    col_b = col_a + LANES

    rows_p = lambda a: a.reshape((tp,) + a.shape[2:])
    rows_s = lambda a: jnp.swapaxes(a, 0, 1).reshape((ts,) + a.shape[2:])
    join = lambda a, b: jnp.concatenate([rows_p(a), rows_s(b)], axis=0)
    seq_p = lambda a: a[:tp].reshape((bp, lp) + a.shape[1:])
    seq_s = lambda a: jnp.swapaxes(a[tp:].reshape((ls, bs) + a.shape[1:]), 0, 1)

    tps = lp // tm
    tiles = lambda a: a.reshape((t // tm, tm) + a.shape[1:])

    def hist_p(a, nrows):
        tl = tiles(a)[:ntp, tm - nrows:]
        prev = jnp.concatenate([jnp.zeros_like(tl[:1]), tl[:-1]], axis=0)
        first = (jnp.arange(ntp) % tps == 0)[:, None, None]
        return jnp.where(first, 0.0, prev)

    def tail_p(a, nrows):
        return tiles(a)[tps - 1:ntp:tps, tm - nrows:]

    def hist_s(state, nrows):
        return jnp.swapaxes(state[:, -nrows:], 0, 1).reshape(1, nrows * bs, state.shape[-1])

    def pad_seq(a):
        a = seq_s(a)
        a = jnp.pad(a, ((0, 0), (0, BASE_BLOCK - ls)) + ((0, 0),) * (a.ndim - 2))
        return a.reshape((bs * BASE_BLOCK,) + a.shape[2:])

    unpad_seq = lambda a: rows_s(a.reshape((bs, BASE_BLOCK) + a.shape[1:])[:, :ls])

    xr = join(x_prompt, x_sample)
    pj = [join(p_prompt[i], p_sample[i]).astype(BF16) for i in range(depth)]
    pool_w_bf = pool_w.astype(BF16)
    w_down_bf = ffn_w_down.astype(BF16)
    w_proj_bf = ple_w_proj.astype(BF16)
    w_in_t = jnp.swapaxes(w_in, 1, 2)
    rows_prompt = _pick(lp, 256, CHUNK)
    spb = max(1, CHUNK // BASE_BLOCK)
    assert bs % spb == 0

    new_gdn_conv_p, new_gdn_p, new_ret_p, new_pool_p, new_ffn_p = [], [], [], [], []
    new_gdn_conv_s, new_gdn_s, new_ret_s, new_pool_s, new_ffn_s = [], [], [], [], []
    for i in range(depth):
        j = i // 2
        if i % 2 == 0:
            (h,) = _rmsnorm(xr, norm_mix[i], [BF16])
            qkv, cs_p = _proj_conv(h, w_in_t, gdn_conv_w, layer=j, tm=lp, stride=1, t0=0, nt=bp, w_t=True)
            qkv, cs_s = _proj_conv(h, w_in_t, gdn_conv_w, layer=j, tm=tm, stride=bs, t0=ntp, nt=1, w_t=True,
                                   hist=hist_s(state_gdn_conv[j], gtaps - 1), prev=qkv)
            proj = _matmul(h, _pack_w_rest(w_in_t[j], gdn_c, heads_a, gdn_v), w_t=True, name="in_proj")
            cols = (col_z, col_a, col_b)
            mshape = (t, gdn_v + heads_r * dv_r)
            mixed, s_a_p = _gdn(qkv, proj, jnp.zeros((bp,) + state_gdn.shape[2:], F32), gdn_a_log[j],
                                gdn_dt_bias[j], gdn_out_norm[j], cols=cols, nseq=bp, seqlen=lp,
                                rows=rows_prompt, seg=CHUNK, valid=CHUNK, carry=True, out_shape=mshape)
            mixed, s_b_p = _retention(proj, jnp.zeros((bp,) + state_ret.shape[2:], F32), col_q=col_r,
                                      nseq=bp, seqlen=lp, rows=rows_prompt, seg=CHUNK, valid=CHUNK, pos0=0,
                                      carry=True, out_shape=mshape, out_block=gdn_v // ret_w, prev=mixed)
            proj_s = pad_seq(proj)
            o_gdn_s, s_a_s = _gdn(pad_seq(qkv), proj_s, state_gdn[j], gdn_a_log[j], gdn_dt_bias[j],
                                  gdn_out_norm[j], cols=cols, nseq=bs, seqlen=BASE_BLOCK,
                                  rows=spb * BASE_BLOCK, seg=BASE_BLOCK, valid=ls, carry=False)
            o_ret_s, s_b_s = _retention(proj_s, state_ret[j], col_q=col_r, nseq=bs, seqlen=BASE_BLOCK,
                                        rows=spb * BASE_BLOCK, seg=BASE_BLOCK, valid=ls, pos0=PAST_LEN,
                                        carry=False)
            mixed_s = jnp.concatenate([unpad_seq(o_gdn_s), unpad_seq(o_ret_s)], axis=1)
            mixed = lax.dynamic_update_slice(mixed, mixed_s, (tp, 0))
            xr, h, ss = _matmul(mixed, w_out, layer=j, res=xr, gain=norm_ffn[i], name="out_proj")
            new_gdn_conv_p.append(cs_p)
            new_gdn_conv_s.append(jnp.swapaxes(cs_s.reshape(gtaps - 1, bs, gdn_c), 0, 1))
            new_gdn_p.append(s_a_p)
            new_gdn_s.append(s_a_s)
            new_ret_p.append(s_b_p)
            new_ret_s.append(s_b_s)
        else:
            h32, = _rmsnorm(xr, norm_mix[i], [F32])
            h_s = seq_s(h32)
            pos = jnp.concatenate([jnp.tile(jnp.arange(lp, dtype=jnp.int32), bp),
                                   jnp.repeat(PAST_LEN + jnp.arange(ls, dtype=jnp.int32), bs)])
            cnt = jnp.stack([jnp.minimum(w, pos + 1).astype(F32) for w in POOL_WINDOWS])[:, :, None]
            zero_row = jnp.zeros((bs, 1, d), F32)
            kw = dict(tm=tm)
            xn = _pool_mixer(h32, hist_p(h32, POOL_HIST), cnt, pool_w_bf[j], pool_scale[j], xr, stride=1,
                             t0=0, nt=ntp, **kw)
            xr = _pool_mixer(h32, hist_s(jnp.concatenate([zero_row, state_pool[j]], axis=1), POOL_HIST), cnt,
                             pool_w_bf[j], pool_scale[j], xr, stride=bs, t0=ntp, nt=1, prev=xn, **kw)
            new_pool_p.append(tail_p(h32, pool_past))
            new_pool_s.append(jnp.concatenate([state_pool[j], h_s], axis=1)[:, -pool_past:])
            ((h,), ss) = _rmsnorm(xr, norm_ffn[i], [BF16]), None
        act, sg_p, sv_p = _ffn_up_act(h, ffn_w_up, ffn_conv_w, ffn_conv_b, layer=i, tm=lp, stride=1, t0=0, nt=bp,
                                      ss=ss)
        act, sg_s, sv_s = _ffn_up_act(h, ffn_w_up, ffn_conv_w, ffn_conv_b, layer=i, tm=tm, stride=bs, t0=ntp,
                                      nt=1, ss=ss, hist=hist_s(state_ffn_conv[i], ftaps - 1), prev=act)
        xr, h, ss = _matmul(act, w_down_bf, layer=i, res=xr, gain=norm_ple[i], tm=1088, name="ffn_down")
        new_ffn_p.append(jnp.concatenate([sg_p, sv_p], axis=-1))
        up_s = jnp.concatenate([sg_s, sv_s], axis=-1).reshape(ftaps - 1, bs, f2)
        new_ffn_s.append(jnp.swapaxes(up_s, 0, 1))
        xr = _matmul(h, ple_w_gate, layer=i, ss=ss, res=xr, p=pj[i], wp=w_proj_bf, name="ple")
    (out_p,) = _rmsnorm(xr, norm_final, [F32], 0, tp)
    (out_s,) = _rmsnorm(xr, norm_final, [F32], tp, ts)
    st = lambda lst: jnp.stack(lst)
    return (out_p.reshape(bp, lp, d), jnp.swapaxes(out_s.reshape(ls, bs, d), 0, 1),
            st(new_gdn_conv_p), st(new_gdn_p), st(new_ret_p), st(new_pool_p), st(new_ffn_p),
            st(new_gdn_conv_s), st(new_gdn_s), st(new_ret_s), st(new_pool_s), st(new_ffn_s))
```

```python
import functools
import math

import numpy as np
import jax
import jax.numpy as jnp
from jax import lax
from jax.experimental import pallas as pl
from jax.experimental.pallas import tpu as pltpu

F32 = jnp.float32
BF16 = jnp.bfloat16
HIGHEST = lax.Precision.HIGHEST

EPS = 1e-6
CHUNK = 64
PAST_LEN = 16384
ROPE_BASE = 10000.0
POOL_WINDOWS = (2, 4, 8, 16)
POOL_HIST = 16
LANES = 128
SUBLANES = 8
VMEM_LIMIT = 56 * 1024 * 1024
X_TILE_BYTES = 24 * 1024 * 1024
BASE_BLOCK = 8
ROW_CHUNK = 256


def _pick(n, pref, align=LANES):
    if n <= pref:
        return n
    t = (pref // align) * align
    while t >= align:
        if n % t == 0:
            return t
        t -= align
    return n


def _cparams(*sem):
    return pltpu.CompilerParams(dimension_semantics=sem, vmem_limit_bytes=VMEM_LIMIT)


def _silu(x):
    return x * (1.0 / (1.0 + jnp.exp(-x)))


def _sigmoid(x):
    return 1.0 / (1.0 + jnp.exp(-x))


def _dot(a, b, **kw):
    return jnp.dot(a, b, preferred_element_type=F32, **kw)


def _dot_nt(a, b, **kw):
    return lax.dot_general(a, b, (((1,), (1,)), ((), ())), preferred_element_type=F32, **kw)


def _dot_tn(a, b, **kw):
    return lax.dot_general(a, b, (((0,), (0,)), ((), ())), preferred_element_type=F32, **kw)


def _mm(a, b):
    return _dot(a.astype(BF16), b.astype(BF16))


def _expand(x, e_bf, passes):
    acc = None
    for _ in range(passes):
        piece = x.astype(BF16)
        term = _dot(piece, e_bf)
        acc = term if acc is None else acc + term
        x = x - piece.astype(F32)
    return acc


def _rmsnorm_kernel(x_ref, g_ref, *o_refs):
    x = x_ref[...]
    y = x * lax.rsqrt(jnp.mean(x * x, axis=-1, keepdims=True) + EPS) * g_ref[...]
    for o_ref in o_refs:
        o_ref[...] = y.astype(o_ref.dtype)


def _rmsnorm(x, gain, dtypes, row0=0, nrows=None):
    d = x.shape[1]
    m = x.shape[0] if nrows is None else nrows
    tm = _pick(math.gcd(m, row0) if row0 else m, 256, SUBLANES)
    t0 = row0 // tm
    return pl.pallas_call(
        _rmsnorm_kernel,
        grid=(m // tm,),
        in_specs=[pl.BlockSpec((tm, d), lambda i: (i + t0, 0)), pl.BlockSpec((1, d), lambda i: (0, 0))],
        out_specs=[pl.BlockSpec((tm, d), lambda i: (i, 0)) for _ in dtypes],
        out_shape=[jax.ShapeDtypeStruct((m, d), dt) for dt in dtypes],
        compiler_params=_cparams("parallel"),
        name="rmsnorm",
    )(x, gain.reshape(1, d))


def _rstd(ss_ref, width):
    return lax.rsqrt(jnp.sum(ss_ref[...], axis=-1, keepdims=True) * (1.0 / width) + EPS)


def _mm_kernel(*refs, epilogue, w_t, scaled, emit_norm):
    it = iter(refs)
    x_ref, w_ref = next(it), next(it)
    ss_in = next(it) if scaled else None
    res_ref = next(it) if epilogue in ("res", "ple") else None
    p_ref, wp_ref = (next(it), next(it)) if epilogue == "ple" else (None, None)
    gn_ref = next(it) if emit_norm else None
    o_ref = next(it)
    acc = (_dot_nt if w_t else _dot)(x_ref[...], w_ref[...].astype(BF16))
    if scaled:
        acc = acc * _rstd(ss_in, x_ref.shape[1])
    if epilogue == "res":
        acc = res_ref[...] + acc
    elif epilogue == "ple":
        acc = res_ref[...] + _sigmoid(acc) * _dot(p_ref[...], wp_ref[...])
    o_ref[...] = acc.astype(o_ref.dtype)
    if emit_norm:
        xb_ref, ss_ref = next(it), next(it)
        xb_ref[...] = (acc * gn_ref[...]).astype(BF16)

        @pl.when(pl.program_id(1) == 0)
        def _():
            ss_ref[...] = jnp.zeros(ss_ref.shape, F32)

        sq = acc * acc
        part = sq[:, :LANES]
        for k in range(1, sq.shape[1] // LANES):
            part = part + sq[:, k * LANES:(k + 1) * LANES]
        ss_ref[...] += part


def _wspec(w, layer, rows, tc, cmap, w_t=False):
    block, index = ((tc, rows), lambda j: (cmap(j), 0)) if w_t else ((rows, tc), lambda j: (0, cmap(j)))
    if w.ndim == 2:
        return pl.BlockSpec(block, lambda i, j: index(j))
    return pl.BlockSpec((None,) + block, lambda i, j: (layer,) + index(j))


def _matmul(x, w, *, layer=0, w_t=False, ss=None, res=None, p=None, wp=None, gain=None, tn=256,
            out_dtype=F32, name="matmul"):
    m, kdim = x.shape
    n = w.shape[-2] if w_t else w.shape[-1]
    tm = _pick(m, X_TILE_BYTES // (2 * kdim), 2 * SUBLANES)
    tn = _pick(n, tn)
    epilogue = "none" if res is None else ("res" if p is None else "ple")
    rows = lambda width: pl.BlockSpec((tm, width), lambda i, j: (i, 0), pipeline_mode=pl.Buffered(1))
    in_specs = [rows(kdim), _wspec(w, layer, kdim, tn, lambda j: j, w_t)]
    args = [x, w]
    if ss is not None:
        in_specs.append(rows(LANES))
        args.append(ss)
    if res is not None:
        in_specs.append(pl.BlockSpec((tm, tn), lambda i, j: (i, j)))
        args.append(res)
    if p is not None:
        in_specs += [rows(p.shape[1]), _wspec(wp, layer, p.shape[1], tn, lambda j: j)]
        args += [p, wp]
    tile = pl.BlockSpec((tm, tn), lambda i, j: (i, j))
    out_specs, out_shape = tile, jax.ShapeDtypeStruct((m, n), out_dtype)
    if gain is not None:
        in_specs.append(pl.BlockSpec((1, tn), lambda i, j: (0, j)))
        args.append(gain.reshape(1, n))
        out_specs = [tile, tile, pl.BlockSpec((tm, LANES), lambda i, j: (i, 0))]
        out_shape = [out_shape, jax.ShapeDtypeStruct((m, n), BF16), jax.ShapeDtypeStruct((m, LANES), F32)]
    return pl.pallas_call(
        functools.partial(_mm_kernel, epilogue=epilogue, w_t=w_t, scaled=ss is not None,
                          emit_norm=gain is not None),
        grid=(m // tm, n // tn),
        in_specs=in_specs,
        out_specs=out_specs,
        out_shape=out_shape,
        compiler_params=_cparams("parallel", "arbitrary"),
        name=name,
    )(*args)


def _row_chunk(tm, keep, pref):
    chunk = _pick(tm, pref, 2 * SUBLANES)
    return chunk if chunk >= keep else tm


def _causal_conv(cur, w_ref, hist, *, taps, stride):
    rows, hrows = cur.shape[0], hist.shape[0]
    ext = jnp.concatenate([hist, cur], axis=0)
    acc = w_ref[taps - 1:taps, :] * cur
    for j in range(taps - 1):
        back = (taps - 1 - j) * stride
        if back % SUBLANES == 0:
            piece = ext[hrows - back:hrows - back + rows, :]
        else:
            piece = pltpu.roll(ext, back, axis=0)[hrows:, :]
        acc = acc + w_ref[j:j + 1, :] * piece
    return acc


def _proj_conv_chunks(x_ref, ws, c_refs, h_refs, emit, *, taps, stride, tm, chunk, w_t=False, scale=None):
    keep = (taps - 1) * stride
    hrows = -(-keep // SUBLANES) * SUBLANES
    mm = _dot_nt if w_t else _dot
    tails = [jnp.zeros((hrows, w.shape[0 if w_t else 1]), F32) if h is None else h[...]
             for h, w in zip(h_refs, ws)]
    assert all(tl.shape[0] == hrows for tl in tails) and chunk >= hrows
    for c in range(tm // chunk):
        r = slice(c * chunk, (c + 1) * chunk)
        ups = [mm(x_ref[r, :], w) for w in ws]
        if scale is not None:
            ups = [u * scale[r, :] for u in ups]
        emit(r, [_causal_conv(u, cr, tl, taps=taps, stride=stride) for u, cr, tl in zip(ups, c_refs, tails)])
        tails = [u[chunk - hrows:, :] for u in ups]
    return [tl[hrows - keep:, :] for tl in tails]


def _ffn_up_kernel(*refs, taps, stride, tm, chunk, scaled, has_hist, aliased):
    x_ref, wg_ref, wv_ref, cg_ref, cv_ref, bg_ref, bv_ref = refs[:7]
    scale = _rstd(refs[7], x_ref.shape[1]) if scaled else None
    h_refs = (refs[7 + scaled], refs[8 + scaled]) if has_hist else (None, None)
    o_ref, sg_ref, sv_ref = refs[7 + scaled + 2 * has_hist + aliased:]

    tc = wg_ref.shape[1]
    side = lambda a, b: jnp.concatenate([a[...], b[...]], axis=1)
    bias = side(bg_ref, bv_ref)

    def emit(r, convs):
        y = convs[0] + bias
        o_ref[r, :] = (_silu(y[:, :tc]) * y[:, tc:]).astype(o_ref.dtype)

    w = side(wg_ref, wv_ref).astype(BF16)
    hist = side(*h_refs) if has_hist else None
    (tail,) = _proj_conv_chunks(x_ref, [w], (side(cg_ref, cv_ref),), (hist,), emit, taps=taps,
                                stride=stride, tm=tm, chunk=chunk, scale=scale)
    sg_ref[...] = tail[:, :tc]
    sv_ref[...] = tail[:, tc:]


def _ffn_up_act(h, w_up, conv_w, conv_b, *, layer, tm, stride, t0, nt, ss=None, hist=None, prev=None):
    t, kdim = h.shape
    f2 = w_up.shape[-1]
    f = f2 // 2
    taps = conv_w.shape[1]
    keep = (taps - 1) * stride
    tc = _pick(f, 256)
    nc = f // tc
    has_hist = hist is not None
    aliased = prev is not None
    kern = functools.partial(_ffn_up_kernel, taps=taps, stride=stride, tm=tm, chunk=_row_chunk(tm, keep, tm),
                             scaled=ss is not None, has_hist=has_hist, aliased=aliased)
    lo = lambda j: j
    hi = lambda j: j + nc
    conv_b = conv_b.reshape(conv_b.shape[0], 1, f2)
    in_specs = [
        pl.BlockSpec((tm, kdim), lambda i, j: (i + t0, 0), pipeline_mode=pl.Buffered(1)),
        _wspec(w_up, layer, kdim, tc, lo), _wspec(w_up, layer, kdim, tc, hi),
        _wspec(conv_w, layer, taps, tc, lo), _wspec(conv_w, layer, taps, tc, hi),
        _wspec(conv_b, layer, 1, tc, lo), _wspec(conv_b, layer, 1, tc, hi),
    ]
    args = [h, w_up, w_up, conv_w, conv_w, conv_b, conv_b]
    if ss is not None:
        in_specs.append(pl.BlockSpec((tm, LANES), lambda i, j: (i + t0, 0), pipeline_mode=pl.Buffered(1)))
        args.append(ss)
    if has_hist:
        in_specs += [pl.BlockSpec((None, keep, tc), lambda i, j: (i, 0, j)),
                     pl.BlockSpec((None, keep, tc), lambda i, j: (i, 0, j + nc))]
        args += [hist, hist]
    if aliased:
        in_specs.append(pl.BlockSpec(memory_space=pl.ANY))
        args.append(prev)
    sspec = pl.BlockSpec((None, keep, tc), lambda i, j: (i, 0, j))
    return pl.pallas_call(
        kern,
        grid=(nt, nc),
        in_specs=in_specs,
        out_specs=[pl.BlockSpec((tm, tc), lambda i, j: (i + t0, j)), sspec, sspec],
        out_shape=[jax.ShapeDtypeStruct((t, f), BF16), jax.ShapeDtypeStruct((nt, keep, f), F32),
                   jax.ShapeDtypeStruct((nt, keep, f), F32)],
        input_output_aliases={len(args) - 1: 0} if aliased else {},
        compiler_params=_cparams("parallel", "arbitrary"),
        name="ffn_up_conv_act",
    )(*args)


def _proj_conv_kernel(*refs, taps, stride, tm, chunk, has_hist, aliased, w_t):
    x_ref, w_ref, c_ref = refs[:3]
    h_ref = refs[3] if has_hist else None
    o_ref, s_ref = refs[3 + has_hist + aliased:]

    def emit(r, convs):
        o_ref[r, :] = _silu(convs[0])

    (s_ref[...],) = _proj_conv_chunks(x_ref, [w_ref[...].astype(BF16)], (c_ref,), (h_ref,), emit, taps=taps,
                                      stride=stride, tm=tm, chunk=chunk, w_t=w_t)


def _proj_conv(h, w, conv_w, *, layer, tm, stride, t0, nt, w_t=False, hist=None, prev=None):
    t, kdim = h.shape
    taps, c = conv_w.shape[-2:]
    keep = (taps - 1) * stride
    tc = _pick(c, 256)
    has_hist = hist is not None
    aliased = prev is not None
    kern = functools.partial(_proj_conv_kernel, taps=taps, stride=stride, tm=tm,
                             chunk=_row_chunk(tm, keep, ROW_CHUNK),
                             has_hist=has_hist, aliased=aliased, w_t=w_t)
    same = lambda j: j
    in_specs = [pl.BlockSpec((tm, kdim), lambda i, j: (i + t0, 0), pipeline_mode=pl.Buffered(1)),
                _wspec(w, layer, kdim, tc, same, w_t), _wspec(conv_w, layer, taps, tc, same)]
    args = [h, w, conv_w]
    if has_hist:
        in_specs.append(pl.BlockSpec((None, keep, tc), lambda i, j: (i, 0, j)))
        args.append(hist)
    if aliased:
        in_specs.append(pl.BlockSpec(memory_space=pl.ANY))
        args.append(prev)
    return pl.pallas_call(
        kern,
        grid=(nt, c // tc),
        in_specs=in_specs,
        out_specs=[pl.BlockSpec((tm, tc), lambda i, j: (i + t0, j)),
                   pl.BlockSpec((None, keep, tc), lambda i, j: (i, 0, j))],
        out_shape=[jax.ShapeDtypeStruct((t, c), F32), jax.ShapeDtypeStruct((nt, keep, c), F32)],
        input_output_aliases={len(args) - 1: 0} if aliased else {},
        compiler_params=_cparams("parallel", "arbitrary"),
        name="qkv_proj_conv",
    )(*args)


def _pool_kernel(*refs, stride, tm, aliased):
    h_ref, hist_ref, cnt_ref, w_ref, sc_ref, x_ref = refs[:6]
    o_ref = refs[-1]
    g = pl.program_id(1)
    off = POOL_HIST * stride
    cur = h_ref[...]
    run = jnp.concatenate([hist_ref[...], cur], axis=0)
    for k, width in enumerate(POOL_WINDOWS):
        back = (width // 2) * stride
        if back % SUBLANES:
            shifted = pltpu.roll(run, back, axis=0)
        else:
            shifted = jnp.concatenate([run[:back, :], run[:run.shape[0] - back, :]], axis=0)
        run = run + shifted if k == 0 else jnp.where(g >= k, run + shifted, run)
    pooled = run[off:, :] / cnt_ref[...] - cur
    y = _dot(pooled.astype(BF16), w_ref[...]) * sc_ref[...]
    o_ref[...] = x_ref[...] + y


def _pool_mixer(h, hist, cnt, pool_w, scale, x, *, tm, stride, t0, nt, prev=None):
    t, d = h.shape
    g = pool_w.shape[0]
    dg = d // g
    hr = POOL_HIST * stride
    aliased = prev is not None
    kern = functools.partial(_pool_kernel, stride=stride, tm=tm, aliased=aliased)
    in_specs = [
        pl.BlockSpec((tm, dg), lambda i, j: (i + t0, j)),
        pl.BlockSpec((None, hr, dg), lambda i, j: (i, 0, j)),
        pl.BlockSpec((None, tm, 1), lambda i, j: (j, i + t0, 0)),
        pl.BlockSpec((None, dg, dg), lambda i, j: (j, 0, 0)),
        pl.BlockSpec((1, dg), lambda i, j: (0, j)),
        pl.BlockSpec((tm, dg), lambda i, j: (i + t0, j)),
    ]
    args = [h, hist, cnt, pool_w, scale.reshape(1, d), x]
    if aliased:
        in_specs.append(pl.BlockSpec(memory_space=pl.ANY))
        args.append(prev)
    return pl.pallas_call(
        kern,
        grid=(nt, g),
        in_specs=in_specs,
        out_specs=pl.BlockSpec((tm, dg), lambda i, j: (i + t0, j)),
        out_shape=jax.ShapeDtypeStruct((t, d), F32),
        input_output_aliases={6: 0} if aliased else {},
        compiler_params=_cparams("parallel", "arbitrary"),
        name="pool_mixer",
    )(*args)


def _gdn_kernel(*refs, heads, dk, rows, seg, valid, carry, unroll, aliased):
    q_ref, k_ref, v_ref, pa_ref, pb_ref, z_ref, s0_ref, alog_ref, dtb_ref, on_ref, ex_ref = refs[:11]
    o_ref, sn_ref, gcb_s, bb_s, gct_s = refs[11 + aliased:16 + aliased]
    s_s = refs[16 + aliased] if carry else None
    dv = dk
    nseg = rows // seg
    sh = int(math.log2(seg))
    assert 1 << sh == seg and seg % BASE_BLOCK == 0

    if carry:
        @pl.when(pl.program_id(1) == 0)
        def _():
            s_s[...] = s0_ref[...]

    rid = lax.broadcasted_iota(jnp.int32, (rows, 1), 0)
    live = jnp.bitwise_and(rid, seg - 1) < valid
    rseg = lax.shift_right_logical(rid, sh)
    ri = lax.broadcasted_iota(jnp.int32, (rows, rows), 0)
    ci = lax.broadcasted_iota(jnp.int32, (rows, rows), 1)
    same = lax.shift_right_logical(ri, sh) == lax.shift_right_logical(ci, sh)
    causal = jnp.logical_and(same, ci <= ri)
    strict = jnp.logical_and(same, ci < ri)
    base = lax.shift_right_logical(ri, 3) == lax.shift_right_logical(ci, 3)
    eye = (ri == ci).astype(F32)
    levels = []
    s = BASE_BLOCK
    while s < seg:
        rb = lax.shift_right_logical(ri, int(math.log2(s)))
        cb = lax.shift_right_logical(ci, int(math.log2(s)))
        levels.append(jnp.logical_and(jnp.bitwise_and(rb, 1) == 1, cb == rb - 1))
        s *= 2

    a = pa_ref[...] + dtb_ref[...]
    softplus = jnp.maximum(a, 0.0) + jnp.log1p(jnp.exp(-jnp.abs(a)))
    glog = jnp.where(live, -jnp.exp(alog_ref[...]) * softplus, 0.0)
    beta = jnp.where(live, _sigmoid(pb_ref[...]), 0.0)
    gc = _dot(causal.astype(F32), glog, precision=HIGHEST)
    gcb_s[...] = _expand(gc, ex_ref[...], 3)
    bb_s[...] = _expand(beta, ex_ref[...], 2)
    gct_s[...] = gc.T

    def widen(x):
        if rows <= dk:
            return x[:, :rows]
        return jnp.concatenate([x] * (rows // dk), axis=1)

    def group(gi, carry_unused):
        hs = [gi * unroll + j for j in range(unroll)]
        los = [pl.multiple_of(h * dk, dk) for h in hs]
        each = range(unroll)
        qs = [q_ref[:, pl.ds(lo, dk)] for lo in los]
        ks = [k_ref[:, pl.ds(lo, dk)] for lo in los]
        vs = [v_ref[:, pl.ds(lo, dk)] for lo in los]
        zs = [z_ref[:, pl.ds(lo, dv)] for lo in los]
        gcbs = [gcb_s[:, pl.ds(lo, dk)] for lo in los]
        bbs = [bb_s[:, pl.ds(lo, dk)] for lo in los]
        grow = [gct_s[pl.ds(h, 1), :] for h in hs]
        if carry:
            sts = [s_s[h] for h in hs]
        qs = [q * lax.rsqrt(jnp.sum(q * q, axis=-1, keepdims=True) + 1e-6) * (dk ** -0.5) for q in qs]
        ks = [k * lax.rsqrt(jnp.sum(k * k, axis=-1, keepdims=True) + 1e-6) for k in ks]
        egcs = [jnp.exp(g) for g in gcbs]
        decays = [jnp.where(causal, jnp.exp(jnp.where(causal, widen(gcbs[j]) - grow[j], 0.0)), 0.0)
                  for j in each]
        kbf = [k.astype(BF16) for k in ks]
        qkks = [_dot_nt(jnp.concatenate([qs[j].astype(BF16), kbf[j]], axis=0), kbf[j]) for j in each]
        qks = [qkks[j][:rows] * decays[j] for j in each]
        amats = [jnp.where(strict, widen(bbs[j]) * decays[j] * qkks[j][rows:], 0.0) for j in each]
        pws = [-jnp.where(base, a, 0.0) for a in amats]
        invs = [eye + p for p in pws]
        for _ in range(2):
            pws = [_mm(p, p) for p in pws]
            invs = [invs[j] + _mm(pws[j], invs[j]) for j in each]
        for lvl in levels:
            lows = [_mm(jnp.where(lvl, amats[j], 0.0), invs[j]) for j in each]
            invs = [invs[j] - _mm(invs[j], lows[j]) for j in each]
        sols = [_mm(invs[j], jnp.concatenate([bbs[j] * vs[j], bbs[j] * egcs[j] * ks[j]], axis=1))
                for j in each]
        wvs = [sl[:, :dv] for sl in sols]
        wks = [sl[:, dv:] for sl in sols]
        qds = [qs[j] * egcs[j] for j in each]
        if carry:
            us = [[] for _ in each]
            os_ = [[] for _ in each]
            for sg in range(nseg):
                r = slice(sg * seg, (sg + 1) * seg)
                last = slice((sg + 1) * seg - 1, (sg + 1) * seg)
                both = [_dot(jnp.concatenate([wks[j][r], qds[j][r]], axis=0).astype(BF16), sts[j].astype(BF16))
                        for j in each]
                u = [wvs[j][r] - both[j][:seg] for j in each]
                kend = [ks[j][r] * jnp.exp(gcbs[j][last] - gcbs[j][r]) for j in each]
                sts = [jnp.exp(gcbs[j][last]) * sts[j] + _dot_tn(kend[j].astype(BF16), u[j].astype(BF16))
                       for j in each]
                for j in each:
                    us[j].append(u[j])
                    os_[j].append(both[j][seg:])
            u_all = [x[0] if nseg == 1 else jnp.concatenate(x, axis=0) for x in us]
            o = [x[0] if nseg == 1 else jnp.concatenate(x, axis=0) for x in os_]
        else:
            lhs = [jnp.concatenate([wks[j], qds[j]], axis=0).astype(BF16) for j in each]
            u_all = [jnp.zeros((rows, dv), F32) for _ in each]
            o = [jnp.zeros((rows, dv), F32) for _ in each]
            news = []
            for sg in range(nseg):
                last = slice((sg + 1) * seg - 1, (sg + 1) * seg)
                mine = rseg == sg
                st0 = [s0_ref[sg, h] for h in hs]
                both = [_dot(lhs[j], st0[j].astype(BF16)) for j in each]
                u_m = [jnp.where(mine, wvs[j] - both[j][:rows], 0.0) for j in each]
                u_all = [u_all[j] + u_m[j] for j in each]
                o = [o[j] + jnp.where(mine, both[j][rows:], 0.0) for j in each]
                kend = [jnp.where(mine, ks[j] * jnp.exp(jnp.where(mine, gcbs[j][last] - gcbs[j], 0.0)), 0.0)
                        for j in each]
                news.append([jnp.exp(gcbs[j][last]) * st0[j] + _dot_tn(kend[j].astype(BF16), u_m[j].astype(BF16))
                             for j in each])
        o = [o[j] + _mm(qks[j], u_all[j]) for j in each]
        o = [o[j] * lax.rsqrt(jnp.mean(o[j] * o[j], axis=-1, keepdims=True) + EPS) * on_ref[...] * _silu(zs[j])
             for j in each]
        for j in each:
            o_ref[:, pl.ds(los[j], dv)] = o[j].astype(o_ref.dtype)
            if carry:
                s_s[hs[j]] = sts[j]
            else:
                for sg in range(nseg):
                    sn_ref[sg, hs[j]] = news[sg][j]
        return carry_unused

    assert heads % unroll == 0
    lax.fori_loop(0, heads // unroll, group, 0)

    if carry:
        @pl.when(pl.program_id(1) == pl.num_programs(1) - 1)
        def _():
            sn_ref[...] = s_s[...]


def _gdn(qkv, proj, s0, a_log, dt_bias, out_norm, *, cols, nseq, seqlen, rows, seg, valid, carry,
         out_shape=None, out_block=0, prev=None):
    _, heads, dk, dv = s0.shape
    assert dk == LANES and dv == LANES and heads <= LANES and rows % seg == 0
    hk = heads * dk
    expand = np.zeros((LANES, hk), np.float32)
    for h in range(heads):
        expand[h, h * dk:(h + 1) * dk] = 1.0
    lane_pad = lambda t: jnp.pad(t.reshape(1, heads).astype(F32), ((0, 0), (0, LANES - heads)))
    col_z, col_a, col_b = cols
    assert col_z % hk == 0 and col_a % LANES == 0 and col_b % LANES == 0
    if carry:
        nblk = seqlen // rows
        grid = (nseq, nblk)
        rmap = lambda i, c: i * nblk + c
        smap = lambda i, c: (i, 0, 0, 0)
        sblock = (None, heads, dk, dv)
        sem = ("parallel", "arbitrary")
    else:
        spb = rows // seg
        grid = (nseq // spb, 1)
        rmap = lambda i, c: i
        smap = lambda i, c: (i, 0, 0, 0)
        sblock = (spb, heads, dk, dv)
        sem = ("parallel", "arbitrary")
    rspec = lambda w, cb: pl.BlockSpec((rows, w), lambda i, c: (rmap(i, c), cb))
    full = lambda shape: pl.BlockSpec(shape, lambda i, c: (0,) * len(shape))
    aliased = prev is not None
    kern = functools.partial(_gdn_kernel, heads=heads, dk=dk, rows=rows, seg=seg, valid=valid, carry=carry,
                             unroll=min(heads, 8) if carry else min(heads, 4), aliased=aliased)
    scratch = [pltpu.VMEM((rows, hk), F32), pltpu.VMEM((rows, hk), F32), pltpu.VMEM((LANES, rows), F32)]
    if carry:
        scratch.append(pltpu.VMEM((heads, dk, dv), F32))
    if out_shape is None:
        out_shape = (nseq * seqlen, heads * dv)
    in_specs = [
        rspec(hk, 0), rspec(hk, 1), rspec(hk, 2),
        rspec(LANES, col_a // LANES), rspec(LANES, col_b // LANES), rspec(hk, col_z // hk),
        pl.BlockSpec(sblock, smap),
        full((1, LANES)), full((1, LANES)), full((1, dv)), full((LANES, hk)),
    ]
    args = [qkv, qkv, qkv, proj, proj, proj, s0, lane_pad(a_log), lane_pad(dt_bias),
            out_norm.reshape(1, dv).astype(F32), jnp.asarray(expand, dtype=BF16)]
    if aliased:
        in_specs.append(pl.BlockSpec(memory_space=pl.ANY))
        args.append(prev)
    return pl.pallas_call(
        kern,
        grid=grid,
        in_specs=in_specs,
        out_specs=[rspec(hk, out_block), pl.BlockSpec(sblock, smap)],
        out_shape=[jax.ShapeDtypeStruct(out_shape, BF16), jax.ShapeDtypeStruct(s0.shape, F32)],
        scratch_shapes=scratch,
        input_output_aliases={len(args) - 1: 0} if aliased else {},
        compiler_params=_cparams(*sem),
        name="gated_deltanet",
    )(*args)


def _ret_kernel(*refs, hps, dk, rows, seg, carry, aliased):
    q_ref, k_ref, v_ref, g_ref, cos_ref, sin_ref, s0_ref, dec_ref, qs_ref, ks_ref, ge_ref = refs[:11]
    o_ref, sn_ref = refs[11 + aliased:13 + aliased]
    s_s = refs[13 + aliased] if carry else None
    dv = dk
    half = dk // 2
    nseg = rows // seg
    sh = int(math.log2(seg))
    hstep = pl.program_id(1) if not carry else 0

    if carry:
        @pl.when(pl.program_id(1) == 0)
        def _():
            s_s[...] = s0_ref[...]

    cos = cos_ref[...]
    sin = sin_ref[...]
    rseg = lax.shift_right_logical(lax.broadcasted_iota(jnp.int32, (rows, 1), 0), sh)

    def rot(x):
        x1, x2 = x[:, :half], x[:, half:]
        return jnp.concatenate([x1 * cos - x2 * sin, x1 * sin + x2 * cos], axis=1)

    def head(hh, carry_unused):
        h = hstep * hps + hh
        lo = pl.multiple_of(h * dk, dk)
        q = rot(q_ref[:, pl.ds(lo, dk)])
        k = rot(k_ref[:, pl.ds(lo, dk)]) * (dk ** -0.5)
        v_bf = v_ref[:, pl.ds(lo, dv)].astype(BF16)
        qk = _dot_nt(q.astype(BF16), k.astype(BF16)) * dec_ref[h]
        qd = (q * qs_ref[:, pl.ds(lo, dk)]).astype(BF16)
        ke = k * ks_ref[:, pl.ds(lo, dk)]
        if carry:
            st = s_s[h]
            os_ = []
            for sg in range(nseg):
                r0 = sg * seg
                os_.append(_dot(qd[r0:r0 + seg], st.astype(BF16)))
                st = ge_ref[h] * st + _dot_tn(ke[r0:r0 + seg].astype(BF16), v_bf[r0:r0 + seg])
            s_s[h] = st
            o = os_[0] if nseg == 1 else jnp.concatenate(os_, axis=0)
        else:
            o = jnp.zeros((rows, dv), F32)
            for sg in range(nseg):
                mine = rseg == sg
                st = s0_ref[sg, hh]
                o = o + jnp.where(mine, _dot(qd, st.astype(BF16)), 0.0)
                sn_ref[sg, hh] = ge_ref[h] * st + _dot_tn(jnp.where(mine, ke, 0.0).astype(BF16), v_bf)
        o = o + _dot(qk.astype(BF16), v_bf)
        mu = jnp.mean(o, axis=-1, keepdims=True)
        var = jnp.mean(jnp.square(o - mu), axis=-1, keepdims=True)
        o = (o - mu) * lax.rsqrt(var + EPS) * _silu(g_ref[:, pl.ds(lo, dv)])
        o_ref[:, pl.ds(lo, dv)] = o.astype(o_ref.dtype)
        return carry_unused

    lax.fori_loop(0, hps, head, 0)

    if carry:
        @pl.when(pl.program_id(1) == pl.num_programs(1) - 1)
        def _():
            sn_ref[...] = s_s[...]


def _retention(proj, s0, *, col_q, nseq, seqlen, rows, seg, valid, pos0, carry,
               out_shape=None, out_block=0, prev=None):
    _, heads, dk, dv = s0.shape
    assert dk == dv and rows % seg == 0
    hw = heads * dk
    half = dk // 2
    assert col_q % hw == 0
    cq = col_q // hw
    npos = seqlen if carry else rows
    offs = np.arange(npos) if carry else np.arange(rows) % seg
    inv_freq = ROPE_BASE ** (-np.arange(half, dtype=np.float64) / half)
    ang = (pos0 + offs.astype(np.float64))[:, None] * inv_freq[None, :]
    cos, sin = np.cos(ang).astype(np.float32), np.sin(ang).astype(np.float32)
    log_gamma = np.log1p(-np.exp2(-5.0 - np.arange(heads, dtype=np.float64)))
    pin = np.arange(rows) % seg
    steps = np.minimum(pin + 1, valid).astype(np.float64)
    gc = log_gamma[:, None] * steps[None, :]
    gl = log_gamma * min(seg, valid)
    sameseg = (np.arange(rows)[:, None] // seg) == (np.arange(rows)[None, :] // seg)
    causal = np.logical_and(sameseg, np.tril(np.ones((rows, rows), bool)))
    diff = gc[:, :, None] - gc[:, None, :]
    dec = np.where(causal[None], np.exp(np.where(causal[None], diff, 0.0)), 0.0).astype(np.float32)
    qs = np.repeat(np.exp(gc).T, dk, axis=1).astype(np.float32)
    ks = np.repeat(np.exp(gl[:, None] - gc).T, dk, axis=1).astype(np.float32)
    ge = np.broadcast_to(np.exp(gl)[:, None, None], (heads, 1, dv)).astype(np.float32)
    if carry:
        nblk = seqlen // rows
        hps = heads
        grid = (nseq, nblk)
        rmap = lambda i, c: i * nblk + c
        tmap = lambda i, c: (c, 0)
        smap = lambda i, c: (i, 0, 0, 0)
        sblock = (None, heads, dk, dv)
    else:
        spb = rows // seg
        hps = max(1, heads // 2)
        grid = (nseq // spb, heads // hps)
        rmap = lambda i, c: i
        tmap = lambda i, c: (0, 0)
        smap = lambda i, c: (i, c, 0, 0)
        sblock = (spb, hps, dk, dv)
    rspec = lambda cb: pl.BlockSpec((rows, hw), lambda i, c: (rmap(i, c), cb))
    full = lambda shape: pl.BlockSpec(shape, lambda i, c: (0,) * len(shape))
    aliased = prev is not None
    kern = functools.partial(_ret_kernel, hps=hps, dk=dk, rows=rows, seg=seg, carry=carry, aliased=aliased)
    if out_shape is None:
        out_shape = (nseq * seqlen, heads * dv)
    in_specs = [
        rspec(cq), rspec(cq + 1), rspec(cq + 2), rspec(cq + 3),
        pl.BlockSpec((rows, half), tmap), pl.BlockSpec((rows, half), tmap),
        pl.BlockSpec(sblock, smap),
        full((heads, rows, rows)), full((rows, hw)), full((rows, hw)), full((heads, 1, dv)),
    ]
    args = [proj, proj, proj, proj, jnp.asarray(cos), jnp.asarray(sin), s0, jnp.asarray(dec), jnp.asarray(qs),
            jnp.asarray(ks), jnp.asarray(ge)]
    if aliased:
        in_specs.append(pl.BlockSpec(memory_space=pl.ANY))
        args.append(prev)
    return pl.pallas_call(
        kern,
        grid=grid,
        in_specs=in_specs,
        out_specs=[rspec(out_block), pl.BlockSpec(sblock, smap)],
        out_shape=[jax.ShapeDtypeStruct(out_shape, BF16), jax.ShapeDtypeStruct(s0.shape, F32)],
        scratch_shapes=[pltpu.VMEM((heads, dk, dv), F32)] if carry else [],
        input_output_aliases={len(args) - 1: 0} if aliased else {},
        compiler_params=_cparams("parallel", "arbitrary"),
        name="retention",
    )(*args)


def _pack_w_rest(wt, gdn_c, heads_a, gdn_v):
    off_a = gdn_c
    off_b = off_a + heads_a
    off_z = off_b + heads_a
    off_r = off_z + gdn_v
    zpad = jnp.zeros((LANES - heads_a, wt.shape[1]), wt.dtype)
    return jnp.concatenate([wt[off_z:off_r], wt[off_r:], wt[off_a:off_b], zpad, wt[off_b:off_z], zpad], axis=0)


def kernel(x_prompt, x_sample, state_gdn_conv, state_gdn, state_ret, state_pool, state_ffn_conv, p_prompt, p_sample, norm_mix, norm_ffn, norm_ple, norm_final, w_in, gdn_conv_w, gdn_a_log, gdn_dt_bias, gdn_out_norm, w_out, pool_w, pool_scale, ffn_w_up, ffn_conv_w, ffn_conv_b, ffn_w_down, ple_w_gate, ple_w_proj):
    bp, lp, d = x_prompt.shape
    bs, ls, _ = x_sample.shape
    depth = p_prompt.shape[0]
    tp, ts = bp * lp, bs * ls
    t = tp + ts
    tm = ts
    assert lp % tm == 0 and tm % SUBLANES == 0
    ntp = tp // tm
    _, _, heads_a, dk_a, dv_a = state_gdn.shape
    _, _, heads_r, dk_r, dv_r = state_ret.shape
    gdn_c = state_gdn_conv.shape[-1]
    gdn_v = heads_a * dv_a
    ret_w = heads_r * dk_r
    gtaps = gdn_conv_w.shape[1]
    ftaps = ffn_conv_w.shape[1]
    pool_past = state_pool.shape[2]
    f2 = ffn_w_up.shape[-1]
    assert min(lp, ls) >= max(gtaps, ftaps) - 1 and pool_past == POOL_HIST - 1 and lp % SUBLANES == 0
    assert lp % CHUNK == 0 and ls <= BASE_BLOCK and gdn_c == 3 * gdn_v and heads_a * dk_a == gdn_v
    assert gdn_v % ret_w == 0 and heads_r * dv_r == ret_w
    col_z, col_r = 0, gdn_v
    col_a = col_r + 4 * ret_w
    col_b = col_a + LANES

    rows_p = lambda a: a.reshape((tp,) + a.shape[2:])
    rows_s = lambda a: jnp.swapaxes(a, 0, 1).reshape((ts,) + a.shape[2:])
    join = lambda a, b: jnp.concatenate([rows_p(a), rows_s(b)], axis=0)
    seq_p = lambda a: a[:tp].reshape((bp, lp) + a.shape[1:])
    seq_s = lambda a: jnp.swapaxes(a[tp:].reshape((ls, bs) + a.shape[1:]), 0, 1)

    tps = lp // tm
    tiles = lambda a: a.reshape((t // tm, tm) + a.shape[1:])

    def hist_p(a, nrows):
        tl = tiles(a)[:ntp, tm - nrows:]
        prev = jnp.concatenate([jnp.zeros_like(tl[:1]), tl[:-1]], axis=0)
        first = (jnp.arange(ntp) % tps == 0)[:, None, None]
        return jnp.where(first, 0.0, prev)

    def tail_p(a, nrows):
        return tiles(a)[tps - 1:ntp:tps, tm - nrows:]

    def hist_s(state, nrows):
        return jnp.swapaxes(state[:, -nrows:], 0, 1).reshape(1, nrows * bs, state.shape[-1])

    def pad_seq(a):
        a = seq_s(a)
        a = jnp.pad(a, ((0, 0), (0, BASE_BLOCK - ls)) + ((0, 0),) * (a.ndim - 2))
        return a.reshape((bs * BASE_BLOCK,) + a.shape[2:])

    unpad_seq = lambda a: rows_s(a.reshape((bs, BASE_BLOCK) + a.shape[1:])[:, :ls])

    xr = join(x_prompt, x_sample)
    pj = [join(p_prompt[i], p_sample[i]).astype(BF16) for i in range(depth)]
    pool_w_bf = pool_w.astype(BF16)
    w_down_bf = ffn_w_down.astype(BF16)
    w_proj_bf = ple_w_proj.astype(BF16)
    w_in_t = jnp.swapaxes(w_in, 1, 2)
    rows_prompt = _pick(lp, 256, CHUNK)
    spb = max(1, CHUNK // BASE_BLOCK)
    assert bs % spb == 0

    new_gdn_conv_p, new_gdn_p, new_ret_p, new_pool_p, new_ffn_p = [], [], [], [], []
    new_gdn_conv_s, new_gdn_s, new_ret_s, new_pool_s, new_ffn_s = [], [], [], [], []
    for i in range(depth):
        j = i // 2
        if i % 2 == 0:
            (h,) = _rmsnorm(xr, norm_mix[i], [BF16])
            qkv, cs_p = _proj_conv(h, w_in_t, gdn_conv_w, layer=j, tm=lp, stride=1, t0=0, nt=bp, w_t=True)
            qkv, cs_s = _proj_conv(h, w_in_t, gdn_conv_w, layer=j, tm=tm, stride=bs, t0=ntp, nt=1, w_t=True,
                                   hist=hist_s(state_gdn_conv[j], gtaps - 1), prev=qkv)
            proj = _matmul(h, _pack_w_rest(w_in_t[j], gdn_c, heads_a, gdn_v), w_t=True, name="in_proj")
            cols = (col_z, col_a, col_b)
            mshape = (t, gdn_v + heads_r * dv_r)
            mixed, s_a_p = _gdn(qkv, proj, jnp.zeros((bp,) + state_gdn.shape[2:], F32), gdn_a_log[j],
                                gdn_dt_bias[j], gdn_out_norm[j], cols=cols, nseq=bp, seqlen=lp,
                                rows=rows_prompt, seg=CHUNK, valid=CHUNK, carry=True, out_shape=mshape)
            mixed, s_b_p = _retention(proj, jnp.zeros((bp,) + state_ret.shape[2:], F32), col_q=col_r,
                                      nseq=bp, seqlen=lp, rows=rows_prompt, seg=CHUNK, valid=CHUNK, pos0=0,
                                      carry=True, out_shape=mshape, out_block=gdn_v // ret_w, prev=mixed)
            proj_s = pad_seq(proj)
            o_gdn_s, s_a_s = _gdn(pad_seq(qkv), proj_s, state_gdn[j], gdn_a_log[j], gdn_dt_bias[j],
                                  gdn_out_norm[j], cols=cols, nseq=bs, seqlen=BASE_BLOCK,
                                  rows=spb * BASE_BLOCK, seg=BASE_BLOCK, valid=ls, carry=False)
            o_ret_s, s_b_s = _retention(proj_s, state_ret[j], col_q=col_r, nseq=bs, seqlen=BASE_BLOCK,
                                        rows=spb * BASE_BLOCK, seg=BASE_BLOCK, valid=ls, pos0=PAST_LEN,
                                        carry=False)
            mixed_s = jnp.concatenate([unpad_seq(o_gdn_s), unpad_seq(o_ret_s)], axis=1)
            mixed = lax.dynamic_update_slice(mixed, mixed_s, (tp, 0))
            xr, h, ss = _matmul(mixed, w_out, layer=j, res=xr, gain=norm_ffn[i], name="out_proj")
            new_gdn_conv_p.append(cs_p)
            new_gdn_conv_s.append(jnp.swapaxes(cs_s.reshape(gtaps - 1, bs, gdn_c), 0, 1))
            new_gdn_p.append(s_a_p)
            new_gdn_s.append(s_a_s)
            new_ret_p.append(s_b_p)
            new_ret_s.append(s_b_s)
        else:
            h32, = _rmsnorm(xr, norm_mix[i], [F32])
            h_s = seq_s(h32)
            pos = jnp.concatenate([jnp.tile(jnp.arange(lp, dtype=jnp.int32), bp),
                                   jnp.repeat(PAST_LEN + jnp.arange(ls, dtype=jnp.int32), bs)])
            cnt = jnp.stack([jnp.minimum(w, pos + 1).astype(F32) for w in POOL_WINDOWS])[:, :, None]
            zero_row = jnp.zeros((bs, 1, d), F32)
            kw = dict(tm=tm)
            xn = _pool_mixer(h32, hist_p(h32, POOL_HIST), cnt, pool_w_bf[j], pool_scale[j], xr, stride=1,
                             t0=0, nt=ntp, **kw)
            xr = _pool_mixer(h32, hist_s(jnp.concatenate([zero_row, state_pool[j]], axis=1), POOL_HIST), cnt,
                             pool_w_bf[j], pool_scale[j], xr, stride=bs, t0=ntp, nt=1, prev=xn, **kw)
            new_pool_p.append(tail_p(h32, pool_past))
            new_pool_s.append(jnp.concatenate([state_pool[j], h_s], axis=1)[:, -pool_past:])
            ((h,), ss) = _rmsnorm(xr, norm_ffn[i], [BF16]), None
        act, sg_p, sv_p = _ffn_up_act(h, ffn_w_up, ffn_conv_w, ffn_conv_b, layer=i, tm=lp, stride=1, t0=0, nt=bp,
                                      ss=ss)
        act, sg_s, sv_s = _ffn_up_act(h, ffn_w_up, ffn_conv_w, ffn_conv_b, layer=i, tm=tm, stride=bs, t0=ntp,
                                      nt=1, ss=ss, hist=hist_s(state_ffn_conv[i], ftaps - 1), prev=act)
        xr, h, ss = _matmul(act, w_down_bf, layer=i, res=xr, gain=norm_ple[i], name="ffn_down")
        new_ffn_p.append(jnp.concatenate([sg_p, sv_p], axis=-1))
        up_s = jnp.concatenate([sg_s, sv_s], axis=-1).reshape(ftaps - 1, bs, f2)
        new_ffn_s.append(jnp.swapaxes(up_s, 0, 1))
        xr = _matmul(h, ple_w_gate, layer=i, ss=ss, res=xr, p=pj[i], wp=w_proj_bf, name="ple")
    (out_p,) = _rmsnorm(xr, norm_final, [F32], 0, tp)
    (out_s,) = _rmsnorm(xr, norm_final, [F32], tp, ts)
    st = lambda lst: jnp.stack(lst)
    return (out_p.reshape(bp, lp, d), jnp.swapaxes(out_s.reshape(ls, bs, d), 0, 1),
            st(new_gdn_conv_p), st(new_gdn_p), st(new_ret_p), st(new_pool_p), st(new_ffn_p),
            st(new_gdn_conv_s), st(new_gdn_s), st(new_ret_s), st(new_pool_s), st(new_ffn_s))
```

```python
import functools
import math

import numpy as np
import jax
import jax.numpy as jnp
from jax import lax
from jax.experimental import pallas as pl
from jax.experimental.pallas import tpu as pltpu

F32 = jnp.float32
BF16 = jnp.bfloat16
HIGHEST = lax.Precision.HIGHEST

EPS = 1e-6
CHUNK = 64
PAST_LEN = 16384
ROPE_BASE = 10000.0
POOL_WINDOWS = (2, 4, 8, 16)
POOL_HIST = 16
LANES = 128
SUBLANES = 8
VMEM_LIMIT = 56 * 1024 * 1024
X_TILE_BYTES = 24 * 1024 * 1024
BASE_BLOCK = 8
RET_GROUP = 4
ROW_CHUNK = 256


def _pick(n, pref, align=LANES):
    if n <= pref:
        return n
    t = (pref // align) * align
    while t >= align:
        if n % t == 0:
            return t
        t -= align
    return n


def _cparams(*sem):
    return pltpu.CompilerParams(dimension_semantics=sem, vmem_limit_bytes=VMEM_LIMIT)


def _silu(x):
    return x * (1.0 / (1.0 + jnp.exp(-x)))


def _sigmoid(x):
    return 1.0 / (1.0 + jnp.exp(-x))


def _dot(a, b, **kw):
    return jnp.dot(a, b, preferred_element_type=F32, **kw)


def _dot_nt(a, b, **kw):
    return lax.dot_general(a, b, (((1,), (1,)), ((), ())), preferred_element_type=F32, **kw)


def _dot_tn(a, b, **kw):
    return lax.dot_general(a, b, (((0,), (0,)), ((), ())), preferred_element_type=F32, **kw)


def _mm(a, b):
    return _dot(a.astype(BF16), b.astype(BF16))


def _expand(x, e_bf, passes):
    acc = None
    for _ in range(passes):
        piece = x.astype(BF16)
        term = _dot(piece, e_bf)
        acc = term if acc is None else acc + term
        x = x - piece.astype(F32)
    return acc


def _rmsnorm_kernel(x_ref, g_ref, *o_refs):
    x = x_ref[...]
    y = x * lax.rsqrt(jnp.mean(x * x, axis=-1, keepdims=True) + EPS) * g_ref[...]
    for o_ref in o_refs:
        o_ref[...] = y.astype(o_ref.dtype)


def _rmsnorm(x, gain, dtypes, row0=0, nrows=None):
    d = x.shape[1]
    m = x.shape[0] if nrows is None else nrows
    tm = _pick(math.gcd(m, row0) if row0 else m, 256, SUBLANES)
    t0 = row0 // tm
    return pl.pallas_call(
        _rmsnorm_kernel,
        grid=(m // tm,),
        in_specs=[pl.BlockSpec((tm, d), lambda i: (i + t0, 0)), pl.BlockSpec((1, d), lambda i: (0, 0))],
        out_specs=[pl.BlockSpec((tm, d), lambda i: (i, 0)) for _ in dtypes],
        out_shape=[jax.ShapeDtypeStruct((m, d), dt) for dt in dtypes],
        compiler_params=_cparams("parallel"),
        name="rmsnorm",
    )(x, gain.reshape(1, d))


def _rstd(ss_ref, width):
    return lax.rsqrt(jnp.sum(ss_ref[...], axis=-1, keepdims=True) * (1.0 / width) + EPS)


def _mm_kernel(*refs, epilogue, w_t, scaled, emit_norm):
    it = iter(refs)
    x_ref, w_ref = next(it), next(it)
    ss_in = next(it) if scaled else None
    res_ref = next(it) if epilogue in ("res", "ple") else None
    p_ref, wp_ref = (next(it), next(it)) if epilogue == "ple" else (None, None)
    gn_ref = next(it) if emit_norm else None
    o_ref = next(it)
    acc = (_dot_nt if w_t else _dot)(x_ref[...], w_ref[...].astype(BF16))
    if scaled:
        acc = acc * _rstd(ss_in, x_ref.shape[1])
    if epilogue == "res":
        acc = res_ref[...] + acc
    elif epilogue == "ple":
        acc = res_ref[...] + _sigmoid(acc) * _dot(p_ref[...], wp_ref[...])
    o_ref[...] = acc.astype(o_ref.dtype)
    if emit_norm:
        xb_ref, ss_ref = next(it), next(it)
        xb_ref[...] = (acc * gn_ref[...]).astype(BF16)

        @pl.when(pl.program_id(1) == 0)
        def _():
            ss_ref[...] = jnp.zeros(ss_ref.shape, F32)

        sq = acc * acc
        part = sq[:, :LANES]
        for k in range(1, sq.shape[1] // LANES):
            part = part + sq[:, k * LANES:(k + 1) * LANES]
        ss_ref[...] += part


def _wspec(w, layer, rows, tc, cmap, w_t=False):
    block, index = ((tc, rows), lambda j: (cmap(j), 0)) if w_t else ((rows, tc), lambda j: (0, cmap(j)))
    if w.ndim == 2:
        return pl.BlockSpec(block, lambda i, j: index(j))
    return pl.BlockSpec((None,) + block, lambda i, j: (layer,) + index(j))


def _matmul(x, w, *, layer=0, w_t=False, ss=None, res=None, p=None, wp=None, gain=None, tn=256,
            out_dtype=F32, name="matmul"):
    m, kdim = x.shape
    n = w.shape[-2] if w_t else w.shape[-1]
    tm = _pick(m, X_TILE_BYTES // (2 * kdim), 2 * SUBLANES)
    tn = _pick(n, tn)
    epilogue = "none" if res is None else ("res" if p is None else "ple")
    rows = lambda width: pl.BlockSpec((tm, width), lambda i, j: (i, 0), pipeline_mode=pl.Buffered(1))
    in_specs = [rows(kdim), _wspec(w, layer, kdim, tn, lambda j: j, w_t)]
    args = [x, w]
    if ss is not None:
        in_specs.append(rows(LANES))
        args.append(ss)
    if res is not None:
        in_specs.append(pl.BlockSpec((tm, tn), lambda i, j: (i, j)))
        args.append(res)
    if p is not None:
        in_specs += [rows(p.shape[1]), _wspec(wp, layer, p.shape[1], tn, lambda j: j)]
        args += [p, wp]
    tile = pl.BlockSpec((tm, tn), lambda i, j: (i, j))
    out_specs, out_shape = tile, jax.ShapeDtypeStruct((m, n), out_dtype)
    if gain is not None:
        in_specs.append(pl.BlockSpec((1, tn), lambda i, j: (0, j)))
        args.append(gain.reshape(1, n))
        out_specs = [tile, tile, pl.BlockSpec((tm, LANES), lambda i, j: (i, 0))]
        out_shape = [out_shape, jax.ShapeDtypeStruct((m, n), BF16), jax.ShapeDtypeStruct((m, LANES), F32)]
    return pl.pallas_call(
        functools.partial(_mm_kernel, epilogue=epilogue, w_t=w_t, scaled=ss is not None,
                          emit_norm=gain is not None),
        grid=(m // tm, n // tn),
        in_specs=in_specs,
        out_specs=out_specs,
        out_shape=out_shape,
        compiler_params=_cparams("parallel", "arbitrary"),
        name=name,
    )(*args)


def _row_chunk(tm, keep, pref):
    chunk = _pick(tm, pref, 2 * SUBLANES)
    return chunk if chunk >= keep else tm


def _causal_conv(cur, w_ref, hist, *, taps, stride):
    rows, hrows = cur.shape[0], hist.shape[0]
    ext = jnp.concatenate([hist, cur], axis=0)
    acc = w_ref[taps - 1:taps, :] * cur
    for j in range(taps - 1):
        back = (taps - 1 - j) * stride
        if back % SUBLANES == 0:
            piece = ext[hrows - back:hrows - back + rows, :]
        else:
            piece = pltpu.roll(ext, back, axis=0)[hrows:, :]
        acc = acc + w_ref[j:j + 1, :] * piece
    return acc


def _proj_conv_chunks(x_ref, ws, c_refs, h_refs, emit, *, taps, stride, tm, chunk, w_t=False, scale=None):
    keep = (taps - 1) * stride
    hrows = -(-keep // SUBLANES) * SUBLANES
    mm = _dot_nt if w_t else _dot
    tails = [jnp.zeros((hrows, w.shape[0 if w_t else 1]), F32) if h is None else h[...]
             for h, w in zip(h_refs, ws)]
    assert all(tl.shape[0] == hrows for tl in tails) and chunk >= hrows
    for c in range(tm // chunk):
        r = slice(c * chunk, (c + 1) * chunk)
        ups = [mm(x_ref[r, :], w) for w in ws]
        if scale is not None:
            ups = [u * scale[r, :] for u in ups]
        emit(r, [_causal_conv(u, cr, tl, taps=taps, stride=stride) for u, cr, tl in zip(ups, c_refs, tails)])
        tails = [u[chunk - hrows:, :] for u in ups]
    return [tl[hrows - keep:, :] for tl in tails]


def _ffn_up_kernel(*refs, taps, stride, tm, chunk, scaled, has_hist, aliased):
    x_ref, wg_ref, wv_ref, cg_ref, cv_ref, bg_ref, bv_ref = refs[:7]
    scale = _rstd(refs[7], x_ref.shape[1]) if scaled else None
    h_refs = (refs[7 + scaled], refs[8 + scaled]) if has_hist else (None, None)
    o_ref, sg_ref, sv_ref = refs[7 + scaled + 2 * has_hist + aliased:]

    tc = wg_ref.shape[1]
    side = lambda a, b: jnp.concatenate([a[...], b[...]], axis=1)
    bias = side(bg_ref, bv_ref)

    def emit(r, convs):
        y = convs[0] + bias
        o_ref[r, :] = (_silu(y[:, :tc]) * y[:, tc:]).astype(o_ref.dtype)

    w = side(wg_ref, wv_ref).astype(BF16)
    hist = side(*h_refs) if has_hist else None
    (tail,) = _proj_conv_chunks(x_ref, [w], (side(cg_ref, cv_ref),), (hist,), emit, taps=taps,
                                stride=stride, tm=tm, chunk=chunk, scale=scale)
    sg_ref[...] = tail[:, :tc]
    sv_ref[...] = tail[:, tc:]


def _ffn_up_act(h, w_up, conv_w, conv_b, *, layer, tm, stride, t0, nt, ss=None, hist=None, prev=None):
    t, kdim = h.shape
    f2 = w_up.shape[-1]
    f = f2 // 2
    taps = conv_w.shape[1]
    keep = (taps - 1) * stride
    tc = _pick(f, 256)
    nc = f // tc
    has_hist = hist is not None
    aliased = prev is not None
    kern = functools.partial(_ffn_up_kernel, taps=taps, stride=stride, tm=tm, chunk=_row_chunk(tm, keep, tm),
                             scaled=ss is not None, has_hist=has_hist, aliased=aliased)
    lo = lambda j: j
    hi = lambda j: j + nc
    conv_b = conv_b.reshape(conv_b.shape[0], 1, f2)
    in_specs = [
        pl.BlockSpec((tm, kdim), lambda i, j: (i + t0, 0), pipeline_mode=pl.Buffered(1)),
        _wspec(w_up, layer, kdim, tc, lo), _wspec(w_up, layer, kdim, tc, hi),
        _wspec(conv_w, layer, taps, tc, lo), _wspec(conv_w, layer, taps, tc, hi),
        _wspec(conv_b, layer, 1, tc, lo), _wspec(conv_b, layer, 1, tc, hi),
    ]
    args = [h, w_up, w_up, conv_w, conv_w, conv_b, conv_b]
    if ss is not None:
        in_specs.append(pl.BlockSpec((tm, LANES), lambda i, j: (i + t0, 0), pipeline_mode=pl.Buffered(1)))
        args.append(ss)
    if has_hist:
        in_specs += [pl.BlockSpec((None, keep, tc), lambda i, j: (i, 0, j)),
                     pl.BlockSpec((None, keep, tc), lambda i, j: (i, 0, j + nc))]
        args += [hist, hist]
    if aliased:
        in_specs.append(pl.BlockSpec(memory_space=pl.ANY))
        args.append(prev)
    sspec = pl.BlockSpec((None, keep, tc), lambda i, j: (i, 0, j))
    return pl.pallas_call(
        kern,
        grid=(nt, nc),
        in_specs=in_specs,
        out_specs=[pl.BlockSpec((tm, tc), lambda i, j: (i + t0, j)), sspec, sspec],
        out_shape=[jax.ShapeDtypeStruct((t, f), BF16), jax.ShapeDtypeStruct((nt, keep, f), F32),
                   jax.ShapeDtypeStruct((nt, keep, f), F32)],
        input_output_aliases={len(args) - 1: 0} if aliased else {},
        compiler_params=_cparams("parallel", "arbitrary"),
        name="ffn_up_conv_act",
    )(*args)


def _proj_conv_kernel(*refs, taps, stride, tm, chunk, has_hist, aliased, w_t):
    x_ref, w_ref, c_ref = refs[:3]
    h_ref = refs[3] if has_hist else None
    o_ref, s_ref = refs[3 + has_hist + aliased:]

    def emit(r, convs):
        o_ref[r, :] = _silu(convs[0])

    (s_ref[...],) = _proj_conv_chunks(x_ref, [w_ref[...].astype(BF16)], (c_ref,), (h_ref,), emit, taps=taps,
                                      stride=stride, tm=tm, chunk=chunk, w_t=w_t)


def _proj_conv(h, w, conv_w, *, layer, tm, stride, t0, nt, w_t=False, hist=None, prev=None):
    t, kdim = h.shape
    taps, c = conv_w.shape[-2:]
    keep = (taps - 1) * stride
    tc = _pick(c, 256)
    has_hist = hist is not None
    aliased = prev is not None
    kern = functools.partial(_proj_conv_kernel, taps=taps, stride=stride, tm=tm,
                             chunk=_row_chunk(tm, keep, ROW_CHUNK),
                             has_hist=has_hist, aliased=aliased, w_t=w_t)
    same = lambda j: j
    in_specs = [pl.BlockSpec((tm, kdim), lambda i, j: (i + t0, 0), pipeline_mode=pl.Buffered(1)),
                _wspec(w, layer, kdim, tc, same, w_t), _wspec(conv_w, layer, taps, tc, same)]
    args = [h, w, conv_w]
    if has_hist:
        in_specs.append(pl.BlockSpec((None, keep, tc), lambda i, j: (i, 0, j)))
        args.append(hist)
    if aliased:
        in_specs.append(pl.BlockSpec(memory_space=pl.ANY))
        args.append(prev)
    return pl.pallas_call(
        kern,
        grid=(nt, c // tc),
        in_specs=in_specs,
        out_specs=[pl.BlockSpec((tm, tc), lambda i, j: (i + t0, j)),
                   pl.BlockSpec((None, keep, tc), lambda i, j: (i, 0, j))],
        out_shape=[jax.ShapeDtypeStruct((t, c), F32), jax.ShapeDtypeStruct((nt, keep, c), F32)],
        input_output_aliases={len(args) - 1: 0} if aliased else {},
        compiler_params=_cparams("parallel", "arbitrary"),
        name="qkv_proj_conv",
    )(*args)


def _pool_kernel(*refs, stride, tm, aliased):
    h_ref, hist_ref, cnt_ref, w_ref, sc_ref, x_ref = refs[:6]
    o_ref = refs[-1]
    g = pl.program_id(1)
    off = POOL_HIST * stride
    cur = h_ref[...]
    run = jnp.concatenate([hist_ref[...], cur], axis=0)
    for k, width in enumerate(POOL_WINDOWS):
        back = (width // 2) * stride
        if back % SUBLANES:
            shifted = pltpu.roll(run, back, axis=0)
        else:
            shifted = jnp.concatenate([run[:back, :], run[:run.shape[0] - back, :]], axis=0)
        run = run + shifted if k == 0 else jnp.where(g >= k, run + shifted, run)
    pooled = run[off:, :] / cnt_ref[...] - cur
    y = _dot(pooled.astype(BF16), w_ref[...]) * sc_ref[...]
    o_ref[...] = x_ref[...] + y


def _pool_mixer(h, hist, cnt, pool_w, scale, x, *, tm, stride, t0, nt, prev=None):
    t, d = h.shape
    g = pool_w.shape[0]
    dg = d // g
    hr = POOL_HIST * stride
    aliased = prev is not None
    kern = functools.partial(_pool_kernel, stride=stride, tm=tm, aliased=aliased)
    in_specs = [
        pl.BlockSpec((tm, dg), lambda i, j: (i + t0, j)),
        pl.BlockSpec((None, hr, dg), lambda i, j: (i, 0, j)),
        pl.BlockSpec((None, tm, 1), lambda i, j: (j, i + t0, 0)),
        pl.BlockSpec((None, dg, dg), lambda i, j: (j, 0, 0)),
        pl.BlockSpec((1, dg), lambda i, j: (0, j)),
        pl.BlockSpec((tm, dg), lambda i, j: (i + t0, j)),
    ]
    args = [h, hist, cnt, pool_w, scale.reshape(1, d), x]
    if aliased:
        in_specs.append(pl.BlockSpec(memory_space=pl.ANY))
        args.append(prev)
    return pl.pallas_call(
        kern,
        grid=(nt, g),
        in_specs=in_specs,
        out_specs=pl.BlockSpec((tm, dg), lambda i, j: (i + t0, j)),
        out_shape=jax.ShapeDtypeStruct((t, d), F32),
        input_output_aliases={6: 0} if aliased else {},
        compiler_params=_cparams("parallel", "arbitrary"),
        name="pool_mixer",
    )(*args)


def _gdn_kernel(*refs, heads, dk, rows, seg, valid, carry, unroll, aliased):
    q_ref, k_ref, v_ref, pa_ref, pb_ref, z_ref, s0_ref, alog_ref, dtb_ref, on_ref, ex_ref = refs[:11]
    o_ref, sn_ref, gcb_s, bb_s, gct_s = refs[11 + aliased:16 + aliased]
    s_s = refs[16 + aliased] if carry else None
    dv = dk
    nseg = rows // seg
    sh = int(math.log2(seg))
    assert 1 << sh == seg and seg % BASE_BLOCK == 0

    if carry:
        @pl.when(pl.program_id(1) == 0)
        def _():
            s_s[...] = s0_ref[...]

    rid = lax.broadcasted_iota(jnp.int32, (rows, 1), 0)
    live = jnp.bitwise_and(rid, seg - 1) < valid
    rseg = lax.shift_right_logical(rid, sh)
    ri = lax.broadcasted_iota(jnp.int32, (rows, rows), 0)
    ci = lax.broadcasted_iota(jnp.int32, (rows, rows), 1)
    same = lax.shift_right_logical(ri, sh) == lax.shift_right_logical(ci, sh)
    causal = jnp.logical_and(same, ci <= ri)
    strict = jnp.logical_and(same, ci < ri)
    base = lax.shift_right_logical(ri, 3) == lax.shift_right_logical(ci, 3)
    eye = (ri == ci).astype(F32)
    levels = []
    s = BASE_BLOCK
    while s < seg:
        rb = lax.shift_right_logical(ri, int(math.log2(s)))
        cb = lax.shift_right_logical(ci, int(math.log2(s)))
        levels.append(jnp.logical_and(jnp.bitwise_and(rb, 1) == 1, cb == rb - 1))
        s *= 2

    a = pa_ref[...] + dtb_ref[...]
    softplus = jnp.maximum(a, 0.0) + jnp.log1p(jnp.exp(-jnp.abs(a)))
    glog = jnp.where(live, -jnp.exp(alog_ref[...]) * softplus, 0.0)
    beta = jnp.where(live, _sigmoid(pb_ref[...]), 0.0)
    gc = _dot(causal.astype(F32), glog, precision=HIGHEST)
    gcb_s[...] = _expand(gc, ex_ref[...], 3)
    bb_s[...] = _expand(beta, ex_ref[...], 2)
    gct_s[...] = gc.T

    def widen(x):
        if rows <= dk:
            return x[:, :rows]
        return jnp.concatenate([x] * (rows // dk), axis=1)

    def group(gi, carry_unused):
        hs = [gi * unroll + j for j in range(unroll)]
        los = [pl.multiple_of(h * dk, dk) for h in hs]
        each = range(unroll)
        qs = [q_ref[:, pl.ds(lo, dk)] for lo in los]
        ks = [k_ref[:, pl.ds(lo, dk)] for lo in los]
        vs = [v_ref[:, pl.ds(lo, dk)] for lo in los]
        zs = [z_ref[:, pl.ds(lo, dv)] for lo in los]
        gcbs = [gcb_s[:, pl.ds(lo, dk)] for lo in los]
        bbs = [bb_s[:, pl.ds(lo, dk)] for lo in los]
        grow = [gct_s[pl.ds(h, 1), :] for h in hs]
        if carry:
            sts = [s_s[h] for h in hs]
        qs = [q * lax.rsqrt(jnp.sum(q * q, axis=-1, keepdims=True) + 1e-6) * (dk ** -0.5) for q in qs]
        ks = [k * lax.rsqrt(jnp.sum(k * k, axis=-1, keepdims=True) + 1e-6) for k in ks]
        egcs = [jnp.exp(g) for g in gcbs]
        decays = [jnp.where(causal, jnp.exp(jnp.where(causal, widen(gcbs[j]) - grow[j], 0.0)), 0.0)
                  for j in each]
        kbf = [k.astype(BF16) for k in ks]
        qkks = [_dot_nt(jnp.concatenate([qs[j].astype(BF16), kbf[j]], axis=0), kbf[j]) for j in each]
        qks = [qkks[j][:rows] * decays[j] for j in each]
        amats = [jnp.where(strict, widen(bbs[j]) * decays[j] * qkks[j][rows:], 0.0) for j in each]
        pws = [-jnp.where(base, a, 0.0) for a in amats]
        invs = [eye + p for p in pws]
        for _ in range(2):
            pws = [_mm(p, p) for p in pws]
            invs = [invs[j] + _mm(pws[j], invs[j]) for j in each]
        for lvl in levels:
            lows = [_mm(jnp.where(lvl, amats[j], 0.0), invs[j]) for j in each]
            invs = [invs[j] - _mm(invs[j], lows[j]) for j in each]
        sols = [_mm(invs[j], jnp.concatenate([bbs[j] * vs[j], bbs[j] * egcs[j] * ks[j]], axis=1))
                for j in each]
        wvs = [sl[:, :dv] for sl in sols]
        wks = [sl[:, dv:] for sl in sols]
        qds = [qs[j] * egcs[j] for j in each]
        if carry:
            us = [[] for _ in each]
            os_ = [[] for _ in each]
            for sg in range(nseg):
                r = slice(sg * seg, (sg + 1) * seg)
                last = slice((sg + 1) * seg - 1, (sg + 1) * seg)
                both = [_dot(jnp.concatenate([wks[j][r], qds[j][r]], axis=0).astype(BF16), sts[j].astype(BF16))
                        for j in each]
                u = [wvs[j][r] - both[j][:seg] for j in each]
                kend = [ks[j][r] * jnp.exp(gcbs[j][last] - gcbs[j][r]) for j in each]
                sts = [jnp.exp(gcbs[j][last]) * sts[j] + _dot_tn(kend[j].astype(BF16), u[j].astype(BF16))
                       for j in each]
                for j in each:
                    us[j].append(u[j])
                    os_[j].append(both[j][seg:])
            u_all = [x[0] if nseg == 1 else jnp.concatenate(x, axis=0) for x in us]
            o = [x[0] if nseg == 1 else jnp.concatenate(x, axis=0) for x in os_]
        else:
            lhs = [jnp.concatenate([wks[j], qds[j]], axis=0).astype(BF16) for j in each]
            u_all = [jnp.zeros((rows, dv), F32) for _ in each]
            o = [jnp.zeros((rows, dv), F32) for _ in each]
            news = []
            for sg in range(nseg):
                last = slice((sg + 1) * seg - 1, (sg + 1) * seg)
                mine = rseg == sg
                st0 = [s0_ref[sg, h] for h in hs]
                both = [_dot(lhs[j], st0[j].astype(BF16)) for j in each]
                u_m = [jnp.where(mine, wvs[j] - both[j][:rows], 0.0) for j in each]
                u_all = [u_all[j] + u_m[j] for j in each]
                o = [o[j] + jnp.where(mine, both[j][rows:], 0.0) for j in each]
                kend = [jnp.where(mine, ks[j] * jnp.exp(jnp.where(mine, gcbs[j][last] - gcbs[j], 0.0)), 0.0)
                        for j in each]
                news.append([jnp.exp(gcbs[j][last]) * st0[j] + _dot_tn(kend[j].astype(BF16), u_m[j].astype(BF16))
                             for j in each])
        o = [o[j] + _mm(qks[j], u_all[j]) for j in each]
        o = [o[j] * lax.rsqrt(jnp.mean(o[j] * o[j], axis=-1, keepdims=True) + EPS) * on_ref[...] * _silu(zs[j])
             for j in each]
        for j in each:
            o_ref[:, pl.ds(los[j], dv)] = o[j].astype(o_ref.dtype)
            if carry:
                s_s[hs[j]] = sts[j]
            else:
                for sg in range(nseg):
                    sn_ref[sg, hs[j]] = news[sg][j]
        return carry_unused

    assert heads % unroll == 0
    lax.fori_loop(0, heads // unroll, group, 0)

    if carry:
        @pl.when(pl.program_id(1) == pl.num_programs(1) - 1)
        def _():
            sn_ref[...] = s_s[...]


def _gdn(qkv, proj, s0, a_log, dt_bias, out_norm, *, cols, nseq, seqlen, rows, seg, valid, carry,
         out_shape=None, out_block=0, prev=None):
    _, heads, dk, dv = s0.shape
    assert dk == LANES and dv == LANES and heads <= LANES and rows % seg == 0
    hk = heads * dk
    expand = np.zeros((LANES, hk), np.float32)
    for h in range(heads):
        expand[h, h * dk:(h + 1) * dk] = 1.0
    lane_pad = lambda t: jnp.pad(t.reshape(1, heads).astype(F32), ((0, 0), (0, LANES - heads)))
    col_z, col_a, col_b = cols
    assert col_z % hk == 0 and col_a % LANES == 0 and col_b % LANES == 0
    if carry:
        nblk = seqlen // rows
        grid = (nseq, nblk)
        rmap = lambda i, c: i * nblk + c
        smap = lambda i, c: (i, 0, 0, 0)
        sblock = (None, heads, dk, dv)
        sem = ("parallel", "arbitrary")
    else:
        spb = rows // seg
        grid = (nseq // spb, 1)
        rmap = lambda i, c: i
        smap = lambda i, c: (i, 0, 0, 0)
        sblock = (spb, heads, dk, dv)
        sem = ("parallel", "arbitrary")
    rspec = lambda w, cb: pl.BlockSpec((rows, w), lambda i, c: (rmap(i, c), cb))
    full = lambda shape: pl.BlockSpec(shape, lambda i, c: (0,) * len(shape))
    aliased = prev is not None
    kern = functools.partial(_gdn_kernel, heads=heads, dk=dk, rows=rows, seg=seg, valid=valid, carry=carry,
                             unroll=min(heads, 8), aliased=aliased)
    scratch = [pltpu.VMEM((rows, hk), F32), pltpu.VMEM((rows, hk), F32), pltpu.VMEM((LANES, rows), F32)]
    if carry:
        scratch.append(pltpu.VMEM((heads, dk, dv), F32))
    if out_shape is None:
        out_shape = (nseq * seqlen, heads * dv)
    in_specs = [
        rspec(hk, 0), rspec(hk, 1), rspec(hk, 2),
        rspec(LANES, col_a // LANES), rspec(LANES, col_b // LANES), rspec(hk, col_z // hk),
        pl.BlockSpec(sblock, smap),
        full((1, LANES)), full((1, LANES)), full((1, dv)), full((LANES, hk)),
    ]
    args = [qkv, qkv, qkv, proj, proj, proj, s0, lane_pad(a_log), lane_pad(dt_bias),
            out_norm.reshape(1, dv).astype(F32), jnp.asarray(expand, dtype=BF16)]
    if aliased:
        in_specs.append(pl.BlockSpec(memory_space=pl.ANY))
        args.append(prev)
    return pl.pallas_call(
        kern,
        grid=grid,
        in_specs=in_specs,
        out_specs=[rspec(hk, out_block), pl.BlockSpec(sblock, smap)],
        out_shape=[jax.ShapeDtypeStruct(out_shape, BF16), jax.ShapeDtypeStruct(s0.shape, F32)],
        scratch_shapes=scratch,
        input_output_aliases={len(args) - 1: 0} if aliased else {},
        compiler_params=_cparams(*sem),
        name="gated_deltanet",
    )(*args)


def _ret_kernel(*refs, hps, dk, rows, seg, carry, aliased):
    q_ref, k_ref, v_ref, g_ref, cos_ref, sin_ref, s0_ref, dec_ref, qs_ref, ks_ref, ge_ref = refs[:11]
    o_ref, sn_ref = refs[11 + aliased:13 + aliased]
    s_s = refs[13 + aliased] if carry else None
    dv = dk
    half = dk // 2
    nseg = rows // seg
    sh = int(math.log2(seg))
    hstep = pl.program_id(1) if not carry else 0

    if carry:
        @pl.when(pl.program_id(1) == 0)
        def _():
            s_s[...] = s0_ref[...]

    cos = cos_ref[...]
    sin = sin_ref[...]
    rseg = lax.shift_right_logical(lax.broadcasted_iota(jnp.int32, (rows, 1), 0), sh)

    def rot(x):
        x1, x2 = x[:, :half], x[:, half:]
        return jnp.concatenate([x1 * cos - x2 * sin, x1 * sin + x2 * cos], axis=1)

    gsz = min(hps, RET_GROUP)
    assert hps % gsz == 0

    def group(gi, carry_unused):
        hhs = [gi * gsz + u for u in range(gsz)]
        hs = [hstep * hps + hh for hh in hhs]
        los = [pl.multiple_of(h * dk, dk) for h in hs]
        each = range(gsz)
        qs = [rot(q_ref[:, pl.ds(lo, dk)]) for lo in los]
        ks = [rot(k_ref[:, pl.ds(lo, dk)]) * (dk ** -0.5) for lo in los]
        vbf = [v_ref[:, pl.ds(lo, dv)].astype(BF16) for lo in los]
        gates = [g_ref[:, pl.ds(lo, dv)] for lo in los]
        ges = [ge_ref[h] for h in hs]
        qks = [_dot_nt(qs[u].astype(BF16), ks[u].astype(BF16)) * dec_ref[hs[u]] for u in each]
        qds = [(qs[u] * qs_ref[:, pl.ds(los[u], dk)]).astype(BF16) for u in each]
        kes = [ks[u] * ks_ref[:, pl.ds(los[u], dk)] for u in each]
        if carry:
            sts = [s_s[h] for h in hs]
            parts = [[] for _ in each]
            for sg in range(nseg):
                r = slice(sg * seg, (sg + 1) * seg)
                for u in each:
                    parts[u].append(_dot(qds[u][r], sts[u].astype(BF16)))
                sts = [ges[u] * sts[u] + _dot_tn(kes[u][r].astype(BF16), vbf[u][r]) for u in each]
            o = [x[0] if nseg == 1 else jnp.concatenate(x, axis=0) for x in parts]
        else:
            o = [jnp.zeros((rows, dv), F32) for _ in each]
            news = []
            for sg in range(nseg):
                mine = rseg == sg
                st0 = [s0_ref[sg, hh] for hh in hhs]
                o = [o[u] + jnp.where(mine, _dot(qds[u], st0[u].astype(BF16)), 0.0) for u in each]
                news.append([ges[u] * st0[u] + _dot_tn(jnp.where(mine, kes[u], 0.0).astype(BF16), vbf[u])
                             for u in each])
        o = [o[u] + _dot(qks[u].astype(BF16), vbf[u]) for u in each]
        mus = [jnp.mean(x, axis=-1, keepdims=True) for x in o]
        vrs = [jnp.mean(jnp.square(o[u] - mus[u]), axis=-1, keepdims=True) for u in each]
        o = [(o[u] - mus[u]) * lax.rsqrt(vrs[u] + EPS) * _silu(gates[u]) for u in each]
        for u in each:
            o_ref[:, pl.ds(los[u], dv)] = o[u].astype(o_ref.dtype)
            if carry:
                s_s[hs[u]] = sts[u]
            else:
                for sg in range(nseg):
                    sn_ref[sg, hhs[u]] = news[sg][u]
        return carry_unused

    lax.fori_loop(0, hps // gsz, group, 0)

    if carry:
        @pl.when(pl.program_id(1) == pl.num_programs(1) - 1)
        def _():
            sn_ref[...] = s_s[...]


def _retention(proj, s0, *, col_q, nseq, seqlen, rows, seg, valid, pos0, carry,
               out_shape=None, out_block=0, prev=None):
    _, heads, dk, dv = s0.shape
    assert dk == dv and rows % seg == 0
    hw = heads * dk
    half = dk // 2
    assert col_q % hw == 0
    cq = col_q // hw
    npos = seqlen if carry else rows
    offs = np.arange(npos) if carry else np.arange(rows) % seg
    inv_freq = ROPE_BASE ** (-np.arange(half, dtype=np.float64) / half)
    ang = (pos0 + offs.astype(np.float64))[:, None] * inv_freq[None, :]
    cos, sin = np.cos(ang).astype(np.float32), np.sin(ang).astype(np.float32)
    log_gamma = np.log1p(-np.exp2(-5.0 - np.arange(heads, dtype=np.float64)))
    pin = np.arange(rows) % seg
    steps = np.minimum(pin + 1, valid).astype(np.float64)
    gc = log_gamma[:, None] * steps[None, :]
    gl = log_gamma * min(seg, valid)
    sameseg = (np.arange(rows)[:, None] // seg) == (np.arange(rows)[None, :] // seg)
    causal = np.logical_and(sameseg, np.tril(np.ones((rows, rows), bool)))
    diff = gc[:, :, None] - gc[:, None, :]
    dec = np.where(causal[None], np.exp(np.where(causal[None], diff, 0.0)), 0.0).astype(np.float32)
    qs = np.repeat(np.exp(gc).T, dk, axis=1).astype(np.float32)
    ks = np.repeat(np.exp(gl[:, None] - gc).T, dk, axis=1).astype(np.float32)
    ge = np.broadcast_to(np.exp(gl)[:, None, None], (heads, 1, dv)).astype(np.float32)
    if carry:
        nblk = seqlen // rows
        hps = heads
        grid = (nseq, nblk)
        rmap = lambda i, c: i * nblk + c
        tmap = lambda i, c: (c, 0)
        smap = lambda i, c: (i, 0, 0, 0)
        sblock = (None, heads, dk, dv)
    else:
        spb = rows // seg
        hps = max(1, heads // 2)
        grid = (nseq // spb, heads // hps)
        rmap = lambda i, c: i
        tmap = lambda i, c: (0, 0)
        smap = lambda i, c: (i, c, 0, 0)
        sblock = (spb, hps, dk, dv)
    rspec = lambda cb: pl.BlockSpec((rows, hw), lambda i, c: (rmap(i, c), cb))
    full = lambda shape: pl.BlockSpec(shape, lambda i, c: (0,) * len(shape))
    aliased = prev is not None
    kern = functools.partial(_ret_kernel, hps=hps, dk=dk, rows=rows, seg=seg, carry=carry, aliased=aliased)
    if out_shape is None:
        out_shape = (nseq * seqlen, heads * dv)
    in_specs = [
        rspec(cq), rspec(cq + 1), rspec(cq + 2), rspec(cq + 3),
        pl.BlockSpec((rows, half), tmap), pl.BlockSpec((rows, half), tmap),
        pl.BlockSpec(sblock, smap),
        full((heads, rows, rows)), full((rows, hw)), full((rows, hw)), full((heads, 1, dv)),
    ]
    args = [proj, proj, proj, proj, jnp.asarray(cos), jnp.asarray(sin), s0, jnp.asarray(dec), jnp.asarray(qs),
            jnp.asarray(ks), jnp.asarray(ge)]
    if aliased:
        in_specs.append(pl.BlockSpec(memory_space=pl.ANY))
        args.append(prev)
    return pl.pallas_call(
        kern,
        grid=grid,
        in_specs=in_specs,
        out_specs=[rspec(out_block), pl.BlockSpec(sblock, smap)],
        out_shape=[jax.ShapeDtypeStruct(out_shape, BF16), jax.ShapeDtypeStruct(s0.shape, F32)],
        scratch_shapes=[pltpu.VMEM((heads, dk, dv), F32)] if carry else [],
        input_output_aliases={len(args) - 1: 0} if aliased else {},
        compiler_params=_cparams("parallel", "arbitrary"),
        name="retention",
    )(*args)


def _pack_w_rest(wt, gdn_c, heads_a, gdn_v):
    off_a = gdn_c
    off_b = off_a + heads_a
    off_z = off_b + heads_a
    off_r = off_z + gdn_v
    zpad = jnp.zeros((LANES - heads_a, wt.shape[1]), wt.dtype)
    return jnp.concatenate([wt[off_z:off_r], wt[off_r:], wt[off_a:off_b], zpad, wt[off_b:off_z], zpad], axis=0)


def kernel(x_prompt, x_sample, state_gdn_conv, state_gdn, state_ret, state_pool, state_ffn_conv, p_prompt, p_sample, norm_mix, norm_ffn, norm_ple, norm_final, w_in, gdn_conv_w, gdn_a_log, gdn_dt_bias, gdn_out_norm, w_out, pool_w, pool_scale, ffn_w_up, ffn_conv_w, ffn_conv_b, ffn_w_down, ple_w_gate, ple_w_proj):
    bp, lp, d = x_prompt.shape
    bs, ls, _ = x_sample.shape
    depth = p_prompt.shape[0]
    tp, ts = bp * lp, bs * ls
    t = tp + ts
    tm = ts
    assert lp % tm == 0 and tm % SUBLANES == 0
    ntp = tp // tm
    _, _, heads_a, dk_a, dv_a = state_gdn.shape
    _, _, heads_r, dk_r, dv_r = state_ret.shape
    gdn_c = state_gdn_conv.shape[-1]
    gdn_v = heads_a * dv_a
    ret_w = heads_r * dk_r
    gtaps = gdn_conv_w.shape[1]
    ftaps = ffn_conv_w.shape[1]
    pool_past = state_pool.shape[2]
    f2 = ffn_w_up.shape[-1]
    assert min(lp, ls) >= max(gtaps, ftaps) - 1 and pool_past == POOL_HIST - 1 and lp % SUBLANES == 0
    assert lp % CHUNK == 0 and ls <= BASE_BLOCK and gdn_c == 3 * gdn_v and heads_a * dk_a == gdn_v
    assert gdn_v % ret_w == 0 and heads_r * dv_r == ret_w
    col_z, col_r = 0, gdn_v
    col_a = col_r + 4 * ret_w
    col_b = col_a + LANES

    rows_p = lambda a: a.reshape((tp,) + a.shape[2:])
    rows_s = lambda a: jnp.swapaxes(a, 0, 1).reshape((ts,) + a.shape[2:])
    join = lambda a, b: jnp.concatenate([rows_p(a), rows_s(b)], axis=0)
    seq_p = lambda a: a[:tp].reshape((bp, lp) + a.shape[1:])
    seq_s = lambda a: jnp.swapaxes(a[tp:].reshape((ls, bs) + a.shape[1:]), 0, 1)

    tps = lp // tm
    tiles = lambda a: a.reshape((t // tm, tm) + a.shape[1:])

    def hist_p(a, nrows):
        tl = tiles(a)[:ntp, tm - nrows:]
        prev = jnp.concatenate([jnp.zeros_like(tl[:1]), tl[:-1]], axis=0)
        first = (jnp.arange(ntp) % tps == 0)[:, None, None]
        return jnp.where(first, 0.0, prev)

    def tail_p(a, nrows):
        return tiles(a)[tps - 1:ntp:tps, tm - nrows:]

    def hist_s(state, nrows):
        return jnp.swapaxes(state[:, -nrows:], 0, 1).reshape(1, nrows * bs, state.shape[-1])

    def pad_seq(a):
        a = seq_s(a)
        a = jnp.pad(a, ((0, 0), (0, BASE_BLOCK - ls)) + ((0, 0),) * (a.ndim - 2))
        return a.reshape((bs * BASE_BLOCK,) + a.shape[2:])

    unpad_seq = lambda a: rows_s(a.reshape((bs, BASE_BLOCK) + a.shape[1:])[:, :ls])

    xr = join(x_prompt, x_sample)
    pj = [join(p_prompt[i], p_sample[i]).astype(BF16) for i in range(depth)]
    pool_w_bf = pool_w.astype(BF16)
    w_down_bf = ffn_w_down.astype(BF16)
    w_proj_bf = ple_w_proj.astype(BF16)
    w_in_t = jnp.swapaxes(w_in, 1, 2)
    rows_prompt = _pick(lp, 256, CHUNK)
    spb = max(1, CHUNK // BASE_BLOCK)
    assert bs % spb == 0

    new_gdn_conv_p, new_gdn_p, new_ret_p, new_pool_p, new_ffn_p = [], [], [], [], []
    new_gdn_conv_s, new_gdn_s, new_ret_s, new_pool_s, new_ffn_s = [], [], [], [], []
    for i in range(depth):
        j = i // 2
        if i % 2 == 0:
            (h,) = _rmsnorm(xr, norm_mix[i], [BF16])
            qkv, cs_p = _proj_conv(h, w_in_t, gdn_conv_w, layer=j, tm=lp, stride=1, t0=0, nt=bp, w_t=True)
            qkv, cs_s = _proj_conv(h, w_in_t, gdn_conv_w, layer=j, tm=tm, stride=bs, t0=ntp, nt=1, w_t=True,
                                   hist=hist_s(state_gdn_conv[j], gtaps - 1), prev=qkv)
            proj = _matmul(h, _pack_w_rest(w_in_t[j], gdn_c, heads_a, gdn_v), w_t=True, name="in_proj")
            cols = (col_z, col_a, col_b)
            mshape = (t, gdn_v + heads_r * dv_r)
            mixed, s_a_p = _gdn(qkv, proj, jnp.zeros((bp,) + state_gdn.shape[2:], F32), gdn_a_log[j],
                                gdn_dt_bias[j], gdn_out_norm[j], cols=cols, nseq=bp, seqlen=lp,
                                rows=rows_prompt, seg=CHUNK, valid=CHUNK, carry=True, out_shape=mshape)
            mixed, s_b_p = _retention(proj, jnp.zeros((bp,) + state_ret.shape[2:], F32), col_q=col_r,
                                      nseq=bp, seqlen=lp, rows=rows_prompt, seg=CHUNK, valid=CHUNK, pos0=0,
                                      carry=True, out_shape=mshape, out_block=gdn_v // ret_w, prev=mixed)
            proj_s = pad_seq(proj)
            o_gdn_s, s_a_s = _gdn(pad_seq(qkv), proj_s, state_gdn[j], gdn_a_log[j], gdn_dt_bias[j],
                                  gdn_out_norm[j], cols=cols, nseq=bs, seqlen=BASE_BLOCK,
                                  rows=spb * BASE_BLOCK, seg=BASE_BLOCK, valid=ls, carry=False)
            o_ret_s, s_b_s = _retention(proj_s, state_ret[j], col_q=col_r, nseq=bs, seqlen=BASE_BLOCK,
                                        rows=spb * BASE_BLOCK, seg=BASE_BLOCK, valid=ls, pos0=PAST_LEN,
                                        carry=False)
            mixed_s = jnp.concatenate([unpad_seq(o_gdn_s), unpad_seq(o_ret_s)], axis=1)
            mixed = lax.dynamic_update_slice(mixed, mixed_s, (tp, 0))
            xr, h, ss = _matmul(mixed, w_out, layer=j, res=xr, gain=norm_ffn[i], name="out_proj")
            new_gdn_conv_p.append(cs_p)
            new_gdn_conv_s.append(jnp.swapaxes(cs_s.reshape(gtaps - 1, bs, gdn_c), 0, 1))
            new_gdn_p.append(s_a_p)
            new_gdn_s.append(s_a_s)
            new_ret_p.append(s_b_p)
            new_ret_s.append(s_b_s)
        else:
            h32, = _rmsnorm(xr, norm_mix[i], [F32])
            h_s = seq_s(h32)
            pos = jnp.concatenate([jnp.tile(jnp.arange(lp, dtype=jnp.int32), bp),
                                   jnp.repeat(PAST_LEN + jnp.arange(ls, dtype=jnp.int32), bs)])
            cnt = jnp.stack([jnp.minimum(w, pos + 1).astype(F32) for w in POOL_WINDOWS])[:, :, None]
            zero_row = jnp.zeros((bs, 1, d), F32)
            kw = dict(tm=tm)
            xn = _pool_mixer(h32, hist_p(h32, POOL_HIST), cnt, pool_w_bf[j], pool_scale[j], xr, stride=1,
                             t0=0, nt=ntp, **kw)
            xr = _pool_mixer(h32, hist_s(jnp.concatenate([zero_row, state_pool[j]], axis=1), POOL_HIST), cnt,
                             pool_w_bf[j], pool_scale[j], xr, stride=bs, t0=ntp, nt=1, prev=xn, **kw)
            new_pool_p.append(tail_p(h32, pool_past))
            new_pool_s.append(jnp.concatenate([state_pool[j], h_s], axis=1)[:, -pool_past:])
            ((h,), ss) = _rmsnorm(xr, norm_ffn[i], [BF16]), None
        act, sg_p, sv_p = _ffn_up_act(h, ffn_w_up, ffn_conv_w, ffn_conv_b, layer=i, tm=lp, stride=1, t0=0, nt=bp,
                                      ss=ss)
        act, sg_s, sv_s = _ffn_up_act(h, ffn_w_up, ffn_conv_w, ffn_conv_b, layer=i, tm=tm, stride=bs, t0=ntp,
                                      nt=1, ss=ss, hist=hist_s(state_ffn_conv[i], ftaps - 1), prev=act)
        xr, h, ss = _matmul(act, w_down_bf, layer=i, res=xr, gain=norm_ple[i], name="ffn_down")
        new_ffn_p.append(jnp.concatenate([sg_p, sv_p], axis=-1))
        up_s = jnp.concatenate([sg_s, sv_s], axis=-1).reshape(ftaps - 1, bs, f2)
        new_ffn_s.append(jnp.swapaxes(up_s, 0, 1))
        xr = _matmul(h, ple_w_gate, layer=i, ss=ss, res=xr, p=pj[i], wp=w_proj_bf, name="ple")
    (out_p,) = _rmsnorm(xr, norm_final, [F32], 0, tp)
    (out_s,) = _rmsnorm(xr, norm_final, [F32], tp, ts)
    st = lambda lst: jnp.stack(lst)
    return (out_p.reshape(bp, lp, d), jnp.swapaxes(out_s.reshape(ls, bs, d), 0, 1),
            st(new_gdn_conv_p), st(new_gdn_p), st(new_ret_p), st(new_pool_p), st(new_ffn_p),
            st(new_gdn_conv_s), st(new_gdn_s), st(new_ret_s), st(new_pool_s), st(new_ffn_s))
```

```python
import functools
import math

import numpy as np
import jax
import jax.numpy as jnp
from jax import lax
from jax.experimental import pallas as pl
from jax.experimental.pallas import tpu as pltpu

F32 = jnp.float32
BF16 = jnp.bfloat16
HIGHEST = lax.Precision.HIGHEST

EPS = 1e-6
CHUNK = 64
PAST_LEN = 16384
ROPE_BASE = 10000.0
POOL_WINDOWS = (2, 4, 8, 16)
POOL_HIST = 16
LANES = 128
SUBLANES = 8
VMEM_LIMIT = 56 * 1024 * 1024
X_TILE_BYTES = 24 * 1024 * 1024
BASE_BLOCK = 8
RET_GROUP = 8
ROW_CHUNK = 256


def _pick(n, pref, align=LANES):
    if n <= pref:
        return n
    t = (pref // align) * align
    while t >= align:
        if n % t == 0:
            return t
        t -= align
    return n


def _cparams(*sem):
    return pltpu.CompilerParams(dimension_semantics=sem, vmem_limit_bytes=VMEM_LIMIT)


def _silu(x):
    return x * (1.0 / (1.0 + jnp.exp(-x)))


def _sigmoid(x):
    return 1.0 / (1.0 + jnp.exp(-x))


def _dot(a, b, **kw):
    return jnp.dot(a, b, preferred_element_type=F32, **kw)


def _dot_nt(a, b, **kw):
    return lax.dot_general(a, b, (((1,), (1,)), ((), ())), preferred_element_type=F32, **kw)


def _dot_tn(a, b, **kw):
    return lax.dot_general(a, b, (((0,), (0,)), ((), ())), preferred_element_type=F32, **kw)


def _mm(a, b):
    return _dot(a.astype(BF16), b.astype(BF16))


def _expand(x, e_bf, passes):
    acc = None
    for _ in range(passes):
        piece = x.astype(BF16)
        term = _dot(piece, e_bf)
        acc = term if acc is None else acc + term
        x = x - piece.astype(F32)
    return acc


def _rmsnorm_kernel(x_ref, g_ref, *o_refs):
    x = x_ref[...]
    y = x * lax.rsqrt(jnp.mean(x * x, axis=-1, keepdims=True) + EPS) * g_ref[...]
    for o_ref in o_refs:
        o_ref[...] = y.astype(o_ref.dtype)


def _rmsnorm(x, gain, dtypes, row0=0, nrows=None):
    d = x.shape[1]
    m = x.shape[0] if nrows is None else nrows
    tm = _pick(math.gcd(m, row0) if row0 else m, 256, SUBLANES)
    t0 = row0 // tm
    return pl.pallas_call(
        _rmsnorm_kernel,
        grid=(m // tm,),
        in_specs=[pl.BlockSpec((tm, d), lambda i: (i + t0, 0)), pl.BlockSpec((1, d), lambda i: (0, 0))],
        out_specs=[pl.BlockSpec((tm, d), lambda i: (i, 0)) for _ in dtypes],
        out_shape=[jax.ShapeDtypeStruct((m, d), dt) for dt in dtypes],
        compiler_params=_cparams("parallel"),
        name="rmsnorm",
    )(x, gain.reshape(1, d))


def _rstd(ss_ref, width):
    return lax.rsqrt(jnp.sum(ss_ref[...], axis=-1, keepdims=True) * (1.0 / width) + EPS)


def _mm_kernel(*refs, epilogue, w_t, scaled, emit_norm):
    it = iter(refs)
    x_ref, w_ref = next(it), next(it)
    ss_in = next(it) if scaled else None
    res_ref = next(it) if epilogue in ("res", "ple") else None
    p_ref, wp_ref = (next(it), next(it)) if epilogue == "ple" else (None, None)
    gn_ref = next(it) if emit_norm else None
    o_ref = next(it)
    acc = (_dot_nt if w_t else _dot)(x_ref[...], w_ref[...].astype(BF16))
    if scaled:
        acc = acc * _rstd(ss_in, x_ref.shape[1])
    if epilogue == "res":
        acc = res_ref[...] + acc
    elif epilogue == "ple":
        acc = res_ref[...] + _sigmoid(acc) * _dot(p_ref[...], wp_ref[...])
    o_ref[...] = acc.astype(o_ref.dtype)
    if emit_norm:
        xb_ref, ss_ref = next(it), next(it)
        xb_ref[...] = (acc * gn_ref[...]).astype(BF16)

        @pl.when(pl.program_id(1) == 0)
        def _():
            ss_ref[...] = jnp.zeros(ss_ref.shape, F32)

        sq = acc * acc
        part = sq[:, :LANES]
        for k in range(1, sq.shape[1] // LANES):
            part = part + sq[:, k * LANES:(k + 1) * LANES]
        ss_ref[...] += part


def _wspec(w, layer, rows, tc, cmap, w_t=False):
    block, index = ((tc, rows), lambda j: (cmap(j), 0)) if w_t else ((rows, tc), lambda j: (0, cmap(j)))
    if w.ndim == 2:
        return pl.BlockSpec(block, lambda i, j: index(j))
    return pl.BlockSpec((None,) + block, lambda i, j: (layer,) + index(j))


def _matmul(x, w, *, layer=0, w_t=False, ss=None, res=None, p=None, wp=None, gain=None, tn=256,
            out_dtype=F32, name="matmul"):
    m, kdim = x.shape
    n = w.shape[-2] if w_t else w.shape[-1]
    tm = _pick(m, X_TILE_BYTES // (2 * kdim), 2 * SUBLANES)
    tn = _pick(n, tn)
    epilogue = "none" if res is None else ("res" if p is None else "ple")
    rows = lambda width: pl.BlockSpec((tm, width), lambda i, j: (i, 0), pipeline_mode=pl.Buffered(1))
    in_specs = [rows(kdim), _wspec(w, layer, kdim, tn, lambda j: j, w_t)]
    args = [x, w]
    if ss is not None:
        in_specs.append(rows(LANES))
        args.append(ss)
    if res is not None:
        in_specs.append(pl.BlockSpec((tm, tn), lambda i, j: (i, j)))
        args.append(res)
    if p is not None:
        in_specs += [rows(p.shape[1]), _wspec(wp, layer, p.shape[1], tn, lambda j: j)]
        args += [p, wp]
    tile = pl.BlockSpec((tm, tn), lambda i, j: (i, j))
    out_specs, out_shape = tile, jax.ShapeDtypeStruct((m, n), out_dtype)
    if gain is not None:
        in_specs.append(pl.BlockSpec((1, tn), lambda i, j: (0, j)))
        args.append(gain.reshape(1, n))
        out_specs = [tile, tile, pl.BlockSpec((tm, LANES), lambda i, j: (i, 0))]
        out_shape = [out_shape, jax.ShapeDtypeStruct((m, n), BF16), jax.ShapeDtypeStruct((m, LANES), F32)]
    return pl.pallas_call(
        functools.partial(_mm_kernel, epilogue=epilogue, w_t=w_t, scaled=ss is not None,
                          emit_norm=gain is not None),
        grid=(m // tm, n // tn),
        in_specs=in_specs,
        out_specs=out_specs,
        out_shape=out_shape,
        compiler_params=_cparams("parallel", "arbitrary"),
        name=name,
    )(*args)


def _row_chunk(tm, keep, pref):
    chunk = _pick(tm, pref, 2 * SUBLANES)
    return chunk if chunk >= keep else tm


def _causal_conv(cur, w_ref, hist, *, taps, stride):
    rows, hrows = cur.shape[0], hist.shape[0]
    ext = jnp.concatenate([hist, cur], axis=0)
    acc = w_ref[taps - 1:taps, :] * cur
    for j in range(taps - 1):
        back = (taps - 1 - j) * stride
        if back % SUBLANES == 0:
            piece = ext[hrows - back:hrows - back + rows, :]
        else:
            piece = pltpu.roll(ext, back, axis=0)[hrows:, :]
        acc = acc + w_ref[j:j + 1, :] * piece
    return acc


def _proj_conv_chunks(x_ref, ws, c_refs, h_refs, emit, *, taps, stride, tm, chunk, w_t=False, scale=None):
    keep = (taps - 1) * stride
    hrows = -(-keep // SUBLANES) * SUBLANES
    mm = _dot_nt if w_t else _dot
    tails = [jnp.zeros((hrows, w.shape[0 if w_t else 1]), F32) if h is None else h[...]
             for h, w in zip(h_refs, ws)]
    assert all(tl.shape[0] == hrows for tl in tails) and chunk >= hrows
    for c in range(tm // chunk):
        r = slice(c * chunk, (c + 1) * chunk)
        ups = [mm(x_ref[r, :], w) for w in ws]
        if scale is not None:
            ups = [u * scale[r, :] for u in ups]
        emit(r, [_causal_conv(u, cr, tl, taps=taps, stride=stride) for u, cr, tl in zip(ups, c_refs, tails)])
        tails = [u[chunk - hrows:, :] for u in ups]
    return [tl[hrows - keep:, :] for tl in tails]


def _ffn_up_kernel(*refs, taps, stride, tm, chunk, scaled, has_hist, aliased):
    x_ref, wg_ref, wv_ref, cg_ref, cv_ref, bg_ref, bv_ref = refs[:7]
    scale = _rstd(refs[7], x_ref.shape[1]) if scaled else None
    h_refs = (refs[7 + scaled], refs[8 + scaled]) if has_hist else (None, None)
    o_ref, sg_ref, sv_ref = refs[7 + scaled + 2 * has_hist + aliased:]

    tc = wg_ref.shape[1]
    side = lambda a, b: jnp.concatenate([a[...], b[...]], axis=1)
    bias = side(bg_ref, bv_ref)

    def emit(r, convs):
        y = convs[0] + bias
        o_ref[r, :] = (_silu(y[:, :tc]) * y[:, tc:]).astype(o_ref.dtype)

    w = side(wg_ref, wv_ref).astype(BF16)
    hist = side(*h_refs) if has_hist else None
    (tail,) = _proj_conv_chunks(x_ref, [w], (side(cg_ref, cv_ref),), (hist,), emit, taps=taps,
                                stride=stride, tm=tm, chunk=chunk, scale=scale)
    sg_ref[...] = tail[:, :tc]
    sv_ref[...] = tail[:, tc:]


def _ffn_up_act(h, w_up, conv_w, conv_b, *, layer, tm, stride, t0, nt, ss=None, hist=None, prev=None):
    t, kdim = h.shape
    f2 = w_up.shape[-1]
    f = f2 // 2
    taps = conv_w.shape[1]
    keep = (taps - 1) * stride
    tc = _pick(f, 256)
    nc = f // tc
    has_hist = hist is not None
    aliased = prev is not None
    kern = functools.partial(_ffn_up_kernel, taps=taps, stride=stride, tm=tm, chunk=_row_chunk(tm, keep, tm),
                             scaled=ss is not None, has_hist=has_hist, aliased=aliased)
    lo = lambda j: j
    hi = lambda j: j + nc
    conv_b = conv_b.reshape(conv_b.shape[0], 1, f2)
    in_specs = [
        pl.BlockSpec((tm, kdim), lambda i, j: (i + t0, 0), pipeline_mode=pl.Buffered(1)),
        _wspec(w_up, layer, kdim, tc, lo), _wspec(w_up, layer, kdim, tc, hi),
        _wspec(conv_w, layer, taps, tc, lo), _wspec(conv_w, layer, taps, tc, hi),
        _wspec(conv_b, layer, 1, tc, lo), _wspec(conv_b, layer, 1, tc, hi),
    ]
    args = [h, w_up, w_up, conv_w, conv_w, conv_b, conv_b]
    if ss is not None:
        in_specs.append(pl.BlockSpec((tm, LANES), lambda i, j: (i + t0, 0), pipeline_mode=pl.Buffered(1)))
        args.append(ss)
    if has_hist:
        in_specs += [pl.BlockSpec((None, keep, tc), lambda i, j: (i, 0, j)),
                     pl.BlockSpec((None, keep, tc), lambda i, j: (i, 0, j + nc))]
        args += [hist, hist]
    if aliased:
        in_specs.append(pl.BlockSpec(memory_space=pl.ANY))
        args.append(prev)
    sspec = pl.BlockSpec((None, keep, tc), lambda i, j: (i, 0, j))
    return pl.pallas_call(
        kern,
        grid=(nt, nc),
        in_specs=in_specs,
        out_specs=[pl.BlockSpec((tm, tc), lambda i, j: (i + t0, j)), sspec, sspec],
        out_shape=[jax.ShapeDtypeStruct((t, f), BF16), jax.ShapeDtypeStruct((nt, keep, f), F32),
                   jax.ShapeDtypeStruct((nt, keep, f), F32)],
        input_output_aliases={len(args) - 1: 0} if aliased else {},
        compiler_params=_cparams("parallel", "arbitrary"),
        name="ffn_up_conv_act",
    )(*args)


def _proj_conv_kernel(*refs, taps, stride, tm, chunk, has_hist, aliased, w_t):
    x_ref, w_ref, c_ref = refs[:3]
    h_ref = refs[3] if has_hist else None
    o_ref, s_ref = refs[3 + has_hist + aliased:]

    def emit(r, convs):
        o_ref[r, :] = _silu(convs[0])

    (s_ref[...],) = _proj_conv_chunks(x_ref, [w_ref[...].astype(BF16)], (c_ref,), (h_ref,), emit, taps=taps,
                                      stride=stride, tm=tm, chunk=chunk, w_t=w_t)


def _proj_conv(h, w, conv_w, *, layer, tm, stride, t0, nt, w_t=False, hist=None, prev=None):
    t, kdim = h.shape
    taps, c = conv_w.shape[-2:]
    keep = (taps - 1) * stride
    tc = _pick(c, 256)
    has_hist = hist is not None
    aliased = prev is not None
    kern = functools.partial(_proj_conv_kernel, taps=taps, stride=stride, tm=tm,
                             chunk=_row_chunk(tm, keep, ROW_CHUNK),
                             has_hist=has_hist, aliased=aliased, w_t=w_t)
    same = lambda j: j
    in_specs = [pl.BlockSpec((tm, kdim), lambda i, j: (i + t0, 0), pipeline_mode=pl.Buffered(1)),
                _wspec(w, layer, kdim, tc, same, w_t), _wspec(conv_w, layer, taps, tc, same)]
    args = [h, w, conv_w]
    if has_hist:
        in_specs.append(pl.BlockSpec((None, keep, tc), lambda i, j: (i, 0, j)))
        args.append(hist)
    if aliased:
        in_specs.append(pl.BlockSpec(memory_space=pl.ANY))
        args.append(prev)
    return pl.pallas_call(
        kern,
        grid=(nt, c // tc),
        in_specs=in_specs,
        out_specs=[pl.BlockSpec((tm, tc), lambda i, j: (i + t0, j)),
                   pl.BlockSpec((None, keep, tc), lambda i, j: (i, 0, j))],
        out_shape=[jax.ShapeDtypeStruct((t, c), F32), jax.ShapeDtypeStruct((nt, keep, c), F32)],
        input_output_aliases={len(args) - 1: 0} if aliased else {},
        compiler_params=_cparams("parallel", "arbitrary"),
        name="qkv_proj_conv",
    )(*args)


def _pool_kernel(*refs, stride, tm, aliased):
    h_ref, hist_ref, cnt_ref, w_ref, sc_ref, x_ref = refs[:6]
    o_ref = refs[-1]
    g = pl.program_id(1)
    off = POOL_HIST * stride
    cur = h_ref[...]
    run = jnp.concatenate([hist_ref[...], cur], axis=0)
    for k, width in enumerate(POOL_WINDOWS):
        back = (width // 2) * stride
        if back % SUBLANES:
            shifted = pltpu.roll(run, back, axis=0)
        else:
            shifted = jnp.concatenate([run[:back, :], run[:run.shape[0] - back, :]], axis=0)
        run = run + shifted if k == 0 else jnp.where(g >= k, run + shifted, run)
    pooled = run[off:, :] / cnt_ref[...] - cur
    y = _dot(pooled.astype(BF16), w_ref[...]) * sc_ref[...]
    o_ref[...] = x_ref[...] + y


def _pool_mixer(h, hist, cnt, pool_w, scale, x, *, tm, stride, t0, nt, prev=None):
    t, d = h.shape
    g = pool_w.shape[0]
    dg = d // g
    hr = POOL_HIST * stride
    aliased = prev is not None
    kern = functools.partial(_pool_kernel, stride=stride, tm=tm, aliased=aliased)
    in_specs = [
        pl.BlockSpec((tm, dg), lambda i, j: (i + t0, j)),
        pl.BlockSpec((None, hr, dg), lambda i, j: (i, 0, j)),
        pl.BlockSpec((None, tm, 1), lambda i, j: (j, i + t0, 0)),
        pl.BlockSpec((None, dg, dg), lambda i, j: (j, 0, 0)),
        pl.BlockSpec((1, dg), lambda i, j: (0, j)),
        pl.BlockSpec((tm, dg), lambda i, j: (i + t0, j)),
    ]
    args = [h, hist, cnt, pool_w, scale.reshape(1, d), x]
    if aliased:
        in_specs.append(pl.BlockSpec(memory_space=pl.ANY))
        args.append(prev)
    return pl.pallas_call(
        kern,
        grid=(nt, g),
        in_specs=in_specs,
        out_specs=pl.BlockSpec((tm, dg), lambda i, j: (i + t0, j)),
        out_shape=jax.ShapeDtypeStruct((t, d), F32),
        input_output_aliases={6: 0} if aliased else {},
        compiler_params=_cparams("parallel", "arbitrary"),
        name="pool_mixer",
    )(*args)


def _gdn_kernel(*refs, heads, dk, rows, seg, valid, carry, unroll, aliased):
    q_ref, k_ref, v_ref, pa_ref, pb_ref, z_ref, s0_ref, alog_ref, dtb_ref, on_ref, ex_ref = refs[:11]
    o_ref, sn_ref, gcb_s, bb_s, gct_s = refs[11 + aliased:16 + aliased]
    s_s = refs[16 + aliased] if carry else None
    dv = dk
    nseg = rows // seg
    sh = int(math.log2(seg))
    assert 1 << sh == seg and seg % BASE_BLOCK == 0

    if carry:
        @pl.when(pl.program_id(1) == 0)
        def _():
            s_s[...] = s0_ref[...]

    rid = lax.broadcasted_iota(jnp.int32, (rows, 1), 0)
    live = jnp.bitwise_and(rid, seg - 1) < valid
    rseg = lax.shift_right_logical(rid, sh)
    ri = lax.broadcasted_iota(jnp.int32, (rows, rows), 0)
    ci = lax.broadcasted_iota(jnp.int32, (rows, rows), 1)
    same = lax.shift_right_logical(ri, sh) == lax.shift_right_logical(ci, sh)
    causal = jnp.logical_and(same, ci <= ri)
    strict = jnp.logical_and(same, ci < ri)
    base = lax.shift_right_logical(ri, 3) == lax.shift_right_logical(ci, 3)
    eye = (ri == ci).astype(F32)
    levels = []
    s = BASE_BLOCK
    while s < seg:
        rb = lax.shift_right_logical(ri, int(math.log2(s)))
        cb = lax.shift_right_logical(ci, int(math.log2(s)))
        levels.append(jnp.logical_and(jnp.bitwise_and(rb, 1) == 1, cb == rb - 1))
        s *= 2

    a = pa_ref[...] + dtb_ref[...]
    softplus = jnp.maximum(a, 0.0) + jnp.log1p(jnp.exp(-jnp.abs(a)))
    glog = jnp.where(live, -jnp.exp(alog_ref[...]) * softplus, 0.0)
    beta = jnp.where(live, _sigmoid(pb_ref[...]), 0.0)
    gc = _dot(causal.astype(F32), glog, precision=HIGHEST)
    gcb_s[...] = _expand(gc, ex_ref[...], 3)
    bb_s[...] = _expand(beta, ex_ref[...], 2)
    gct_s[...] = gc.T

    def widen(x):
        if rows <= dk:
            return x[:, :rows]
        return jnp.concatenate([x] * (rows // dk), axis=1)

    def group(gi, carry_unused):
        hs = [gi * unroll + j for j in range(unroll)]
        los = [pl.multiple_of(h * dk, dk) for h in hs]
        each = range(unroll)
        qs = [q_ref[:, pl.ds(lo, dk)] for lo in los]
        ks = [k_ref[:, pl.ds(lo, dk)] for lo in los]
        vs = [v_ref[:, pl.ds(lo, dk)] for lo in los]
        zs = [z_ref[:, pl.ds(lo, dv)] for lo in los]
        gcbs = [gcb_s[:, pl.ds(lo, dk)] for lo in los]
        bbs = [bb_s[:, pl.ds(lo, dk)] for lo in los]
        grow = [gct_s[pl.ds(h, 1), :] for h in hs]
        if carry:
            sts = [s_s[h] for h in hs]
        qs = [q * lax.rsqrt(jnp.sum(q * q, axis=-1, keepdims=True) + 1e-6) * (dk ** -0.5) for q in qs]
        ks = [k * lax.rsqrt(jnp.sum(k * k, axis=-1, keepdims=True) + 1e-6) for k in ks]
        egcs = [jnp.exp(g) for g in gcbs]
        decays = [jnp.where(causal, jnp.exp(jnp.where(causal, widen(gcbs[j]) - grow[j], 0.0)), 0.0)
                  for j in each]
        kbf = [k.astype(BF16) for k in ks]
        qkks = [_dot_nt(jnp.concatenate([qs[j].astype(BF16), kbf[j]], axis=0), kbf[j]) for j in each]
        qks = [qkks[j][:rows] * decays[j] for j in each]
        amats = [jnp.where(strict, widen(bbs[j]) * decays[j] * qkks[j][rows:], 0.0) for j in each]
        pws = [-jnp.where(base, a, 0.0) for a in amats]
        invs = [eye + p for p in pws]
        for _ in range(2):
            pws = [_mm(p, p) for p in pws]
            invs = [invs[j] + _mm(pws[j], invs[j]) for j in each]
        for lvl in levels:
            lows = [_mm(jnp.where(lvl, amats[j], 0.0), invs[j]) for j in each]
            invs = [invs[j] - _mm(invs[j], lows[j]) for j in each]
        sols = [_mm(invs[j], jnp.concatenate([bbs[j] * vs[j], bbs[j] * egcs[j] * ks[j]], axis=1))
                for j in each]
        wvs = [sl[:, :dv] for sl in sols]
        wks = [sl[:, dv:] for sl in sols]
        qds = [qs[j] * egcs[j] for j in each]
        if carry:
            us = [[] for _ in each]
            os_ = [[] for _ in each]
            for sg in range(nseg):
                r = slice(sg * seg, (sg + 1) * seg)
                last = slice((sg + 1) * seg - 1, (sg + 1) * seg)
                both = [_dot(jnp.concatenate([wks[j][r], qds[j][r]], axis=0).astype(BF16), sts[j].astype(BF16))
                        for j in each]
                u = [wvs[j][r] - both[j][:seg] for j in each]
                kend = [ks[j][r] * jnp.exp(gcbs[j][last] - gcbs[j][r]) for j in each]
                sts = [jnp.exp(gcbs[j][last]) * sts[j] + _dot_tn(kend[j].astype(BF16), u[j].astype(BF16))
                       for j in each]
                for j in each:
                    us[j].append(u[j])
                    os_[j].append(both[j][seg:])
            u_all = [x[0] if nseg == 1 else jnp.concatenate(x, axis=0) for x in us]
            o = [x[0] if nseg == 1 else jnp.concatenate(x, axis=0) for x in os_]
        else:
            lhs = [jnp.concatenate([wks[j], qds[j]], axis=0).astype(BF16) for j in each]
            u_all = [jnp.zeros((rows, dv), F32) for _ in each]
            o = [jnp.zeros((rows, dv), F32) for _ in each]
            news = []
            for sg in range(nseg):
                last = slice((sg + 1) * seg - 1, (sg + 1) * seg)
                mine = rseg == sg
                st0 = [s0_ref[sg, h] for h in hs]
                both = [_dot(lhs[j], st0[j].astype(BF16)) for j in each]
                u_m = [jnp.where(mine, wvs[j] - both[j][:rows], 0.0) for j in each]
                u_all = [u_all[j] + u_m[j] for j in each]
                o = [o[j] + jnp.where(mine, both[j][rows:], 0.0) for j in each]
                kend = [jnp.where(mine, ks[j] * jnp.exp(jnp.where(mine, gcbs[j][last] - gcbs[j], 0.0)), 0.0)
                        for j in each]
                news.append([jnp.exp(gcbs[j][last]) * st0[j] + _dot_tn(kend[j].astype(BF16), u_m[j].astype(BF16))
                             for j in each])
        o = [o[j] + _mm(qks[j], u_all[j]) for j in each]
        o = [o[j] * lax.rsqrt(jnp.mean(o[j] * o[j], axis=-1, keepdims=True) + EPS) * on_ref[...] * _silu(zs[j])
             for j in each]
        for j in each:
            o_ref[:, pl.ds(los[j], dv)] = o[j].astype(o_ref.dtype)
            if carry:
                s_s[hs[j]] = sts[j]
            else:
                for sg in range(nseg):
                    sn_ref[sg, hs[j]] = news[sg][j]
        return carry_unused

    assert heads % unroll == 0
    lax.fori_loop(0, heads // unroll, group, 0)

    if carry:
        @pl.when(pl.program_id(1) == pl.num_programs(1) - 1)
        def _():
            sn_ref[...] = s_s[...]


def _gdn(qkv, proj, s0, a_log, dt_bias, out_norm, *, cols, nseq, seqlen, rows, seg, valid, carry,
         out_shape=None, out_block=0, prev=None):
    _, heads, dk, dv = s0.shape
    assert dk == LANES and dv == LANES and heads <= LANES and rows % seg == 0
    hk = heads * dk
    expand = np.zeros((LANES, hk), np.float32)
    for h in range(heads):
        expand[h, h * dk:(h + 1) * dk] = 1.0
    lane_pad = lambda t: jnp.pad(t.reshape(1, heads).astype(F32), ((0, 0), (0, LANES - heads)))
    col_z, col_a, col_b = cols
    assert col_z % hk == 0 and col_a % LANES == 0 and col_b % LANES == 0
    if carry:
        nblk = seqlen // rows
        grid = (nseq, nblk)
        rmap = lambda i, c: i * nblk + c
        smap = lambda i, c: (i, 0, 0, 0)
        sblock = (None, heads, dk, dv)
        sem = ("parallel", "arbitrary")
    else:
        spb = rows // seg
        grid = (nseq // spb, 1)
        rmap = lambda i, c: i
        smap = lambda i, c: (i, 0, 0, 0)
        sblock = (spb, heads, dk, dv)
        sem = ("parallel", "arbitrary")
    rspec = lambda w, cb: pl.BlockSpec((rows, w), lambda i, c: (rmap(i, c), cb))
    full = lambda shape: pl.BlockSpec(shape, lambda i, c: (0,) * len(shape))
    aliased = prev is not None
    kern = functools.partial(_gdn_kernel, heads=heads, dk=dk, rows=rows, seg=seg, valid=valid, carry=carry,
                             unroll=min(heads, 16 if carry else 8), aliased=aliased)
    scratch = [pltpu.VMEM((rows, hk), F32), pltpu.VMEM((rows, hk), F32), pltpu.VMEM((LANES, rows), F32)]
    if carry:
        scratch.append(pltpu.VMEM((heads, dk, dv), F32))
    if out_shape is None:
        out_shape = (nseq * seqlen, heads * dv)
    in_specs = [
        rspec(hk, 0), rspec(hk, 1), rspec(hk, 2),
        rspec(LANES, col_a // LANES), rspec(LANES, col_b // LANES), rspec(hk, col_z // hk),
        pl.BlockSpec(sblock, smap),
        full((1, LANES)), full((1, LANES)), full((1, dv)), full((LANES, hk)),
    ]
    args = [qkv, qkv, qkv, proj, proj, proj, s0, lane_pad(a_log), lane_pad(dt_bias),
            out_norm.reshape(1, dv).astype(F32), jnp.asarray(expand, dtype=BF16)]
    if aliased:
        in_specs.append(pl.BlockSpec(memory_space=pl.ANY))
        args.append(prev)
    return pl.pallas_call(
        kern,
        grid=grid,
        in_specs=in_specs,
        out_specs=[rspec(hk, out_block), pl.BlockSpec(sblock, smap)],
        out_shape=[jax.ShapeDtypeStruct(out_shape, BF16), jax.ShapeDtypeStruct(s0.shape, F32)],
        scratch_shapes=scratch,
        input_output_aliases={len(args) - 1: 0} if aliased else {},
        compiler_params=_cparams(*sem),
        name="gated_deltanet",
    )(*args)


def _ret_kernel(*refs, hps, dk, rows, seg, carry, aliased):
    q_ref, k_ref, v_ref, g_ref, cos_ref, sin_ref, s0_ref, dec_ref, qs_ref, ks_ref, ge_ref = refs[:11]
    o_ref, sn_ref = refs[11 + aliased:13 + aliased]
    s_s = refs[13 + aliased] if carry else None
    dv = dk
    half = dk // 2
    nseg = rows // seg
    sh = int(math.log2(seg))
    hstep = pl.program_id(1) if not carry else 0

    if carry:
        @pl.when(pl.program_id(1) == 0)
        def _():
            s_s[...] = s0_ref[...]

    cos = cos_ref[...]
    sin = sin_ref[...]
    rseg = lax.shift_right_logical(lax.broadcasted_iota(jnp.int32, (rows, 1), 0), sh)

    def rot(x):
        x1, x2 = x[:, :half], x[:, half:]
        return jnp.concatenate([x1 * cos - x2 * sin, x1 * sin + x2 * cos], axis=1)

    gsz = min(hps, RET_GROUP)
    assert hps % gsz == 0

    def group(gi, carry_unused):
        hhs = [gi * gsz + u for u in range(gsz)]
        hs = [hstep * hps + hh for hh in hhs]
        los = [pl.multiple_of(h * dk, dk) for h in hs]
        each = range(gsz)
        qs = [rot(q_ref[:, pl.ds(lo, dk)]) for lo in los]
        ks = [rot(k_ref[:, pl.ds(lo, dk)]) * (dk ** -0.5) for lo in los]
        vbf = [v_ref[:, pl.ds(lo, dv)].astype(BF16) for lo in los]
        gates = [g_ref[:, pl.ds(lo, dv)] for lo in los]
        ges = [ge_ref[h] for h in hs]
        qks = [_dot_nt(qs[u].astype(BF16), ks[u].astype(BF16)) * dec_ref[hs[u]] for u in each]
        qds = [(qs[u] * qs_ref[:, pl.ds(los[u], dk)]).astype(BF16) for u in each]
        kes = [ks[u] * ks_ref[:, pl.ds(los[u], dk)] for u in each]
        if carry:
            sts = [s_s[h] for h in hs]
            parts = [[] for _ in each]
            for sg in range(nseg):
                r = slice(sg * seg, (sg + 1) * seg)
                for u in each:
                    parts[u].append(_dot(qds[u][r], sts[u].astype(BF16)))
                sts = [ges[u] * sts[u] + _dot_tn(kes[u][r].astype(BF16), vbf[u][r]) for u in each]
            o = [x[0] if nseg == 1 else jnp.concatenate(x, axis=0) for x in parts]
        else:
            o = [jnp.zeros((rows, dv), F32) for _ in each]
            news = []
            for sg in range(nseg):
                mine = rseg == sg
                st0 = [s0_ref[sg, hh] for hh in hhs]
                o = [o[u] + jnp.where(mine, _dot(qds[u], st0[u].astype(BF16)), 0.0) for u in each]
                news.append([ges[u] * st0[u] + _dot_tn(jnp.where(mine, kes[u], 0.0).astype(BF16), vbf[u])
                             for u in each])
        o = [o[u] + _dot(qks[u].astype(BF16), vbf[u]) for u in each]
        mus = [jnp.mean(x, axis=-1, keepdims=True) for x in o]
        vrs = [jnp.mean(jnp.square(o[u] - mus[u]), axis=-1, keepdims=True) for u in each]
        o = [(o[u] - mus[u]) * lax.rsqrt(vrs[u] + EPS) * _silu(gates[u]) for u in each]
        for u in each:
            o_ref[:, pl.ds(los[u], dv)] = o[u].astype(o_ref.dtype)
            if carry:
                s_s[hs[u]] = sts[u]
            else:
                for sg in range(nseg):
                    sn_ref[sg, hhs[u]] = news[sg][u]
        return carry_unused

    lax.fori_loop(0, hps // gsz, group, 0)

    if carry:
        @pl.when(pl.program_id(1) == pl.num_programs(1) - 1)
        def _():
            sn_ref[...] = s_s[...]


def _retention(proj, s0, *, col_q, nseq, seqlen, rows, seg, valid, pos0, carry,
               out_shape=None, out_block=0, prev=None):
    _, heads, dk, dv = s0.shape
    assert dk == dv and rows % seg == 0
    hw = heads * dk
    half = dk // 2
    assert col_q % hw == 0
    cq = col_q // hw
    npos = seqlen if carry else rows
    offs = np.arange(npos) if carry else np.arange(rows) % seg
    inv_freq = ROPE_BASE ** (-np.arange(half, dtype=np.float64) / half)
    ang = (pos0 + offs.astype(np.float64))[:, None] * inv_freq[None, :]
    cos, sin = np.cos(ang).astype(np.float32), np.sin(ang).astype(np.float32)
    log_gamma = np.log1p(-np.exp2(-5.0 - np.arange(heads, dtype=np.float64)))
    pin = np.arange(rows) % seg
    steps = np.minimum(pin + 1, valid).astype(np.float64)
    gc = log_gamma[:, None] * steps[None, :]
    gl = log_gamma * min(seg, valid)
    sameseg = (np.arange(rows)[:, None] // seg) == (np.arange(rows)[None, :] // seg)
    causal = np.logical_and(sameseg, np.tril(np.ones((rows, rows), bool)))
    diff = gc[:, :, None] - gc[:, None, :]
    dec = np.where(causal[None], np.exp(np.where(causal[None], diff, 0.0)), 0.0).astype(np.float32)
    qs = np.repeat(np.exp(gc).T, dk, axis=1).astype(np.float32)
    ks = np.repeat(np.exp(gl[:, None] - gc).T, dk, axis=1).astype(np.float32)
    ge = np.broadcast_to(np.exp(gl)[:, None, None], (heads, 1, dv)).astype(np.float32)
    if carry:
        nblk = seqlen // rows
        hps = heads
        grid = (nseq, nblk)
        rmap = lambda i, c: i * nblk + c
        tmap = lambda i, c: (c, 0)
        smap = lambda i, c: (i, 0, 0, 0)
        sblock = (None, heads, dk, dv)
    else:
        spb = rows // seg
        hps = max(1, heads // 2)
        grid = (nseq // spb, heads // hps)
        rmap = lambda i, c: i
        tmap = lambda i, c: (0, 0)
        smap = lambda i, c: (i, c, 0, 0)
        sblock = (spb, hps, dk, dv)
    rspec = lambda cb: pl.BlockSpec((rows, hw), lambda i, c: (rmap(i, c), cb))
    full = lambda shape: pl.BlockSpec(shape, lambda i, c: (0,) * len(shape))
    aliased = prev is not None
    kern = functools.partial(_ret_kernel, hps=hps, dk=dk, rows=rows, seg=seg, carry=carry, aliased=aliased)
    if out_shape is None:
        out_shape = (nseq * seqlen, heads * dv)
    in_specs = [
        rspec(cq), rspec(cq + 1), rspec(cq + 2), rspec(cq + 3),
        pl.BlockSpec((rows, half), tmap), pl.BlockSpec((rows, half), tmap),
        pl.BlockSpec(sblock, smap),
        full((heads, rows, rows)), full((rows, hw)), full((rows, hw)), full((heads, 1, dv)),
    ]
    args = [proj, proj, proj, proj, jnp.asarray(cos), jnp.asarray(sin), s0, jnp.asarray(dec), jnp.asarray(qs),
            jnp.asarray(ks), jnp.asarray(ge)]
    if aliased:
        in_specs.append(pl.BlockSpec(memory_space=pl.ANY))
        args.append(prev)
    return pl.pallas_call(
        kern,
        grid=grid,
        in_specs=in_specs,
        out_specs=[rspec(out_block), pl.BlockSpec(sblock, smap)],
        out_shape=[jax.ShapeDtypeStruct(out_shape, BF16), jax.ShapeDtypeStruct(s0.shape, F32)],
        scratch_shapes=[pltpu.VMEM((heads, dk, dv), F32)] if carry else [],
        input_output_aliases={len(args) - 1: 0} if aliased else {},
        compiler_params=_cparams("parallel", "arbitrary"),
        name="retention",
    )(*args)


def _pack_w_rest(wt, gdn_c, heads_a, gdn_v):
    off_a = gdn_c
    off_b = off_a + heads_a
    off_z = off_b + heads_a
    off_r = off_z + gdn_v
    zpad = jnp.zeros((LANES - heads_a, wt.shape[1]), wt.dtype)
    return jnp.concatenate([wt[off_z:off_r], wt[off_r:], wt[off_a:off_b], zpad, wt[off_b:off_z], zpad], axis=0)


def kernel(x_prompt, x_sample, state_gdn_conv, state_gdn, state_ret, state_pool, state_ffn_conv, p_prompt, p_sample, norm_mix, norm_ffn, norm_ple, norm_final, w_in, gdn_conv_w, gdn_a_log, gdn_dt_bias, gdn_out_norm, w_out, pool_w, pool_scale, ffn_w_up, ffn_conv_w, ffn_conv_b, ffn_w_down, ple_w_gate, ple_w_proj):
    bp, lp, d = x_prompt.shape
    bs, ls, _ = x_sample.shape
    depth = p_prompt.shape[0]
    tp, ts = bp * lp, bs * ls
    t = tp + ts
    tm = ts
    assert lp % tm == 0 and tm % SUBLANES == 0
    ntp = tp // tm
    _, _, heads_a, dk_a, dv_a = state_gdn.shape
    _, _, heads_r, dk_r, dv_r = state_ret.shape
    gdn_c = state_gdn_conv.shape[-1]
    gdn_v = heads_a * dv_a
    ret_w = heads_r * dk_r
    gtaps = gdn_conv_w.shape[1]
    ftaps = ffn_conv_w.shape[1]
    pool_past = state_pool.shape[2]
    f2 = ffn_w_up.shape[-1]
    assert min(lp, ls) >= max(gtaps, ftaps) - 1 and pool_past == POOL_HIST - 1 and lp % SUBLANES == 0
    assert lp % CHUNK == 0 and ls <= BASE_BLOCK and gdn_c == 3 * gdn_v and heads_a * dk_a == gdn_v
    assert gdn_v % ret_w == 0 and heads_r * dv_r == ret_w
    col_z, col_r = 0, gdn_v
    col_a = col_r + 4 * ret_w
    col_b = col_a + LANES

    rows_p = lambda a: a.reshape((tp,) + a.shape[2:])
    rows_s = lambda a: jnp.swapaxes(a, 0, 1).reshape((ts,) + a.shape[2:])
    join = lambda a, b: jnp.concatenate([rows_p(a), rows_s(b)], axis=0)
    seq_p = lambda a: a[:tp].reshape((bp, lp) + a.shape[1:])
    seq_s = lambda a: jnp.swapaxes(a[tp:].reshape((ls, bs) + a.shape[1:]), 0, 1)

    tps = lp // tm
    tiles = lambda a: a.reshape((t // tm, tm) + a.shape[1:])

    def hist_p(a, nrows):
        tl = tiles(a)[:ntp, tm - nrows:]
        prev = jnp.concatenate([jnp.zeros_like(tl[:1]), tl[:-1]], axis=0)
        first = (jnp.arange(ntp) % tps == 0)[:, None, None]
        return jnp.where(first, 0.0, prev)

    def tail_p(a, nrows):
        return tiles(a)[tps - 1:ntp:tps, tm - nrows:]

    def hist_s(state, nrows):
        return jnp.swapaxes(state[:, -nrows:], 0, 1).reshape(1, nrows * bs, state.shape[-1])

    def pad_seq(a):
        a = seq_s(a)
        a = jnp.pad(a, ((0, 0), (0, BASE_BLOCK - ls)) + ((0, 0),) * (a.ndim - 2))
        return a.reshape((bs * BASE_BLOCK,) + a.shape[2:])

    unpad_seq = lambda a: rows_s(a.reshape((bs, BASE_BLOCK) + a.shape[1:])[:, :ls])

    xr = join(x_prompt, x_sample)
    pj = [join(p_prompt[i], p_sample[i]).astype(BF16) for i in range(depth)]
    pool_w_bf = pool_w.astype(BF16)
    w_down_bf = ffn_w_down.astype(BF16)
    w_proj_bf = ple_w_proj.astype(BF16)
    w_in_t = jnp.swapaxes(w_in, 1, 2)
    rows_prompt = _pick(lp, 256, CHUNK)
    spb = max(1, CHUNK // BASE_BLOCK)
    assert bs % spb == 0

    new_gdn_conv_p, new_gdn_p, new_ret_p, new_pool_p, new_ffn_p = [], [], [], [], []
    new_gdn_conv_s, new_gdn_s, new_ret_s, new_pool_s, new_ffn_s = [], [], [], [], []
    for i in range(depth):
        j = i // 2
        if i % 2 == 0:
            (h,) = _rmsnorm(xr, norm_mix[i], [BF16])
            qkv, cs_p = _proj_conv(h, w_in_t, gdn_conv_w, layer=j, tm=lp, stride=1, t0=0, nt=bp, w_t=True)
            qkv, cs_s = _proj_conv(h, w_in_t, gdn_conv_w, layer=j, tm=tm, stride=bs, t0=ntp, nt=1, w_t=True,
                                   hist=hist_s(state_gdn_conv[j], gtaps - 1), prev=qkv)
            proj = _matmul(h, _pack_w_rest(w_in_t[j], gdn_c, heads_a, gdn_v), w_t=True, name="in_proj")
            cols = (col_z, col_a, col_b)
            mshape = (t, gdn_v + heads_r * dv_r)
            mixed, s_a_p = _gdn(qkv, proj, jnp.zeros((bp,) + state_gdn.shape[2:], F32), gdn_a_log[j],
                                gdn_dt_bias[j], gdn_out_norm[j], cols=cols, nseq=bp, seqlen=lp,
                                rows=rows_prompt, seg=CHUNK, valid=CHUNK, carry=True, out_shape=mshape)
            mixed, s_b_p = _retention(proj, jnp.zeros((bp,) + state_ret.shape[2:], F32), col_q=col_r,
                                      nseq=bp, seqlen=lp, rows=rows_prompt, seg=CHUNK, valid=CHUNK, pos0=0,
                                      carry=True, out_shape=mshape, out_block=gdn_v // ret_w, prev=mixed)
            proj_s = pad_seq(proj)
            o_gdn_s, s_a_s = _gdn(pad_seq(qkv), proj_s, state_gdn[j], gdn_a_log[j], gdn_dt_bias[j],
                                  gdn_out_norm[j], cols=cols, nseq=bs, seqlen=BASE_BLOCK,
                                  rows=spb * BASE_BLOCK, seg=BASE_BLOCK, valid=ls, carry=False)
            o_ret_s, s_b_s = _retention(proj_s, state_ret[j], col_q=col_r, nseq=bs, seqlen=BASE_BLOCK,
                                        rows=spb * BASE_BLOCK, seg=BASE_BLOCK, valid=ls, pos0=PAST_LEN,
                                        carry=False)
            mixed_s = jnp.concatenate([unpad_seq(o_gdn_s), unpad_seq(o_ret_s)], axis=1)
            mixed = lax.dynamic_update_slice(mixed, mixed_s, (tp, 0))
            xr, h, ss = _matmul(mixed, w_out, layer=j, res=xr, gain=norm_ffn[i], name="out_proj")
            new_gdn_conv_p.append(cs_p)
            new_gdn_conv_s.append(jnp.swapaxes(cs_s.reshape(gtaps - 1, bs, gdn_c), 0, 1))
            new_gdn_p.append(s_a_p)
            new_gdn_s.append(s_a_s)
            new_ret_p.append(s_b_p)
            new_ret_s.append(s_b_s)
        else:
            h32, = _rmsnorm(xr, norm_mix[i], [F32])
            h_s = seq_s(h32)
            pos = jnp.concatenate([jnp.tile(jnp.arange(lp, dtype=jnp.int32), bp),
                                   jnp.repeat(PAST_LEN + jnp.arange(ls, dtype=jnp.int32), bs)])
            cnt = jnp.stack([jnp.minimum(w, pos + 1).astype(F32) for w in POOL_WINDOWS])[:, :, None]
            zero_row = jnp.zeros((bs, 1, d), F32)
            kw = dict(tm=tm)
            xn = _pool_mixer(h32, hist_p(h32, POOL_HIST), cnt, pool_w_bf[j], pool_scale[j], xr, stride=1,
                             t0=0, nt=ntp, **kw)
            xr = _pool_mixer(h32, hist_s(jnp.concatenate([zero_row, state_pool[j]], axis=1), POOL_HIST), cnt,
                             pool_w_bf[j], pool_scale[j], xr, stride=bs, t0=ntp, nt=1, prev=xn, **kw)
            new_pool_p.append(tail_p(h32, pool_past))
            new_pool_s.append(jnp.concatenate([state_pool[j], h_s], axis=1)[:, -pool_past:])
            ((h,), ss) = _rmsnorm(xr, norm_ffn[i], [BF16]), None
        act, sg_p, sv_p = _ffn_up_act(h, ffn_w_up, ffn_conv_w, ffn_conv_b, layer=i, tm=lp, stride=1, t0=0, nt=bp,
                                      ss=ss)
        act, sg_s, sv_s = _ffn_up_act(h, ffn_w_up, ffn_conv_w, ffn_conv_b, layer=i, tm=tm, stride=bs, t0=ntp,
                                      nt=1, ss=ss, hist=hist_s(state_ffn_conv[i], ftaps - 1), prev=act)
        xr, h, ss = _matmul(act, w_down_bf, layer=i, res=xr, gain=norm_ple[i], name="ffn_down")
        new_ffn_p.append(jnp.concatenate([sg_p, sv_p], axis=-1))
        up_s = jnp.concatenate([sg_s, sv_s], axis=-1).reshape(ftaps - 1, bs, f2)
        new_ffn_s.append(jnp.swapaxes(up_s, 0, 1))
        xr = _matmul(h, ple_w_gate, layer=i, ss=ss, res=xr, p=pj[i], wp=w_proj_bf, name="ple")
    (out_p,) = _rmsnorm(xr, norm_final, [F32], 0, tp)
    (out_s,) = _rmsnorm(xr, norm_final, [F32], tp, ts)
    st = lambda lst: jnp.stack(lst)
    return (out_p.reshape(bp, lp, d), jnp.swapaxes(out_s.reshape(ls, bs, d), 0, 1),
            st(new_gdn_conv_p), st(new_gdn_p), st(new_ret_p), st(new_pool_p), st(new_ffn_p),
            st(new_gdn_conv_s), st(new_gdn_s), st(new_ret_s), st(new_pool_s), st(new_ffn_s))
```

```python
import functools
import math

import numpy as np
import jax
import jax.numpy as jnp
from jax import lax
from jax.experimental import pallas as pl
from jax.experimental.pallas import tpu as pltpu

F32 = jnp.float32
BF16 = jnp.bfloat16
HIGHEST = lax.Precision.HIGHEST

EPS = 1e-6
CHUNK = 64
PAST_LEN = 16384
ROPE_BASE = 10000.0
POOL_WINDOWS = (2, 4, 8, 16)
POOL_HIST = 16
LANES = 128
SUBLANES = 8
VMEM_LIMIT = 56 * 1024 * 1024
X_TILE_BYTES = 24 * 1024 * 1024
BASE_BLOCK = 8
RET_GROUP = 8
ROW_CHUNK = 256


def _pick(n, pref, align=LANES):
    if n <= pref:
        return n
    t = (pref // align) * align
    while t >= align:
        if n % t == 0:
            return t
        t -= align
    return n


def _cparams(*sem):
    return pltpu.CompilerParams(dimension_semantics=sem, vmem_limit_bytes=VMEM_LIMIT)


def _silu(x):
    return x * (1.0 / (1.0 + jnp.exp(-x)))


def _sigmoid(x):
    return 1.0 / (1.0 + jnp.exp(-x))


def _dot(a, b, **kw):
    return jnp.dot(a, b, preferred_element_type=F32, **kw)


def _dot_nt(a, b, **kw):
    return lax.dot_general(a, b, (((1,), (1,)), ((), ())), preferred_element_type=F32, **kw)


def _dot_tn(a, b, **kw):
    return lax.dot_general(a, b, (((0,), (0,)), ((), ())), preferred_element_type=F32, **kw)


def _mm(a, b):
    return _dot(a.astype(BF16), b.astype(BF16))


def _expand(x, e_bf, passes):
    acc = None
    for _ in range(passes):
        piece = x.astype(BF16)
        term = _dot(piece, e_bf)
        acc = term if acc is None else acc + term
        x = x - piece.astype(F32)
    return acc


def _rmsnorm_kernel(x_ref, g_ref, *o_refs):
    x = x_ref[...]
    y = x * lax.rsqrt(jnp.mean(x * x, axis=-1, keepdims=True) + EPS) * g_ref[...]
    for o_ref in o_refs:
        o_ref[...] = y.astype(o_ref.dtype)


def _rmsnorm(x, gain, dtypes, row0=0, nrows=None):
    d = x.shape[1]
    m = x.shape[0] if nrows is None else nrows
    tm = _pick(math.gcd(m, row0) if row0 else m, 256, SUBLANES)
    t0 = row0 // tm
    return pl.pallas_call(
        _rmsnorm_kernel,
        grid=(m // tm,),
        in_specs=[pl.BlockSpec((tm, d), lambda i: (i + t0, 0)), pl.BlockSpec((1, d), lambda i: (0, 0))],
        out_specs=[pl.BlockSpec((tm, d), lambda i: (i, 0)) for _ in dtypes],
        out_shape=[jax.ShapeDtypeStruct((m, d), dt) for dt in dtypes],
        compiler_params=_cparams("parallel"),
        name="rmsnorm",
    )(x, gain.reshape(1, d))


def _rstd(ss_ref, width):
    return lax.rsqrt(jnp.sum(ss_ref[...], axis=-1, keepdims=True) * (1.0 / width) + EPS)


def _mm_kernel(*refs, epilogue, w_t, scaled, emit_norm):
    it = iter(refs)
    x_ref, w_ref = next(it), next(it)
    ss_in = next(it) if scaled else None
    res_ref = next(it) if epilogue in ("res", "ple") else None
    p_ref, wp_ref = (next(it), next(it)) if epilogue == "ple" else (None, None)
    gn_ref = next(it) if emit_norm else None
    o_ref = next(it)
    acc = (_dot_nt if w_t else _dot)(x_ref[...], w_ref[...].astype(BF16))
    if scaled:
        acc = acc * _rstd(ss_in, x_ref.shape[1])
    if epilogue == "res":
        acc = res_ref[...] + acc
    elif epilogue == "ple":
        acc = res_ref[...] + _sigmoid(acc) * _dot(p_ref[...], wp_ref[...])
    o_ref[...] = acc.astype(o_ref.dtype)
    if emit_norm:
        xb_ref, ss_ref = next(it), next(it)
        xb_ref[...] = (acc * gn_ref[...]).astype(BF16)

        @pl.when(pl.program_id(1) == 0)
        def _():
            ss_ref[...] = jnp.zeros(ss_ref.shape, F32)

        sq = acc * acc
        part = sq[:, :LANES]
        for k in range(1, sq.shape[1] // LANES):
            part = part + sq[:, k * LANES:(k + 1) * LANES]
        ss_ref[...] += part


def _wspec(w, layer, rows, tc, cmap, w_t=False):
    block, index = ((tc, rows), lambda j: (cmap(j), 0)) if w_t else ((rows, tc), lambda j: (0, cmap(j)))
    if w.ndim == 2:
        return pl.BlockSpec(block, lambda i, j: index(j))
    return pl.BlockSpec((None,) + block, lambda i, j: (layer,) + index(j))


def _matmul(x, w, *, layer=0, w_t=False, ss=None, res=None, p=None, wp=None, gain=None, tn=256,
            out_dtype=F32, name="matmul"):
    m, kdim = x.shape
    n = w.shape[-2] if w_t else w.shape[-1]
    tm = _pick(m, X_TILE_BYTES // (2 * kdim), 2 * SUBLANES)
    tn = _pick(n, tn)
    epilogue = "none" if res is None else ("res" if p is None else "ple")
    rows = lambda width: pl.BlockSpec((tm, width), lambda i, j: (i, 0), pipeline_mode=pl.Buffered(1))
    in_specs = [rows(kdim), _wspec(w, layer, kdim, tn, lambda j: j, w_t)]
    args = [x, w]
    if ss is not None:
        in_specs.append(rows(LANES))
        args.append(ss)
    if res is not None:
        in_specs.append(pl.BlockSpec((tm, tn), lambda i, j: (i, j)))
        args.append(res)
    if p is not None:
        in_specs += [rows(p.shape[1]), _wspec(wp, layer, p.shape[1], tn, lambda j: j)]
        args += [p, wp]
    tile = pl.BlockSpec((tm, tn), lambda i, j: (i, j))
    out_specs, out_shape = tile, jax.ShapeDtypeStruct((m, n), out_dtype)
    if gain is not None:
        in_specs.append(pl.BlockSpec((1, tn), lambda i, j: (0, j)))
        args.append(gain.reshape(1, n))
        out_specs = [tile, tile, pl.BlockSpec((tm, LANES), lambda i, j: (i, 0))]
        out_shape = [out_shape, jax.ShapeDtypeStruct((m, n), BF16), jax.ShapeDtypeStruct((m, LANES), F32)]
    return pl.pallas_call(
        functools.partial(_mm_kernel, epilogue=epilogue, w_t=w_t, scaled=ss is not None,
                          emit_norm=gain is not None),
        grid=(m // tm, n // tn),
        in_specs=in_specs,
        out_specs=out_specs,
        out_shape=out_shape,
        compiler_params=_cparams("parallel", "arbitrary"),
        name=name,
    )(*args)


def _row_chunk(tm, keep, pref):
    chunk = _pick(tm, pref, 2 * SUBLANES)
    return chunk if chunk >= keep else tm


def _causal_conv(cur, w_ref, hist, *, taps, stride):
    rows, hrows = cur.shape[0], hist.shape[0]
    ext = jnp.concatenate([hist, cur], axis=0)
    acc = w_ref[taps - 1:taps, :] * cur
    for j in range(taps - 1):
        back = (taps - 1 - j) * stride
        if back % SUBLANES == 0:
            piece = ext[hrows - back:hrows - back + rows, :]
        else:
            piece = pltpu.roll(ext, back, axis=0)[hrows:, :]
        acc = acc + w_ref[j:j + 1, :] * piece
    return acc


def _proj_conv_chunks(x_ref, ws, c_refs, h_refs, emit, *, taps, stride, tm, chunk, w_t=False, scale=None):
    keep = (taps - 1) * stride
    hrows = -(-keep // SUBLANES) * SUBLANES
    mm = _dot_nt if w_t else _dot
    tails = [jnp.zeros((hrows, w.shape[0 if w_t else 1]), F32) if h is None else h[...]
             for h, w in zip(h_refs, ws)]
    assert all(tl.shape[0] == hrows for tl in tails) and chunk >= hrows
    for c in range(tm // chunk):
        r = slice(c * chunk, (c + 1) * chunk)
        ups = [mm(x_ref[r, :], w) for w in ws]
        if scale is not None:
            ups = [u * scale[r, :] for u in ups]
        emit(r, [_causal_conv(u, cr, tl, taps=taps, stride=stride) for u, cr, tl in zip(ups, c_refs, tails)])
        tails = [u[chunk - hrows:, :] for u in ups]
    return [tl[hrows - keep:, :] for tl in tails]


def _ffn_up_kernel(*refs, taps, stride, tm, chunk, scaled, has_hist, aliased):
    x_ref, wg_ref, wv_ref, cg_ref, cv_ref, bg_ref, bv_ref = refs[:7]
    scale = _rstd(refs[7], x_ref.shape[1]) if scaled else None
    h_refs = (refs[7 + scaled], refs[8 + scaled]) if has_hist else (None, None)
    o_ref, sg_ref, sv_ref = refs[7 + scaled + 2 * has_hist + aliased:]

    tc = wg_ref.shape[1]
    side = lambda a, b: jnp.concatenate([a[...], b[...]], axis=1)
    bias = side(bg_ref, bv_ref)

    def emit(r, convs):
        y = convs[0] + bias
        o_ref[r, :] = (_silu(y[:, :tc]) * y[:, tc:]).astype(o_ref.dtype)

    w = side(wg_ref, wv_ref).astype(BF16)
    hist = side(*h_refs) if has_hist else None
    (tail,) = _proj_conv_chunks(x_ref, [w], (side(cg_ref, cv_ref),), (hist,), emit, taps=taps,
                                stride=stride, tm=tm, chunk=chunk, scale=scale)
    sg_ref[...] = tail[:, :tc]
    sv_ref[...] = tail[:, tc:]


def _ffn_up_act(h, w_up, conv_w, conv_b, *, layer, tm, stride, t0, nt, ss=None, hist=None, prev=None):
    t, kdim = h.shape
    f2 = w_up.shape[-1]
    f = f2 // 2
    taps = conv_w.shape[1]
    keep = (taps - 1) * stride
    tc = _pick(f, 256)
    nc = f // tc
    has_hist = hist is not None
    aliased = prev is not None
    kern = functools.partial(_ffn_up_kernel, taps=taps, stride=stride, tm=tm, chunk=_row_chunk(tm, keep, tm),
                             scaled=ss is not None, has_hist=has_hist, aliased=aliased)
    lo = lambda j: j
    hi = lambda j: j + nc
    conv_b = conv_b.reshape(conv_b.shape[0], 1, f2)
    in_specs = [
        pl.BlockSpec((tm, kdim), lambda i, j: (i + t0, 0), pipeline_mode=pl.Buffered(1)),
        _wspec(w_up, layer, kdim, tc, lo), _wspec(w_up, layer, kdim, tc, hi),
        _wspec(conv_w, layer, taps, tc, lo), _wspec(conv_w, layer, taps, tc, hi),
        _wspec(conv_b, layer, 1, tc, lo), _wspec(conv_b, layer, 1, tc, hi),
    ]
    args = [h, w_up, w_up, conv_w, conv_w, conv_b, conv_b]
    if ss is not None:
        in_specs.append(pl.BlockSpec((tm, LANES), lambda i, j: (i + t0, 0), pipeline_mode=pl.Buffered(1)))
        args.append(ss)
    if has_hist:
        in_specs += [pl.BlockSpec((None, keep, tc), lambda i, j: (i, 0, j)),
                     pl.BlockSpec((None, keep, tc), lambda i, j: (i, 0, j + nc))]
        args += [hist, hist]
    if aliased:
        in_specs.append(pl.BlockSpec(memory_space=pl.ANY))
        args.append(prev)
    sspec = pl.BlockSpec((None, keep, tc), lambda i, j: (i, 0, j))
    return pl.pallas_call(
        kern,
        grid=(nt, nc),
        in_specs=in_specs,
        out_specs=[pl.BlockSpec((tm, tc), lambda i, j: (i + t0, j)), sspec, sspec],
        out_shape=[jax.ShapeDtypeStruct((t, f), BF16), jax.ShapeDtypeStruct((nt, keep, f), F32),
                   jax.ShapeDtypeStruct((nt, keep, f), F32)],
        input_output_aliases={len(args) - 1: 0} if aliased else {},
        compiler_params=_cparams("parallel", "arbitrary"),
        name="ffn_up_conv_act",
    )(*args)


def _proj_conv_kernel(*refs, taps, stride, tm, chunk, has_hist, aliased, w_t):
    x_ref, w_ref, c_ref = refs[:3]
    h_ref = refs[3] if has_hist else None
    o_ref, s_ref = refs[3 + has_hist + aliased:]

    def emit(r, convs):
        o_ref[r, :] = _silu(convs[0])

    (s_ref[...],) = _proj_conv_chunks(x_ref, [w_ref[...].astype(BF16)], (c_ref,), (h_ref,), emit, taps=taps,
                                      stride=stride, tm=tm, chunk=chunk, w_t=w_t)


def _proj_conv(h, w, conv_w, *, layer, tm, stride, t0, nt, w_t=False, hist=None, prev=None):
    t, kdim = h.shape
    taps, c = conv_w.shape[-2:]
    keep = (taps - 1) * stride
    tc = _pick(c, 256)
    has_hist = hist is not None
    aliased = prev is not None
    kern = functools.partial(_proj_conv_kernel, taps=taps, stride=stride, tm=tm,
                             chunk=_row_chunk(tm, keep, ROW_CHUNK),
                             has_hist=has_hist, aliased=aliased, w_t=w_t)
    same = lambda j: j
    in_specs = [pl.BlockSpec((tm, kdim), lambda i, j: (i + t0, 0), pipeline_mode=pl.Buffered(1)),
                _wspec(w, layer, kdim, tc, same, w_t), _wspec(conv_w, layer, taps, tc, same)]
    args = [h, w, conv_w]
    if has_hist:
        in_specs.append(pl.BlockSpec((None, keep, tc), lambda i, j: (i, 0, j)))
        args.append(hist)
    if aliased:
        in_specs.append(pl.BlockSpec(memory_space=pl.ANY))
        args.append(prev)
    return pl.pallas_call(
        kern,
        grid=(nt, c // tc),
        in_specs=in_specs,
        out_specs=[pl.BlockSpec((tm, tc), lambda i, j: (i + t0, j)),
                   pl.BlockSpec((None, keep, tc), lambda i, j: (i, 0, j))],
        out_shape=[jax.ShapeDtypeStruct((t, c), F32), jax.ShapeDtypeStruct((nt, keep, c), F32)],
        input_output_aliases={len(args) - 1: 0} if aliased else {},
        compiler_params=_cparams("parallel", "arbitrary"),
        name="qkv_proj_conv",
    )(*args)


def _pool_kernel(*refs, stride, tm, aliased):
    h_ref, hist_ref, cnt_ref, w_ref, sc_ref, x_ref = refs[:6]
    o_ref = refs[-1]
    g = pl.program_id(1)
    off = POOL_HIST * stride
    cur = h_ref[...]
    run = jnp.concatenate([hist_ref[...], cur], axis=0)
    for k, width in enumerate(POOL_WINDOWS):
        back = (width // 2) * stride
        if back % SUBLANES:
            shifted = pltpu.roll(run, back, axis=0)
        else:
            shifted = jnp.concatenate([run[:back, :], run[:run.shape[0] - back, :]], axis=0)
        run = run + shifted if k == 0 else jnp.where(g >= k, run + shifted, run)
    pooled = run[off:, :] / cnt_ref[...] - cur
    y = _dot(pooled.astype(BF16), w_ref[...]) * sc_ref[...]
    o_ref[...] = x_ref[...] + y


def _pool_mixer(h, hist, cnt, pool_w, scale, x, *, tm, stride, t0, nt, prev=None):
    t, d = h.shape
    g = pool_w.shape[0]
    dg = d // g
    hr = POOL_HIST * stride
    aliased = prev is not None
    kern = functools.partial(_pool_kernel, stride=stride, tm=tm, aliased=aliased)
    in_specs = [
        pl.BlockSpec((tm, dg), lambda i, j: (i + t0, j)),
        pl.BlockSpec((None, hr, dg), lambda i, j: (i, 0, j)),
        pl.BlockSpec((None, tm, 1), lambda i, j: (j, i + t0, 0)),
        pl.BlockSpec((None, dg, dg), lambda i, j: (j, 0, 0)),
        pl.BlockSpec((1, dg), lambda i, j: (0, j)),
        pl.BlockSpec((tm, dg), lambda i, j: (i + t0, j)),
    ]
    args = [h, hist, cnt, pool_w, scale.reshape(1, d), x]
    if aliased:
        in_specs.append(pl.BlockSpec(memory_space=pl.ANY))
        args.append(prev)
    return pl.pallas_call(
        kern,
        grid=(nt, g),
        in_specs=in_specs,
        out_specs=pl.BlockSpec((tm, dg), lambda i, j: (i + t0, j)),
        out_shape=jax.ShapeDtypeStruct((t, d), F32),
        input_output_aliases={6: 0} if aliased else {},
        compiler_params=_cparams("parallel", "arbitrary"),
        name="pool_mixer",
    )(*args)


def _gdn_kernel(*refs, heads, dk, rows, seg, valid, carry, unroll, aliased):
    q_ref, k_ref, v_ref, pa_ref, pb_ref, z_ref, s0_ref, alog_ref, dtb_ref, on_ref, ex_ref = refs[:11]
    o_ref, sn_ref, gcb_s, bb_s, gct_s = refs[11 + aliased:16 + aliased]
    s_s = refs[16 + aliased] if carry else None
    dv = dk
    nseg = rows // seg
    sh = int(math.log2(seg))
    assert 1 << sh == seg and seg % BASE_BLOCK == 0

    if carry:
        @pl.when(pl.program_id(1) == 0)
        def _():
            s_s[...] = s0_ref[...]

    rid = lax.broadcasted_iota(jnp.int32, (rows, 1), 0)
    live = jnp.bitwise_and(rid, seg - 1) < valid
    rseg = lax.shift_right_logical(rid, sh)
    ri = lax.broadcasted_iota(jnp.int32, (rows, rows), 0)
    ci = lax.broadcasted_iota(jnp.int32, (rows, rows), 1)
    same = lax.shift_right_logical(ri, sh) == lax.shift_right_logical(ci, sh)
    causal = jnp.logical_and(same, ci <= ri)
    strict = jnp.logical_and(same, ci < ri)
    base = lax.shift_right_logical(ri, 3) == lax.shift_right_logical(ci, 3)
    eye = (ri == ci).astype(F32)
    levels = []
    s = BASE_BLOCK
    while s < seg:
        rb = lax.shift_right_logical(ri, int(math.log2(s)))
        cb = lax.shift_right_logical(ci, int(math.log2(s)))
        levels.append(jnp.logical_and(jnp.bitwise_and(rb, 1) == 1, cb == rb - 1))
        s *= 2

    a = pa_ref[...] + dtb_ref[...]
    softplus = jnp.maximum(a, 0.0) + jnp.log1p(jnp.exp(-jnp.abs(a)))
    glog = jnp.where(live, -jnp.exp(alog_ref[...]) * softplus, 0.0)
    beta = jnp.where(live, _sigmoid(pb_ref[...]), 0.0)
    gc = _dot(causal.astype(F32), glog, precision=HIGHEST)
    gcb_s[...] = _expand(gc, ex_ref[...], 3)
    bb_s[...] = _expand(beta, ex_ref[...], 2)
    gct_s[...] = gc.T

    def widen(x):
        if rows <= dk:
            return x[:, :rows]
        return jnp.concatenate([x] * (rows // dk), axis=1)

    def group(gi, carry_unused):
        hs = [gi * unroll + j for j in range(unroll)]
        los = [pl.multiple_of(h * dk, dk) for h in hs]
        each = range(unroll)
        qs = [q_ref[:, pl.ds(lo, dk)] for lo in los]
        ks = [k_ref[:, pl.ds(lo, dk)] for lo in los]
        vs = [v_ref[:, pl.ds(lo, dk)] for lo in los]
        zs = [z_ref[:, pl.ds(lo, dv)] for lo in los]
        gcbs = [gcb_s[:, pl.ds(lo, dk)] for lo in los]
        bbs = [bb_s[:, pl.ds(lo, dk)] for lo in los]
        grow = [gct_s[pl.ds(h, 1), :] for h in hs]
        if carry:
            sts = [s_s[h] for h in hs]
        qs = [q * lax.rsqrt(jnp.sum(q * q, axis=-1, keepdims=True) + 1e-6) * (dk ** -0.5) for q in qs]
        ks = [k * lax.rsqrt(jnp.sum(k * k, axis=-1, keepdims=True) + 1e-6) for k in ks]
        egcs = [jnp.exp(g) for g in gcbs]
        decays = [jnp.where(causal, jnp.exp(jnp.where(causal, widen(gcbs[j]) - grow[j], 0.0)), 0.0)
                  for j in each]
        kbf = [k.astype(BF16) for k in ks]
        qkks = [_dot_nt(jnp.concatenate([qs[j].astype(BF16), kbf[j]], axis=0), kbf[j]) for j in each]
        qks = [qkks[j][:rows] * decays[j] for j in each]
        amats = [jnp.where(strict, widen(bbs[j]) * decays[j] * qkks[j][rows:], 0.0) for j in each]
        pws = [-jnp.where(base, a, 0.0) for a in amats]
        invs = [eye + p for p in pws]
        for _ in range(2):
            pws = [_mm(p, p) for p in pws]
            invs = [invs[j] + _mm(pws[j], invs[j]) for j in each]
        for lvl in levels:
            lows = [_mm(jnp.where(lvl, amats[j], 0.0), invs[j]) for j in each]
            invs = [invs[j] - _mm(invs[j], lows[j]) for j in each]
        sols = [_mm(invs[j], jnp.concatenate([bbs[j] * vs[j], bbs[j] * egcs[j] * ks[j]], axis=1))
                for j in each]
        wvs = [sl[:, :dv] for sl in sols]
        wks = [sl[:, dv:] for sl in sols]
        qds = [qs[j] * egcs[j] for j in each]
        if carry:
            us = [[] for _ in each]
            os_ = [[] for _ in each]
            for sg in range(nseg):
                r = slice(sg * seg, (sg + 1) * seg)
                last = slice((sg + 1) * seg - 1, (sg + 1) * seg)
                both = [_dot(jnp.concatenate([wks[j][r], qds[j][r]], axis=0).astype(BF16), sts[j].astype(BF16))
                        for j in each]
                u = [wvs[j][r] - both[j][:seg] for j in each]
                kend = [ks[j][r] * jnp.exp(gcbs[j][last] - gcbs[j][r]) for j in each]
                sts = [jnp.exp(gcbs[j][last]) * sts[j] + _dot_tn(kend[j].astype(BF16), u[j].astype(BF16))
                       for j in each]
                for j in each:
                    us[j].append(u[j])
                    os_[j].append(both[j][seg:])
            u_all = [x[0] if nseg == 1 else jnp.concatenate(x, axis=0) for x in us]
            o = [x[0] if nseg == 1 else jnp.concatenate(x, axis=0) for x in os_]
        else:
            lhs = [jnp.concatenate([wks[j], qds[j]], axis=0).astype(BF16) for j in each]
            u_all = [jnp.zeros((rows, dv), F32) for _ in each]
            o = [jnp.zeros((rows, dv), F32) for _ in each]
            news = []
            for sg in range(nseg):
                last = slice((sg + 1) * seg - 1, (sg + 1) * seg)
                mine = rseg == sg
                st0 = [s0_ref[sg, h] for h in hs]
                both = [_dot(lhs[j], st0[j].astype(BF16)) for j in each]
                u_m = [jnp.where(mine, wvs[j] - both[j][:rows], 0.0) for j in each]
                u_all = [u_all[j] + u_m[j] for j in each]
                o = [o[j] + jnp.where(mine, both[j][rows:], 0.0) for j in each]
                kend = [jnp.where(mine, ks[j] * jnp.exp(jnp.where(mine, gcbs[j][last] - gcbs[j], 0.0)), 0.0)
                        for j in each]
                news.append([jnp.exp(gcbs[j][last]) * st0[j] + _dot_tn(kend[j].astype(BF16), u_m[j].astype(BF16))
                             for j in each])
        o = [o[j] + _mm(qks[j], u_all[j]) for j in each]
        o = [o[j] * lax.rsqrt(jnp.mean(o[j] * o[j], axis=-1, keepdims=True) + EPS) * on_ref[...] * _silu(zs[j])
             for j in each]
        for j in each:
            o_ref[:, pl.ds(los[j], dv)] = o[j].astype(o_ref.dtype)
            if carry:
                s_s[hs[j]] = sts[j]
            else:
                for sg in range(nseg):
                    sn_ref[sg, hs[j]] = news[sg][j]
        return carry_unused

    assert heads % unroll == 0
    lax.fori_loop(0, heads // unroll, group, 0)

    if carry:
        @pl.when(pl.program_id(1) == pl.num_programs(1) - 1)
        def _():
            sn_ref[...] = s_s[...]


def _gdn(qkv, proj, s0, a_log, dt_bias, out_norm, *, cols, nseq, seqlen, rows, seg, valid, carry,
         out_shape=None, out_block=0, prev=None):
    _, heads, dk, dv = s0.shape
    assert dk == LANES and dv == LANES and heads <= LANES and rows % seg == 0
    hk = heads * dk
    expand = np.zeros((LANES, hk), np.float32)
    for h in range(heads):
        expand[h, h * dk:(h + 1) * dk] = 1.0
    lane_pad = lambda t: jnp.pad(t.reshape(1, heads).astype(F32), ((0, 0), (0, LANES - heads)))
    col_z, col_a, col_b = cols
    assert col_z % hk == 0 and col_a % LANES == 0 and col_b % LANES == 0
    if carry:
        nblk = seqlen // rows
        grid = (nseq, nblk)
        rmap = lambda i, c: i * nblk + c
        smap = lambda i, c: (i, 0, 0, 0)
        sblock = (None, heads, dk, dv)
        sem = ("parallel", "arbitrary")
    else:
        spb = rows // seg
        grid = (nseq // spb, 1)
        rmap = lambda i, c: i
        smap = lambda i, c: (i, 0, 0, 0)
        sblock = (spb, heads, dk, dv)
        sem = ("parallel", "arbitrary")
    rspec = lambda w, cb: pl.BlockSpec((rows, w), lambda i, c: (rmap(i, c), cb))
    full = lambda shape: pl.BlockSpec(shape, lambda i, c: (0,) * len(shape))
    aliased = prev is not None
    kern = functools.partial(_gdn_kernel, heads=heads, dk=dk, rows=rows, seg=seg, valid=valid, carry=carry,
                             unroll=min(heads, 16), aliased=aliased)
    scratch = [pltpu.VMEM((rows, hk), F32), pltpu.VMEM((rows, hk), F32), pltpu.VMEM((LANES, rows), F32)]
    if carry:
        scratch.append(pltpu.VMEM((heads, dk, dv), F32))
    if out_shape is None:
        out_shape = (nseq * seqlen, heads * dv)
    in_specs = [
        rspec(hk, 0), rspec(hk, 1), rspec(hk, 2),
        rspec(LANES, col_a // LANES), rspec(LANES, col_b // LANES), rspec(hk, col_z // hk),
        pl.BlockSpec(sblock, smap),
        full((1, LANES)), full((1, LANES)), full((1, dv)), full((LANES, hk)),
    ]
    args = [qkv, qkv, qkv, proj, proj, proj, s0, lane_pad(a_log), lane_pad(dt_bias),
            out_norm.reshape(1, dv).astype(F32), jnp.asarray(expand, dtype=BF16)]
    if aliased:
        in_specs.append(pl.BlockSpec(memory_space=pl.ANY))
        args.append(prev)
    return pl.pallas_call(
        kern,
        grid=grid,
        in_specs=in_specs,
        out_specs=[rspec(hk, out_block), pl.BlockSpec(sblock, smap)],
        out_shape=[jax.ShapeDtypeStruct(out_shape, BF16), jax.ShapeDtypeStruct(s0.shape, F32)],
        scratch_shapes=scratch,
        input_output_aliases={len(args) - 1: 0} if aliased else {},
        compiler_params=_cparams(*sem),
        name="gated_deltanet",
    )(*args)


def _ret_kernel(*refs, hps, dk, rows, seg, carry, aliased):
    q_ref, k_ref, v_ref, g_ref, cos_ref, sin_ref, s0_ref, dec_ref, qs_ref, ks_ref, ge_ref = refs[:11]
    o_ref, sn_ref = refs[11 + aliased:13 + aliased]
    s_s = refs[13 + aliased] if carry else None
    dv = dk
    half = dk // 2
    nseg = rows // seg
    sh = int(math.log2(seg))
    hstep = pl.program_id(1) if not carry else 0

    if carry:
        @pl.when(pl.program_id(1) == 0)
        def _():
            s_s[...] = s0_ref[...]

    cos = cos_ref[...]
    sin = sin_ref[...]
    rseg = lax.shift_right_logical(lax.broadcasted_iota(jnp.int32, (rows, 1), 0), sh)

    def rot(x):
        x1, x2 = x[:, :half], x[:, half:]
        return jnp.concatenate([x1 * cos - x2 * sin, x1 * sin + x2 * cos], axis=1)

    gsz = min(hps, RET_GROUP)
    assert hps % gsz == 0

    def group(gi, carry_unused):
        hhs = [gi * gsz + u for u in range(gsz)]
        hs = [hstep * hps + hh for hh in hhs]
        los = [pl.multiple_of(h * dk, dk) for h in hs]
        each = range(gsz)
        qs = [rot(q_ref[:, pl.ds(lo, dk)]) for lo in los]
        ks = [rot(k_ref[:, pl.ds(lo, dk)]) * (dk ** -0.5) for lo in los]
        vbf = [v_ref[:, pl.ds(lo, dv)].astype(BF16) for lo in los]
        gates = [g_ref[:, pl.ds(lo, dv)] for lo in los]
        ges = [ge_ref[h] for h in hs]
        qks = [_dot_nt(qs[u].astype(BF16), ks[u].astype(BF16)) * dec_ref[hs[u]] for u in each]
        qds = [(qs[u] * qs_ref[:, pl.ds(los[u], dk)]).astype(BF16) for u in each]
        kes = [ks[u] * ks_ref[:, pl.ds(los[u], dk)] for u in each]
        if carry:
            sts = [s_s[h] for h in hs]
            parts = [[] for _ in each]
            for sg in range(nseg):
                r = slice(sg * seg, (sg + 1) * seg)
                for u in each:
                    parts[u].append(_dot(qds[u][r], sts[u].astype(BF16)))
                sts = [ges[u] * sts[u] + _dot_tn(kes[u][r].astype(BF16), vbf[u][r]) for u in each]
            o = [x[0] if nseg == 1 else jnp.concatenate(x, axis=0) for x in parts]
        else:
            o = [jnp.zeros((rows, dv), F32) for _ in each]
            news = []
            for sg in range(nseg):
                mine = rseg == sg
                st0 = [s0_ref[sg, hh] for hh in hhs]
                o = [o[u] + jnp.where(mine, _dot(qds[u], st0[u].astype(BF16)), 0.0) for u in each]
                news.append([ges[u] * st0[u] + _dot_tn(jnp.where(mine, kes[u], 0.0).astype(BF16), vbf[u])
                             for u in each])
        o = [o[u] + _dot(qks[u].astype(BF16), vbf[u]) for u in each]
        mus = [jnp.mean(x, axis=-1, keepdims=True) for x in o]
        vrs = [jnp.mean(jnp.square(o[u] - mus[u]), axis=-1, keepdims=True) for u in each]
        o = [(o[u] - mus[u]) * lax.rsqrt(vrs[u] + EPS) * _silu(gates[u]) for u in each]
        for u in each:
            o_ref[:, pl.ds(los[u], dv)] = o[u].astype(o_ref.dtype)
            if carry:
                s_s[hs[u]] = sts[u]
            else:
                for sg in range(nseg):
                    sn_ref[sg, hhs[u]] = news[sg][u]
        return carry_unused

    lax.fori_loop(0, hps // gsz, group, 0)

    if carry:
        @pl.when(pl.program_id(1) == pl.num_programs(1) - 1)
        def _():
            sn_ref[...] = s_s[...]


def _retention(proj, s0, *, col_q, nseq, seqlen, rows, seg, valid, pos0, carry,
               out_shape=None, out_block=0, prev=None):
    _, heads, dk, dv = s0.shape
    assert dk == dv and rows % seg == 0
    hw = heads * dk
    half = dk // 2
    assert col_q % hw == 0
    cq = col_q // hw
    npos = seqlen if carry else rows
    offs = np.arange(npos) if carry else np.arange(rows) % seg
    inv_freq = ROPE_BASE ** (-np.arange(half, dtype=np.float64) / half)
    ang = (pos0 + offs.astype(np.float64))[:, None] * inv_freq[None, :]
    cos, sin = np.cos(ang).astype(np.float32), np.sin(ang).astype(np.float32)
    log_gamma = np.log1p(-np.exp2(-5.0 - np.arange(heads, dtype=np.float64)))
    pin = np.arange(rows) % seg
    steps = np.minimum(pin + 1, valid).astype(np.float64)
    gc = log_gamma[:, None] * steps[None, :]
    gl = log_gamma * min(seg, valid)
    sameseg = (np.arange(rows)[:, None] // seg) == (np.arange(rows)[None, :] // seg)
    causal = np.logical_and(sameseg, np.tril(np.ones((rows, rows), bool)))
    diff = gc[:, :, None] - gc[:, None, :]
    dec = np.where(causal[None], np.exp(np.where(causal[None], diff, 0.0)), 0.0).astype(np.float32)
    qs = np.repeat(np.exp(gc).T, dk, axis=1).astype(np.float32)
    ks = np.repeat(np.exp(gl[:, None] - gc).T, dk, axis=1).astype(np.float32)
    ge = np.broadcast_to(np.exp(gl)[:, None, None], (heads, 1, dv)).astype(np.float32)
    if carry:
        nblk = seqlen // rows
        hps = heads
        grid = (nseq, nblk)
        rmap = lambda i, c: i * nblk + c
        tmap = lambda i, c: (c, 0)
        smap = lambda i, c: (i, 0, 0, 0)
        sblock = (None, heads, dk, dv)
    else:
        spb = rows // seg
        hps = max(1, heads // 2)
        grid = (nseq // spb, heads // hps)
        rmap = lambda i, c: i
        tmap = lambda i, c: (0, 0)
        smap = lambda i, c: (i, c, 0, 0)
        sblock = (spb, hps, dk, dv)
    rspec = lambda cb: pl.BlockSpec((rows, hw), lambda i, c: (rmap(i, c), cb))
    full = lambda shape: pl.BlockSpec(shape, lambda i, c: (0,) * len(shape))
    aliased = prev is not None
    kern = functools.partial(_ret_kernel, hps=hps, dk=dk, rows=rows, seg=seg, carry=carry, aliased=aliased)
    if out_shape is None:
        out_shape = (nseq * seqlen, heads * dv)
    in_specs = [
        rspec(cq), rspec(cq + 1), rspec(cq + 2), rspec(cq + 3),
        pl.BlockSpec((rows, half), tmap), pl.BlockSpec((rows, half), tmap),
        pl.BlockSpec(sblock, smap),
        full((heads, rows, rows)), full((rows, hw)), full((rows, hw)), full((heads, 1, dv)),
    ]
    args = [proj, proj, proj, proj, jnp.asarray(cos), jnp.asarray(sin), s0, jnp.asarray(dec), jnp.asarray(qs),
            jnp.asarray(ks), jnp.asarray(ge)]
    if aliased:
        in_specs.append(pl.BlockSpec(memory_space=pl.ANY))
        args.append(prev)
    return pl.pallas_call(
        kern,
        grid=grid,
        in_specs=in_specs,
        out_specs=[rspec(out_block), pl.BlockSpec(sblock, smap)],
        out_shape=[jax.ShapeDtypeStruct(out_shape, BF16), jax.ShapeDtypeStruct(s0.shape, F32)],
        scratch_shapes=[pltpu.VMEM((heads, dk, dv), F32)] if carry else [],
        input_output_aliases={len(args) - 1: 0} if aliased else {},
        compiler_params=_cparams("parallel", "arbitrary"),
        name="retention",
    )(*args)


def _pack_w_rest(wt, gdn_c, heads_a, gdn_v):
    off_a = gdn_c
    off_b = off_a + heads_a
    off_z = off_b + heads_a
    off_r = off_z + gdn_v
    zpad = jnp.zeros((LANES - heads_a, wt.shape[1]), wt.dtype)
    return jnp.concatenate([wt[off_z:off_r], wt[off_r:], wt[off_a:off_b], zpad, wt[off_b:off_z], zpad], axis=0)


def kernel(x_prompt, x_sample, state_gdn_conv, state_gdn, state_ret, state_pool, state_ffn_conv, p_prompt, p_sample, norm_mix, norm_ffn, norm_ple, norm_final, w_in, gdn_conv_w, gdn_a_log, gdn_dt_bias, gdn_out_norm, w_out, pool_w, pool_scale, ffn_w_up, ffn_conv_w, ffn_conv_b, ffn_w_down, ple_w_gate, ple_w_proj):
    bp, lp, d = x_prompt.shape
    bs, ls, _ = x_sample.shape
    depth = p_prompt.shape[0]
    tp, ts = bp * lp, bs * ls
    t = tp + ts
    tm = ts
    assert lp % tm == 0 and tm % SUBLANES == 0
    ntp = tp // tm
    _, _, heads_a, dk_a, dv_a = state_gdn.shape
    _, _, heads_r, dk_r, dv_r = state_ret.shape
    gdn_c = state_gdn_conv.shape[-1]
    gdn_v = heads_a * dv_a
    ret_w = heads_r * dk_r
    gtaps = gdn_conv_w.shape[1]
    ftaps = ffn_conv_w.shape[1]
    pool_past = state_pool.shape[2]
    f2 = ffn_w_up.shape[-1]
    assert min(lp, ls) >= max(gtaps, ftaps) - 1 and pool_past == POOL_HIST - 1 and lp % SUBLANES == 0
    assert lp % CHUNK == 0 and ls <= BASE_BLOCK and gdn_c == 3 * gdn_v and heads_a * dk_a == gdn_v
    assert gdn_v % ret_w == 0 and heads_r * dv_r == ret_w
    col_z, col_r = 0, gdn_v
    col_a = col_r + 4 * ret_w
    col_b = col_a + LANES

    rows_p = lambda a: a.reshape((tp,) + a.shape[2:])
    rows_s = lambda a: jnp.swapaxes(a, 0, 1).reshape((ts,) + a.shape[2:])
    join = lambda a, b: jnp.concatenate([rows_p(a), rows_s(b)], axis=0)
    seq_p = lambda a: a[:tp].reshape((bp, lp) + a.shape[1:])
    seq_s = lambda a: jnp.swapaxes(a[tp:].reshape((ls, bs) + a.shape[1:]), 0, 1)

    tps = lp // tm
    tiles = lambda a: a.reshape((t // tm, tm) + a.shape[1:])

    def hist_p(a, nrows):
        tl = tiles(a)[:ntp, tm - nrows:]
        prev = jnp.concatenate([jnp.zeros_like(tl[:1]), tl[:-1]], axis=0)
        first = (jnp.arange(ntp) % tps == 0)[:, None, None]
        return jnp.where(first, 0.0, prev)

    def tail_p(a, nrows):
        return tiles(a)[tps - 1:ntp:tps, tm - nrows:]

    def hist_s(state, nrows):
        return jnp.swapaxes(state[:, -nrows:], 0, 1).reshape(1, nrows * bs, state.shape[-1])

    def pad_seq(a):
        a = seq_s(a)
        a = jnp.pad(a, ((0, 0), (0, BASE_BLOCK - ls)) + ((0, 0),) * (a.ndim - 2))
        return a.reshape((bs * BASE_BLOCK,) + a.shape[2:])

    unpad_seq = lambda a: rows_s(a.reshape((bs, BASE_BLOCK) + a.shape[1:])[:, :ls])

    xr = join(x_prompt, x_sample)
    pj = [join(p_prompt[i], p_sample[i]).astype(BF16) for i in range(depth)]
    pool_w_bf = pool_w.astype(BF16)
    w_down_bf = ffn_w_down.astype(BF16)
    w_proj_bf = ple_w_proj.astype(BF16)
    w_in_t = jnp.swapaxes(w_in, 1, 2)
    rows_prompt = _pick(lp, 256, CHUNK)
    spb = max(1, CHUNK // BASE_BLOCK)
    assert bs % spb == 0

    new_gdn_conv_p, new_gdn_p, new_ret_p, new_pool_p, new_ffn_p = [], [], [], [], []
    new_gdn_conv_s, new_gdn_s, new_ret_s, new_pool_s, new_ffn_s = [], [], [], [], []
    for i in range(depth):
        j = i // 2
        if i % 2 == 0:
            (h,) = _rmsnorm(xr, norm_mix[i], [BF16])
            qkv, cs_p = _proj_conv(h, w_in_t, gdn_conv_w, layer=j, tm=lp, stride=1, t0=0, nt=bp, w_t=True)
            qkv, cs_s = _proj_conv(h, w_in_t, gdn_conv_w, layer=j, tm=tm, stride=bs, t0=ntp, nt=1, w_t=True,
                                   hist=hist_s(state_gdn_conv[j], gtaps - 1), prev=qkv)
            proj = _matmul(h, _pack_w_rest(w_in_t[j], gdn_c, heads_a, gdn_v), w_t=True, name="in_proj")
            cols = (col_z, col_a, col_b)
            mshape = (t, gdn_v + heads_r * dv_r)
            mixed, s_a_p = _gdn(qkv, proj, jnp.zeros((bp,) + state_gdn.shape[2:], F32), gdn_a_log[j],
                                gdn_dt_bias[j], gdn_out_norm[j], cols=cols, nseq=bp, seqlen=lp,
                                rows=rows_prompt, seg=CHUNK, valid=CHUNK, carry=True, out_shape=mshape)
            mixed, s_b_p = _retention(proj, jnp.zeros((bp,) + state_ret.shape[2:], F32), col_q=col_r,
                                      nseq=bp, seqlen=lp, rows=rows_prompt, seg=CHUNK, valid=CHUNK, pos0=0,
                                      carry=True, out_shape=mshape, out_block=gdn_v // ret_w, prev=mixed)
            proj_s = pad_seq(proj)
            o_gdn_s, s_a_s = _gdn(pad_seq(qkv), proj_s, state_gdn[j], gdn_a_log[j], gdn_dt_bias[j],
                                  gdn_out_norm[j], cols=cols, nseq=bs, seqlen=BASE_BLOCK,
                                  rows=spb * BASE_BLOCK, seg=BASE_BLOCK, valid=ls, carry=False)
            o_ret_s, s_b_s = _retention(proj_s, state_ret[j], col_q=col_r, nseq=bs, seqlen=BASE_BLOCK,
                                        rows=spb * BASE_BLOCK, seg=BASE_BLOCK, valid=ls, pos0=PAST_LEN,
                                        carry=False)
            mixed_s = jnp.concatenate([unpad_seq(o_gdn_s), unpad_seq(o_ret_s)], axis=1)
            mixed = lax.dynamic_update_slice(mixed, mixed_s, (tp, 0))
            xr, h, ss = _matmul(mixed, w_out, layer=j, res=xr, gain=norm_ffn[i], name="out_proj")
            new_gdn_conv_p.append(cs_p)
            new_gdn_conv_s.append(jnp.swapaxes(cs_s.reshape(gtaps - 1, bs, gdn_c), 0, 1))
            new_gdn_p.append(s_a_p)
            new_gdn_s.append(s_a_s)
            new_ret_p.append(s_b_p)
            new_ret_s.append(s_b_s)
        else:
            h32, = _rmsnorm(xr, norm_mix[i], [F32])
            h_s = seq_s(h32)
            pos = jnp.concatenate([jnp.tile(jnp.arange(lp, dtype=jnp.int32), bp),
                                   jnp.repeat(PAST_LEN + jnp.arange(ls, dtype=jnp.int32), bs)])
            cnt = jnp.stack([jnp.minimum(w, pos + 1).astype(F32) for w in POOL_WINDOWS])[:, :, None]
            zero_row = jnp.zeros((bs, 1, d), F32)
            kw = dict(tm=tm)
            xn = _pool_mixer(h32, hist_p(h32, POOL_HIST), cnt, pool_w_bf[j], pool_scale[j], xr, stride=1,
                             t0=0, nt=ntp, **kw)
            xr = _pool_mixer(h32, hist_s(jnp.concatenate([zero_row, state_pool[j]], axis=1), POOL_HIST), cnt,
                             pool_w_bf[j], pool_scale[j], xr, stride=bs, t0=ntp, nt=1, prev=xn, **kw)
            new_pool_p.append(tail_p(h32, pool_past))
            new_pool_s.append(jnp.concatenate([state_pool[j], h_s], axis=1)[:, -pool_past:])
            ((h,), ss) = _rmsnorm(xr, norm_ffn[i], [BF16]), None
        act, sg_p, sv_p = _ffn_up_act(h, ffn_w_up, ffn_conv_w, ffn_conv_b, layer=i, tm=lp, stride=1, t0=0, nt=bp,
                                      ss=ss)
        act, sg_s, sv_s = _ffn_up_act(h, ffn_w_up, ffn_conv_w, ffn_conv_b, layer=i, tm=tm, stride=bs, t0=ntp,
                                      nt=1, ss=ss, hist=hist_s(state_ffn_conv[i], ftaps - 1), prev=act)
        xr, h, ss = _matmul(act, w_down_bf, layer=i, res=xr, gain=norm_ple[i], name="ffn_down")
        new_ffn_p.append(jnp.concatenate([sg_p, sv_p], axis=-1))
        up_s = jnp.concatenate([sg_s, sv_s], axis=-1).reshape(ftaps - 1, bs, f2)
        new_ffn_s.append(jnp.swapaxes(up_s, 0, 1))
        xr = _matmul(h, ple_w_gate, layer=i, ss=ss, res=xr, p=pj[i], wp=w_proj_bf, name="ple")
    (out_p,) = _rmsnorm(xr, norm_final, [F32], 0, tp)
    (out_s,) = _rmsnorm(xr, norm_final, [F32], tp, ts)
    st = lambda lst: jnp.stack(lst)
    return (out_p.reshape(bp, lp, d), jnp.swapaxes(out_s.reshape(ls, bs, d), 0, 1),
            st(new_gdn_conv_p), st(new_gdn_p), st(new_ret_p), st(new_pool_p), st(new_ffn_p),
            st(new_gdn_conv_s), st(new_gdn_s), st(new_ret_s), st(new_pool_s), st(new_ffn_s))
```

```python
import functools
import math

import numpy as np
import jax
import jax.numpy as jnp
from jax import lax
from jax.experimental import pallas as pl
from jax.experimental.pallas import tpu as pltpu

F32 = jnp.float32
BF16 = jnp.bfloat16
HIGHEST = lax.Precision.HIGHEST

EPS = 1e-6
CHUNK = 64
PAST_LEN = 16384
ROPE_BASE = 10000.0
POOL_WINDOWS = (2, 4, 8, 16)
POOL_HIST = 16
LANES = 128
SUBLANES = 8
VMEM_LIMIT = 56 * 1024 * 1024
X_TILE_BYTES = 24 * 1024 * 1024
BASE_BLOCK = 8
RET_GROUP = 8
ROW_CHUNK = 256


def _pick(n, pref, align=LANES):
    if n <= pref:
        return n
    t = (pref // align) * align
    while t >= align:
        if n % t == 0:
            return t
        t -= align
    return n


def _cparams(*sem):
    return pltpu.CompilerParams(dimension_semantics=sem, vmem_limit_bytes=VMEM_LIMIT)


def _silu(x):
    return x * (1.0 / (1.0 + jnp.exp(-x)))


def _sigmoid(x):
    return 1.0 / (1.0 + jnp.exp(-x))


def _dot(a, b, **kw):
    return jnp.dot(a, b, preferred_element_type=F32, **kw)


def _dot_nt(a, b, **kw):
    return lax.dot_general(a, b, (((1,), (1,)), ((), ())), preferred_element_type=F32, **kw)


def _dot_tn(a, b, **kw):
    return lax.dot_general(a, b, (((0,), (0,)), ((), ())), preferred_element_type=F32, **kw)


def _mm(a, b):
    return _dot(a.astype(BF16), b.astype(BF16))


def _expand(x, e_bf, passes, left=False):
    acc = None
    for _ in range(passes):
        piece = x.astype(BF16)
        term = _dot(e_bf, piece) if left else _dot(piece, e_bf)
        acc = term if acc is None else acc + term
        x = x - piece.astype(F32)
    return acc


def _rmsnorm_kernel(x_ref, g_ref, *o_refs):
    x = x_ref[...]
    y = x * lax.rsqrt(jnp.mean(x * x, axis=-1, keepdims=True) + EPS) * g_ref[...]
    for o_ref in o_refs:
        o_ref[...] = y.astype(o_ref.dtype)


def _rmsnorm(x, gain, dtypes, row0=0, nrows=None):
    d = x.shape[1]
    m = x.shape[0] if nrows is None else nrows
    tm = _pick(math.gcd(m, row0) if row0 else m, 256, SUBLANES)
    t0 = row0 // tm
    return pl.pallas_call(
        _rmsnorm_kernel,
        grid=(m // tm,),
        in_specs=[pl.BlockSpec((tm, d), lambda i: (i + t0, 0)), pl.BlockSpec((1, d), lambda i: (0, 0))],
        out_specs=[pl.BlockSpec((tm, d), lambda i: (i, 0)) for _ in dtypes],
        out_shape=[jax.ShapeDtypeStruct((m, d), dt) for dt in dtypes],
        compiler_params=_cparams("parallel"),
        name="rmsnorm",
    )(x, gain.reshape(1, d))


def _rstd(ss_ref, width):
    return lax.rsqrt(jnp.sum(ss_ref[...], axis=-1, keepdims=True) * (1.0 / width) + EPS)


def _mm_kernel(*refs, epilogue, w_t, scaled, emit_norm):
    it = iter(refs)
    x_ref, w_ref = next(it), next(it)
    ss_in = next(it) if scaled else None
    res_ref = next(it) if epilogue in ("res", "ple") else None
    p_ref, wp_ref = (next(it), next(it)) if epilogue == "ple" else (None, None)
    gn_ref = next(it) if emit_norm else None
    o_ref = next(it)
    acc = (_dot_nt if w_t else _dot)(x_ref[...], w_ref[...].astype(BF16))
    if scaled:
        acc = acc * _rstd(ss_in, x_ref.shape[1])
    if epilogue == "res":
        acc = res_ref[...] + acc
    elif epilogue == "ple":
        acc = res_ref[...] + _sigmoid(acc) * _dot(p_ref[...], wp_ref[...])
    o_ref[...] = acc.astype(o_ref.dtype)
    if emit_norm:
        xb_ref, ss_ref = next(it), next(it)
        xb_ref[...] = (acc * gn_ref[...]).astype(BF16)

        @pl.when(pl.program_id(1) == 0)
        def _():
            ss_ref[...] = jnp.zeros(ss_ref.shape, F32)

        sq = acc * acc
        part = sq[:, :LANES]
        for k in range(1, sq.shape[1] // LANES):
            part = part + sq[:, k * LANES:(k + 1) * LANES]
        ss_ref[...] += part


def _wspec(w, layer, rows, tc, cmap, w_t=False):
    block, index = ((tc, rows), lambda j: (cmap(j), 0)) if w_t else ((rows, tc), lambda j: (0, cmap(j)))
    if w.ndim == 2:
        return pl.BlockSpec(block, lambda i, j: index(j))
    return pl.BlockSpec((None,) + block, lambda i, j: (layer,) + index(j))


def _matmul(x, w, *, layer=0, w_t=False, ss=None, res=None, p=None, wp=None, gain=None, tn=256,
            out_dtype=F32, name="matmul"):
    m, kdim = x.shape
    n = w.shape[-2] if w_t else w.shape[-1]
    tm = _pick(m, X_TILE_BYTES // (2 * kdim), 2 * SUBLANES)
    tn = _pick(n, tn)
    epilogue = "none" if res is None else ("res" if p is None else "ple")
    rows = lambda width: pl.BlockSpec((tm, width), lambda i, j: (i, 0), pipeline_mode=pl.Buffered(1))
    in_specs = [rows(kdim), _wspec(w, layer, kdim, tn, lambda j: j, w_t)]
    args = [x, w]
    if ss is not None:
        in_specs.append(rows(LANES))
        args.append(ss)
    if res is not None:
        in_specs.append(pl.BlockSpec((tm, tn), lambda i, j: (i, j)))
        args.append(res)
    if p is not None:
        in_specs += [rows(p.shape[1]), _wspec(wp, layer, p.shape[1], tn, lambda j: j)]
        args += [p, wp]
    tile = pl.BlockSpec((tm, tn), lambda i, j: (i, j))
    out_specs, out_shape = tile, jax.ShapeDtypeStruct((m, n), out_dtype)
    if gain is not None:
        in_specs.append(pl.BlockSpec((1, tn), lambda i, j: (0, j)))
        args.append(gain.reshape(1, n))
        out_specs = [tile, tile, pl.BlockSpec((tm, LANES), lambda i, j: (i, 0))]
        out_shape = [out_shape, jax.ShapeDtypeStruct((m, n), BF16), jax.ShapeDtypeStruct((m, LANES), F32)]
    return pl.pallas_call(
        functools.partial(_mm_kernel, epilogue=epilogue, w_t=w_t, scaled=ss is not None,
                          emit_norm=gain is not None),
        grid=(m // tm, n // tn),
        in_specs=in_specs,
        out_specs=out_specs,
        out_shape=out_shape,
        compiler_params=_cparams("parallel", "arbitrary"),
        name=name,
    )(*args)


def _row_chunk(tm, keep, pref):
    chunk = _pick(tm, pref, 2 * SUBLANES)
    return chunk if chunk >= keep else tm


def _causal_conv(cur, w_ref, hist, *, taps, stride):
    rows, hrows = cur.shape[0], hist.shape[0]
    ext = jnp.concatenate([hist, cur], axis=0)
    acc = w_ref[taps - 1:taps, :] * cur
    for j in range(taps - 1):
        back = (taps - 1 - j) * stride
        if back % SUBLANES == 0:
            piece = ext[hrows - back:hrows - back + rows, :]
        else:
            piece = pltpu.roll(ext, back, axis=0)[hrows:, :]
        acc = acc + w_ref[j:j + 1, :] * piece
    return acc


def _proj_conv_chunks(x_ref, ws, c_refs, h_refs, emit, *, taps, stride, tm, chunk, w_t=False, scale=None):
    keep = (taps - 1) * stride
    hrows = -(-keep // SUBLANES) * SUBLANES
    mm = _dot_nt if w_t else _dot
    tails = [jnp.zeros((hrows, w.shape[0 if w_t else 1]), F32) if h is None else h[...]
             for h, w in zip(h_refs, ws)]
    assert all(tl.shape[0] == hrows for tl in tails) and chunk >= hrows
    for c in range(tm // chunk):
        r = slice(c * chunk, (c + 1) * chunk)
        ups = [mm(x_ref[r, :], w) for w in ws]
        if scale is not None:
            ups = [u * scale[r, :] for u in ups]
        emit(r, [_causal_conv(u, cr, tl, taps=taps, stride=stride) for u, cr, tl in zip(ups, c_refs, tails)])
        tails = [u[chunk - hrows:, :] for u in ups]
    return [tl[hrows - keep:, :] for tl in tails]


def _ffn_up_kernel(*refs, taps, stride, tm, chunk, scaled, has_hist, aliased):
    x_ref, wg_ref, wv_ref, cg_ref, cv_ref, bg_ref, bv_ref = refs[:7]
    scale = _rstd(refs[7], x_ref.shape[1]) if scaled else None
    h_refs = (refs[7 + scaled], refs[8 + scaled]) if has_hist else (None, None)
    o_ref, sg_ref, sv_ref = refs[7 + scaled + 2 * has_hist + aliased:]

    tc = wg_ref.shape[1]
    side = lambda a, b: jnp.concatenate([a[...], b[...]], axis=1)
    bias = side(bg_ref, bv_ref)

    def emit(r, convs):
        y = convs[0] + bias
        o_ref[r, :] = (_silu(y[:, :tc]) * y[:, tc:]).astype(o_ref.dtype)

    w = side(wg_ref, wv_ref).astype(BF16)
    hist = side(*h_refs) if has_hist else None
    (tail,) = _proj_conv_chunks(x_ref, [w], (side(cg_ref, cv_ref),), (hist,), emit, taps=taps,
                                stride=stride, tm=tm, chunk=chunk, scale=scale)
    sg_ref[...] = tail[:, :tc]
    sv_ref[...] = tail[:, tc:]


def _ffn_up_act(h, w_up, conv_w, conv_b, *, layer, tm, stride, t0, nt, ss=None, hist=None, prev=None):
    t, kdim = h.shape
    f2 = w_up.shape[-1]
    f = f2 // 2
    taps = conv_w.shape[1]
    keep = (taps - 1) * stride
    tc = _pick(f, 256)
    nc = f // tc
    has_hist = hist is not None
    aliased = prev is not None
    kern = functools.partial(_ffn_up_kernel, taps=taps, stride=stride, tm=tm, chunk=_row_chunk(tm, keep, tm),
                             scaled=ss is not None, has_hist=has_hist, aliased=aliased)
    lo = lambda j: j
    hi = lambda j: j + nc
    conv_b = conv_b.reshape(conv_b.shape[0], 1, f2)
    in_specs = [
        pl.BlockSpec((tm, kdim), lambda i, j: (i + t0, 0), pipeline_mode=pl.Buffered(1)),
        _wspec(w_up, layer, kdim, tc, lo), _wspec(w_up, layer, kdim, tc, hi),
        _wspec(conv_w, layer, taps, tc, lo), _wspec(conv_w, layer, taps, tc, hi),
        _wspec(conv_b, layer, 1, tc, lo), _wspec(conv_b, layer, 1, tc, hi),
    ]
    args = [h, w_up, w_up, conv_w, conv_w, conv_b, conv_b]
    if ss is not None:
        in_specs.append(pl.BlockSpec((tm, LANES), lambda i, j: (i + t0, 0), pipeline_mode=pl.Buffered(1)))
        args.append(ss)
    if has_hist:
        in_specs += [pl.BlockSpec((None, keep, tc), lambda i, j: (i, 0, j)),
                     pl.BlockSpec((None, keep, tc), lambda i, j: (i, 0, j + nc))]
        args += [hist, hist]
    if aliased:
        in_specs.append(pl.BlockSpec(memory_space=pl.ANY))
        args.append(prev)
    sspec = pl.BlockSpec((None, keep, tc), lambda i, j: (i, 0, j))
    return pl.pallas_call(
        kern,
        grid=(nt, nc),
        in_specs=in_specs,
        out_specs=[pl.BlockSpec((tm, tc), lambda i, j: (i + t0, j)), sspec, sspec],
        out_shape=[jax.ShapeDtypeStruct((t, f), BF16), jax.ShapeDtypeStruct((nt, keep, f), F32),
                   jax.ShapeDtypeStruct((nt, keep, f), F32)],
        input_output_aliases={len(args) - 1: 0} if aliased else {},
        compiler_params=_cparams("parallel", "arbitrary"),
        name="ffn_up_conv_act",
    )(*args)


def _proj_conv_kernel(*refs, taps, stride, tm, chunk, has_hist, aliased, w_t):
    x_ref, w_ref, c_ref = refs[:3]
    h_ref = refs[3] if has_hist else None
    o_ref, s_ref = refs[3 + has_hist + aliased:]

    def emit(r, convs):
        o_ref[r, :] = _silu(convs[0])

    (s_ref[...],) = _proj_conv_chunks(x_ref, [w_ref[...].astype(BF16)], (c_ref,), (h_ref,), emit, taps=taps,
                                      stride=stride, tm=tm, chunk=chunk, w_t=w_t)


def _proj_conv(h, w, conv_w, *, layer, tm, stride, t0, nt, w_t=False, hist=None, prev=None):
    t, kdim = h.shape
    taps, c = conv_w.shape[-2:]
    keep = (taps - 1) * stride
    tc = _pick(c, 256)
    has_hist = hist is not None
    aliased = prev is not None
    kern = functools.partial(_proj_conv_kernel, taps=taps, stride=stride, tm=tm,
                             chunk=_row_chunk(tm, keep, tm),
                             has_hist=has_hist, aliased=aliased, w_t=w_t)
    same = lambda j: j
    in_specs = [pl.BlockSpec((tm, kdim), lambda i, j: (i + t0, 0), pipeline_mode=pl.Buffered(1)),
                _wspec(w, layer, kdim, tc, same, w_t), _wspec(conv_w, layer, taps, tc, same)]
    args = [h, w, conv_w]
    if has_hist:
        in_specs.append(pl.BlockSpec((None, keep, tc), lambda i, j: (i, 0, j)))
        args.append(hist)
    if aliased:
        in_specs.append(pl.BlockSpec(memory_space=pl.ANY))
        args.append(prev)
    return pl.pallas_call(
        kern,
        grid=(nt, c // tc),
        in_specs=in_specs,
        out_specs=[pl.BlockSpec((tm, tc), lambda i, j: (i + t0, j)),
                   pl.BlockSpec((None, keep, tc), lambda i, j: (i, 0, j))],
        out_shape=[jax.ShapeDtypeStruct((t, c), F32), jax.ShapeDtypeStruct((nt, keep, c), F32)],
        input_output_aliases={len(args) - 1: 0} if aliased else {},
        compiler_params=_cparams("parallel", "arbitrary"),
        name="qkv_proj_conv",
    )(*args)


def _pool_kernel(*refs, stride, tm, aliased):
    h_ref, hist_ref, cnt_ref, w_ref, sc_ref, x_ref = refs[:6]
    o_ref = refs[-1]
    g = pl.program_id(1)
    off = POOL_HIST * stride
    cur = h_ref[...]
    run = jnp.concatenate([hist_ref[...], cur], axis=0)
    for k, width in enumerate(POOL_WINDOWS):
        back = (width // 2) * stride
        if back % SUBLANES:
            shifted = pltpu.roll(run, back, axis=0)
        else:
            shifted = jnp.concatenate([run[:back, :], run[:run.shape[0] - back, :]], axis=0)
        run = run + shifted if k == 0 else jnp.where(g >= k, run + shifted, run)
    pooled = run[off:, :] / cnt_ref[...] - cur
    y = _dot(pooled.astype(BF16), w_ref[...]) * sc_ref[...]
    o_ref[...] = x_ref[...] + y


def _pool_mixer(h, hist, cnt, pool_w, scale, x, *, tm, stride, t0, nt, prev=None):
    t, d = h.shape
    g = pool_w.shape[0]
    dg = d // g
    hr = POOL_HIST * stride
    aliased = prev is not None
    kern = functools.partial(_pool_kernel, stride=stride, tm=tm, aliased=aliased)
    in_specs = [
        pl.BlockSpec((tm, dg), lambda i, j: (i + t0, j)),
        pl.BlockSpec((None, hr, dg), lambda i, j: (i, 0, j)),
        pl.BlockSpec((None, tm, 1), lambda i, j: (j, i + t0, 0)),
        pl.BlockSpec((None, dg, dg), lambda i, j: (j, 0, 0)),
        pl.BlockSpec((1, dg), lambda i, j: (0, j)),
        pl.BlockSpec((tm, dg), lambda i, j: (i + t0, j)),
    ]
    args = [h, hist, cnt, pool_w, scale.reshape(1, d), x]
    if aliased:
        in_specs.append(pl.BlockSpec(memory_space=pl.ANY))
        args.append(prev)
    return pl.pallas_call(
        kern,
        grid=(nt, g),
        in_specs=in_specs,
        out_specs=pl.BlockSpec((tm, dg), lambda i, j: (i + t0, j)),
        out_shape=jax.ShapeDtypeStruct((t, d), F32),
        input_output_aliases={6: 0} if aliased else {},
        compiler_params=_cparams("parallel", "arbitrary"),
        name="pool_mixer",
    )(*args)


def _gdn_kernel(*refs, heads, dk, rows, seg, valid, carry, unroll, aliased):
    q_ref, k_ref, v_ref, pa_ref, pb_ref, z_ref, s0_ref, alog_ref, dtb_ref, on_ref, ex_ref = refs[:11]
    o_ref, sn_ref, gcb_s, bb_s, gct_s = refs[11 + aliased:16 + aliased]
    s_s = refs[16 + aliased] if carry else None
    dv = dk
    nseg = rows // seg
    sh = int(math.log2(seg))
    assert 1 << sh == seg and seg % BASE_BLOCK == 0

    if carry:
        @pl.when(pl.program_id(1) == 0)
        def _():
            s_s[...] = s0_ref[...]

    rid = lax.broadcasted_iota(jnp.int32, (rows, 1), 0)
    live = jnp.bitwise_and(rid, seg - 1) < valid
    rseg = lax.shift_right_logical(rid, sh)
    ri = lax.broadcasted_iota(jnp.int32, (rows, rows), 0)
    ci = lax.broadcasted_iota(jnp.int32, (rows, rows), 1)
    same = lax.shift_right_logical(ri, sh) == lax.shift_right_logical(ci, sh)
    causal = jnp.logical_and(same, ci <= ri)
    strict = jnp.logical_and(same, ci < ri)
    base = lax.shift_right_logical(ri, 3) == lax.shift_right_logical(ci, 3)
    eye = (ri == ci).astype(F32)
    levels = []
    s = BASE_BLOCK
    while s < seg:
        rb = lax.shift_right_logical(ri, int(math.log2(s)))
        cb = lax.shift_right_logical(ci, int(math.log2(s)))
        levels.append(jnp.logical_and(jnp.bitwise_and(rb, 1) == 1, cb == rb - 1))
        s *= 2

    a = pa_ref[...] + dtb_ref[...]
    softplus = jnp.maximum(a, 0.0) + jnp.log1p(jnp.exp(-jnp.abs(a)))
    glog = jnp.where(live, -jnp.exp(alog_ref[...]) * softplus, 0.0)
    beta = jnp.where(live, _sigmoid(pb_ref[...]), 0.0)
    gc = _expand(glog, causal.astype(BF16), 3, left=True)
    gcb_s[...] = _expand(gc, ex_ref[...], 3)
    bb_s[...] = _expand(beta, ex_ref[...], 2)
    gct_s[...] = gc.T

    def widen(x):
        if rows <= dk:
            return x[:, :rows]
        return jnp.concatenate([x] * (rows // dk), axis=1)

    def group(gi, carry_unused):
        hs = [gi * unroll + j for j in range(unroll)]
        los = [pl.multiple_of(h * dk, dk) for h in hs]
        each = range(unroll)
        qs = [q_ref[:, pl.ds(lo, dk)] for lo in los]
        ks = [k_ref[:, pl.ds(lo, dk)] for lo in los]
        vs = [v_ref[:, pl.ds(lo, dk)] for lo in los]
        zs = [z_ref[:, pl.ds(lo, dv)] for lo in los]
        gcbs = [gcb_s[:, pl.ds(lo, dk)] for lo in los]
        bbs = [bb_s[:, pl.ds(lo, dk)] for lo in los]
        grow = [gct_s[pl.ds(h, 1), :] for h in hs]
        if carry:
            sts = [s_s[h] for h in hs]
        qs = [q * lax.rsqrt(jnp.sum(q * q, axis=-1, keepdims=True) + 1e-6) * (dk ** -0.5) for q in qs]
        ks = [k * lax.rsqrt(jnp.sum(k * k, axis=-1, keepdims=True) + 1e-6) for k in ks]
        egcs = [jnp.exp(g) for g in gcbs]
        decays = [jnp.where(causal, jnp.exp(jnp.where(causal, widen(gcbs[j]) - grow[j], 0.0)), 0.0)
                  for j in each]
        kbf = [k.astype(BF16) for k in ks]
        qkks = [_dot_nt(jnp.concatenate([qs[j].astype(BF16), kbf[j]], axis=0), kbf[j]) for j in each]
        qks = [qkks[j][:rows] * decays[j] for j in each]
        amats = [jnp.where(strict, widen(bbs[j]) * decays[j] * qkks[j][rows:], 0.0) for j in each]
        pws = [-jnp.where(base, a, 0.0) for a in amats]
        invs = [eye + p for p in pws]
        for _ in range(2):
            pws = [_mm(p, p) for p in pws]
            invs = [invs[j] + _mm(pws[j], invs[j]) for j in each]
        for lvl in levels:
            lows = [_mm(jnp.where(lvl, amats[j], 0.0), invs[j]) for j in each]
            invs = [invs[j] - _mm(invs[j], lows[j]) for j in each]
        sols = [_mm(invs[j], jnp.concatenate([bbs[j] * vs[j], bbs[j] * egcs[j] * ks[j]], axis=1))
                for j in each]
        wvs = [sl[:, :dv] for sl in sols]
        wks = [sl[:, dv:] for sl in sols]
        qds = [qs[j] * egcs[j] for j in each]
        if carry:
            us = [[] for _ in each]
            os_ = [[] for _ in each]
            for sg in range(nseg):
                r = slice(sg * seg, (sg + 1) * seg)
                last = slice((sg + 1) * seg - 1, (sg + 1) * seg)
                both = [_dot(jnp.concatenate([wks[j][r], qds[j][r]], axis=0).astype(BF16), sts[j].astype(BF16))
                        for j in each]
                u = [wvs[j][r] - both[j][:seg] for j in each]
                kend = [ks[j][r] * jnp.exp(gcbs[j][last] - gcbs[j][r]) for j in each]
                sts = [jnp.exp(gcbs[j][last]) * sts[j] + _dot_tn(kend[j].astype(BF16), u[j].astype(BF16))
                       for j in each]
                for j in each:
                    us[j].append(u[j])
                    os_[j].append(both[j][seg:])
            u_all = [x[0] if nseg == 1 else jnp.concatenate(x, axis=0) for x in us]
            o = [x[0] if nseg == 1 else jnp.concatenate(x, axis=0) for x in os_]
        else:
            lhs = [jnp.concatenate([wks[j], qds[j]], axis=0).astype(BF16) for j in each]
            u_all = [jnp.zeros((rows, dv), F32) for _ in each]
            o = [jnp.zeros((rows, dv), F32) for _ in each]
            news = []
            for sg in range(nseg):
                last = slice((sg + 1) * seg - 1, (sg + 1) * seg)
                mine = rseg == sg
                st0 = [s0_ref[sg, h] for h in hs]
                both = [_dot(lhs[j], st0[j].astype(BF16)) for j in each]
                u_m = [jnp.where(mine, wvs[j] - both[j][:rows], 0.0) for j in each]
                u_all = [u_all[j] + u_m[j] for j in each]
                o = [o[j] + jnp.where(mine, both[j][rows:], 0.0) for j in each]
                kend = [jnp.where(mine, ks[j] * jnp.exp(jnp.where(mine, gcbs[j][last] - gcbs[j], 0.0)), 0.0)
                        for j in each]
                news.append([jnp.exp(gcbs[j][last]) * st0[j] + _dot_tn(kend[j].astype(BF16), u_m[j].astype(BF16))
                             for j in each])
        o = [o[j] + _mm(qks[j], u_all[j]) for j in each]
        o = [o[j] * lax.rsqrt(jnp.mean(o[j] * o[j], axis=-1, keepdims=True) + EPS) * on_ref[...] * _silu(zs[j])
             for j in each]
        for j in each:
            o_ref[:, pl.ds(los[j], dv)] = o[j].astype(o_ref.dtype)
            if carry:
                s_s[hs[j]] = sts[j]
            else:
                for sg in range(nseg):
                    sn_ref[sg, hs[j]] = news[sg][j]
        return carry_unused

    assert heads % unroll == 0
    lax.fori_loop(0, heads // unroll, group, 0)

    if carry:
        @pl.when(pl.program_id(1) == pl.num_programs(1) - 1)
        def _():
            sn_ref[...] = s_s[...]


def _gdn(qkv, proj, s0, a_log, dt_bias, out_norm, *, cols, nseq, seqlen, rows, seg, valid, carry,
         out_shape=None, out_block=0, prev=None):
    _, heads, dk, dv = s0.shape
    assert dk == LANES and dv == LANES and heads <= LANES and rows % seg == 0
    hk = heads * dk
    expand = np.zeros((LANES, hk), np.float32)
    for h in range(heads):
        expand[h, h * dk:(h + 1) * dk] = 1.0
    lane_pad = lambda t: jnp.pad(t.reshape(1, heads).astype(F32), ((0, 0), (0, LANES - heads)))
    col_z, col_a, col_b = cols
    assert col_z % hk == 0 and col_a % LANES == 0 and col_b % LANES == 0
    if carry:
        nblk = seqlen // rows
        grid = (nseq, nblk)
        rmap = lambda i, c: i * nblk + c
        smap = lambda i, c: (i, 0, 0, 0)
        sblock = (None, heads, dk, dv)
        sem = ("parallel", "arbitrary")
    else:
        spb = rows // seg
        grid = (nseq // spb, 1)
        rmap = lambda i, c: i
        smap = lambda i, c: (i, 0, 0, 0)
        sblock = (spb, heads, dk, dv)
        sem = ("parallel", "arbitrary")
    rspec = lambda w, cb: pl.BlockSpec((rows, w), lambda i, c: (rmap(i, c), cb))
    full = lambda shape: pl.BlockSpec(shape, lambda i, c: (0,) * len(shape))
    aliased = prev is not None
    kern = functools.partial(_gdn_kernel, heads=heads, dk=dk, rows=rows, seg=seg, valid=valid, carry=carry,
                             unroll=min(heads, 16), aliased=aliased)
    scratch = [pltpu.VMEM((rows, hk), F32), pltpu.VMEM((rows, hk), F32), pltpu.VMEM((LANES, rows), F32)]
    if carry:
        scratch.append(pltpu.VMEM((heads, dk, dv), F32))
    if out_shape is None:
        out_shape = (nseq * seqlen, heads * dv)
    in_specs = [
        rspec(hk, 0), rspec(hk, 1), rspec(hk, 2),
        rspec(LANES, col_a // LANES), rspec(LANES, col_b // LANES), rspec(hk, col_z // hk),
        pl.BlockSpec(sblock, smap),
        full((1, LANES)), full((1, LANES)), full((1, dv)), full((LANES, hk)),
    ]
    args = [qkv, qkv, qkv, proj, proj, proj, s0, lane_pad(a_log), lane_pad(dt_bias),
            out_norm.reshape(1, dv).astype(F32), jnp.asarray(expand, dtype=BF16)]
    if aliased:
        in_specs.append(pl.BlockSpec(memory_space=pl.ANY))
        args.append(prev)
    return pl.pallas_call(
        kern,
        grid=grid,
        in_specs=in_specs,
        out_specs=[rspec(hk, out_block), pl.BlockSpec(sblock, smap)],
        out_shape=[jax.ShapeDtypeStruct(out_shape, BF16), jax.ShapeDtypeStruct(s0.shape, F32)],
        scratch_shapes=scratch,
        input_output_aliases={len(args) - 1: 0} if aliased else {},
        compiler_params=_cparams(*sem),
        name="gated_deltanet",
    )(*args)


def _ret_kernel(*refs, hps, dk, rows, seg, carry, aliased):
    q_ref, k_ref, v_ref, g_ref, cos_ref, sin_ref, s0_ref, dec_ref, qs_ref, ks_ref, ge_ref = refs[:11]
    o_ref, sn_ref = refs[11 + aliased:13 + aliased]
    s_s = refs[13 + aliased] if carry else None
    dv = dk
    half = dk // 2
    nseg = rows // seg
    sh = int(math.log2(seg))
    hstep = pl.program_id(1) if not carry else 0

    if carry:
        @pl.when(pl.program_id(1) == 0)
        def _():
            s_s[...] = s0_ref[...]

    cos = cos_ref[...]
    sin = sin_ref[...]
    rseg = lax.shift_right_logical(lax.broadcasted_iota(jnp.int32, (rows, 1), 0), sh)

    def rot(x):
        x1, x2 = x[:, :half], x[:, half:]
        return jnp.concatenate([x1 * cos - x2 * sin, x1 * sin + x2 * cos], axis=1)

    gsz = min(hps, RET_GROUP)
    assert hps % gsz == 0

    def group(gi, carry_unused):
        hhs = [gi * gsz + u for u in range(gsz)]
        hs = [hstep * hps + hh for hh in hhs]
        los = [pl.multiple_of(h * dk, dk) for h in hs]
        each = range(gsz)
        qs = [rot(q_ref[:, pl.ds(lo, dk)]) for lo in los]
        ks = [rot(k_ref[:, pl.ds(lo, dk)]) * (dk ** -0.5) for lo in los]
        vbf = [v_ref[:, pl.ds(lo, dv)].astype(BF16) for lo in los]
        gates = [g_ref[:, pl.ds(lo, dv)] for lo in los]
        ges = [ge_ref[h] for h in hs]
        qks = [_dot_nt(qs[u].astype(BF16), ks[u].astype(BF16)) * dec_ref[hs[u]] for u in each]
        qds = [(qs[u] * qs_ref[:, pl.ds(los[u], dk)]).astype(BF16) for u in each]
        kes = [ks[u] * ks_ref[:, pl.ds(los[u], dk)] for u in each]
        if carry:
            sts = [s_s[h] for h in hs]
            parts = [[] for _ in each]
            for sg in range(nseg):
                r = slice(sg * seg, (sg + 1) * seg)
                for u in each:
                    parts[u].append(_dot(qds[u][r], sts[u].astype(BF16)))
                sts = [ges[u] * sts[u] + _dot_tn(kes[u][r].astype(BF16), vbf[u][r]) for u in each]
            o = [x[0] if nseg == 1 else jnp.concatenate(x, axis=0) for x in parts]
        else:
            o = [jnp.zeros((rows, dv), F32) for _ in each]
            news = []
            for sg in range(nseg):
                mine = rseg == sg
                st0 = [s0_ref[sg, hh] for hh in hhs]
                o = [o[u] + jnp.where(mine, _dot(qds[u], st0[u].astype(BF16)), 0.0) for u in each]
                news.append([ges[u] * st0[u] + _dot_tn(jnp.where(mine, kes[u], 0.0).astype(BF16), vbf[u])
                             for u in each])
        o = [o[u] + _dot(qks[u].astype(BF16), vbf[u]) for u in each]
        mus = [jnp.mean(x, axis=-1, keepdims=True) for x in o]
        vrs = [jnp.mean(jnp.square(o[u] - mus[u]), axis=-1, keepdims=True) for u in each]
        o = [(o[u] - mus[u]) * lax.rsqrt(vrs[u] + EPS) * _silu(gates[u]) for u in each]
        for u in each:
            o_ref[:, pl.ds(los[u], dv)] = o[u].astype(o_ref.dtype)
            if carry:
                s_s[hs[u]] = sts[u]
            else:
                for sg in range(nseg):
                    sn_ref[sg, hhs[u]] = news[sg][u]
        return carry_unused

    lax.fori_loop(0, hps // gsz, group, 0)

    if carry:
        @pl.when(pl.program_id(1) == pl.num_programs(1) - 1)
        def _():
            sn_ref[...] = s_s[...]


def _retention(proj, s0, *, col_q, nseq, seqlen, rows, seg, valid, pos0, carry,
               out_shape=None, out_block=0, prev=None):
    _, heads, dk, dv = s0.shape
    assert dk == dv and rows % seg == 0
    hw = heads * dk
    half = dk // 2
    assert col_q % hw == 0
    cq = col_q // hw
    npos = seqlen if carry else rows
    offs = np.arange(npos) if carry else np.arange(rows) % seg
    inv_freq = ROPE_BASE ** (-np.arange(half, dtype=np.float64) / half)
    ang = (pos0 + offs.astype(np.float64))[:, None] * inv_freq[None, :]
    cos, sin = np.cos(ang).astype(np.float32), np.sin(ang).astype(np.float32)
    log_gamma = np.log1p(-np.exp2(-5.0 - np.arange(heads, dtype=np.float64)))
    pin = np.arange(rows) % seg
    steps = np.minimum(pin + 1, valid).astype(np.float64)
    gc = log_gamma[:, None] * steps[None, :]
    gl = log_gamma * min(seg, valid)
    sameseg = (np.arange(rows)[:, None] // seg) == (np.arange(rows)[None, :] // seg)
    causal = np.logical_and(sameseg, np.tril(np.ones((rows, rows), bool)))
    diff = gc[:, :, None] - gc[:, None, :]
    dec = np.where(causal[None], np.exp(np.where(causal[None], diff, 0.0)), 0.0).astype(np.float32)
    qs = np.repeat(np.exp(gc).T, dk, axis=1).astype(np.float32)
    ks = np.repeat(np.exp(gl[:, None] - gc).T, dk, axis=1).astype(np.float32)
    ge = np.broadcast_to(np.exp(gl)[:, None, None], (heads, 1, dv)).astype(np.float32)
    if carry:
        nblk = seqlen // rows
        hps = heads
        grid = (nseq, nblk)
        rmap = lambda i, c: i * nblk + c
        tmap = lambda i, c: (c, 0)
        smap = lambda i, c: (i, 0, 0, 0)
        sblock = (None, heads, dk, dv)
    else:
        spb = rows // seg
        hps = max(1, heads // 2)
        grid = (nseq // spb, heads // hps)
        rmap = lambda i, c: i
        tmap = lambda i, c: (0, 0)
        smap = lambda i, c: (i, c, 0, 0)
        sblock = (spb, hps, dk, dv)
    rspec = lambda cb: pl.BlockSpec((rows, hw), lambda i, c: (rmap(i, c), cb))
    full = lambda shape: pl.BlockSpec(shape, lambda i, c: (0,) * len(shape))
    aliased = prev is not None
    kern = functools.partial(_ret_kernel, hps=hps, dk=dk, rows=rows, seg=seg, carry=carry, aliased=aliased)
    if out_shape is None:
        out_shape = (nseq * seqlen, heads * dv)
    in_specs = [
        rspec(cq), rspec(cq + 1), rspec(cq + 2), rspec(cq + 3),
        pl.BlockSpec((rows, half), tmap), pl.BlockSpec((rows, half), tmap),
        pl.BlockSpec(sblock, smap),
        full((heads, rows, rows)), full((rows, hw)), full((rows, hw)), full((heads, 1, dv)),
    ]
    args = [proj, proj, proj, proj, jnp.asarray(cos), jnp.asarray(sin), s0, jnp.asarray(dec), jnp.asarray(qs),
            jnp.asarray(ks), jnp.asarray(ge)]
    if aliased:
        in_specs.append(pl.BlockSpec(memory_space=pl.ANY))
        args.append(prev)
    return pl.pallas_call(
        kern,
        grid=grid,
        in_specs=in_specs,
        out_specs=[rspec(out_block), pl.BlockSpec(sblock, smap)],
        out_shape=[jax.ShapeDtypeStruct(out_shape, BF16), jax.ShapeDtypeStruct(s0.shape, F32)],
        scratch_shapes=[pltpu.VMEM((heads, dk, dv), F32)] if carry else [],
        input_output_aliases={len(args) - 1: 0} if aliased else {},
        compiler_params=_cparams("parallel", "arbitrary"),
        name="retention",
    )(*args)


def _pack_w_rest(wt, gdn_c, heads_a, gdn_v):
    off_a = gdn_c
    off_b = off_a + heads_a
    off_z = off_b + heads_a
    off_r = off_z + gdn_v
    zpad = jnp.zeros((LANES - heads_a, wt.shape[1]), wt.dtype)
    return jnp.concatenate([wt[off_z:off_r], wt[off_r:], wt[off_a:off_b], zpad, wt[off_b:off_z], zpad], axis=0)


def kernel(x_prompt, x_sample, state_gdn_conv, state_gdn, state_ret, state_pool, state_ffn_conv, p_prompt, p_sample, norm_mix, norm_ffn, norm_ple, norm_final, w_in, gdn_conv_w, gdn_a_log, gdn_dt_bias, gdn_out_norm, w_out, pool_w, pool_scale, ffn_w_up, ffn_conv_w, ffn_conv_b, ffn_w_down, ple_w_gate, ple_w_proj):
    bp, lp, d = x_prompt.shape
    bs, ls, _ = x_sample.shape
    depth = p_prompt.shape[0]
    tp, ts = bp * lp, bs * ls
    t = tp + ts
    tm = ts
    assert lp % tm == 0 and tm % SUBLANES == 0
    ntp = tp // tm
    _, _, heads_a, dk_a, dv_a = state_gdn.shape
    _, _, heads_r, dk_r, dv_r = state_ret.shape
    gdn_c = state_gdn_conv.shape[-1]
    gdn_v = heads_a * dv_a
    ret_w = heads_r * dk_r
    gtaps = gdn_conv_w.shape[1]
    ftaps = ffn_conv_w.shape[1]
    pool_past = state_pool.shape[2]
    f2 = ffn_w_up.shape[-1]
    assert min(lp, ls) >= max(gtaps, ftaps) - 1 and pool_past == POOL_HIST - 1 and lp % SUBLANES == 0
    assert lp % CHUNK == 0 and ls <= BASE_BLOCK and gdn_c == 3 * gdn_v and heads_a * dk_a == gdn_v
    assert gdn_v % ret_w == 0 and heads_r * dv_r == ret_w
    col_z, col_r = 0, gdn_v
    col_a = col_r + 4 * ret_w
    col_b = col_a + LANES

    rows_p = lambda a: a.reshape((tp,) + a.shape[2:])
    rows_s = lambda a: jnp.swapaxes(a, 0, 1).reshape((ts,) + a.shape[2:])
    join = lambda a, b: jnp.concatenate([rows_p(a), rows_s(b)], axis=0)
    seq_p = lambda a: a[:tp].reshape((bp, lp) + a.shape[1:])
    seq_s = lambda a: jnp.swapaxes(a[tp:].reshape((ls, bs) + a.shape[1:]), 0, 1)

    tps = lp // tm
    tiles = lambda a: a.reshape((t // tm, tm) + a.shape[1:])

    def hist_p(a, nrows):
        tl = tiles(a)[:ntp, tm - nrows:]
        prev = jnp.concatenate([jnp.zeros_like(tl[:1]), tl[:-1]], axis=0)
        first = (jnp.arange(ntp) % tps == 0)[:, None, None]
        return jnp.where(first, 0.0, prev)

    def tail_p(a, nrows):
        return tiles(a)[tps - 1:ntp:tps, tm - nrows:]

    def hist_s(state, nrows):
        return jnp.swapaxes(state[:, -nrows:], 0, 1).reshape(1, nrows * bs, state.shape[-1])

    def pad_seq(a):
        a = seq_s(a)
        a = jnp.pad(a, ((0, 0), (0, BASE_BLOCK - ls)) + ((0, 0),) * (a.ndim - 2))
        return a.reshape((bs * BASE_BLOCK,) + a.shape[2:])

    unpad_seq = lambda a: rows_s(a.reshape((bs, BASE_BLOCK) + a.shape[1:])[:, :ls])

    xr = join(x_prompt, x_sample)
    pj = [join(p_prompt[i], p_sample[i]).astype(BF16) for i in range(depth)]
    pool_w_bf = pool_w.astype(BF16)
    w_down_bf = ffn_w_down.astype(BF16)
    w_proj_bf = ple_w_proj.astype(BF16)
    w_in_t = jnp.swapaxes(w_in, 1, 2)
    rows_prompt = _pick(lp, 256, CHUNK)
    spb = max(1, CHUNK // BASE_BLOCK)
    assert bs % spb == 0

    new_gdn_conv_p, new_gdn_p, new_ret_p, new_pool_p, new_ffn_p = [], [], [], [], []
    new_gdn_conv_s, new_gdn_s, new_ret_s, new_pool_s, new_ffn_s = [], [], [], [], []
    for i in range(depth):
        j = i // 2
        if i % 2 == 0:
            (h,) = _rmsnorm(xr, norm_mix[i], [BF16])
            qkv, cs_p = _proj_conv(h, w_in_t, gdn_conv_w, layer=j, tm=lp, stride=1, t0=0, nt=bp, w_t=True)
            qkv, cs_s = _proj_conv(h, w_in_t, gdn_conv_w, layer=j, tm=tm, stride=bs, t0=ntp, nt=1, w_t=True,
                                   hist=hist_s(state_gdn_conv[j], gtaps - 1), prev=qkv)
            proj = _matmul(h, _pack_w_rest(w_in_t[j], gdn_c, heads_a, gdn_v), w_t=True, name="in_proj")
            cols = (col_z, col_a, col_b)
            mshape = (t, gdn_v + heads_r * dv_r)
            mixed, s_a_p = _gdn(qkv, proj, jnp.zeros((bp,) + state_gdn.shape[2:], F32), gdn_a_log[j],
                                gdn_dt_bias[j], gdn_out_norm[j], cols=cols, nseq=bp, seqlen=lp,
                                rows=rows_prompt, seg=CHUNK, valid=CHUNK, carry=True, out_shape=mshape)
            mixed, s_b_p = _retention(proj, jnp.zeros((bp,) + state_ret.shape[2:], F32), col_q=col_r,
                                      nseq=bp, seqlen=lp, rows=rows_prompt, seg=CHUNK, valid=CHUNK, pos0=0,
                                      carry=True, out_shape=mshape, out_block=gdn_v // ret_w, prev=mixed)
            proj_s = pad_seq(proj)
            o_gdn_s, s_a_s = _gdn(pad_seq(qkv), proj_s, state_gdn[j], gdn_a_log[j], gdn_dt_bias[j],
                                  gdn_out_norm[j], cols=cols, nseq=bs, seqlen=BASE_BLOCK,
                                  rows=spb * BASE_BLOCK, seg=BASE_BLOCK, valid=ls, carry=False)
            o_ret_s, s_b_s = _retention(proj_s, state_ret[j], col_q=col_r, nseq=bs, seqlen=BASE_BLOCK,
                                        rows=spb * BASE_BLOCK, seg=BASE_BLOCK, valid=ls, pos0=PAST_LEN,
                                        carry=False)
            mixed_s = jnp.concatenate([unpad_seq(o_gdn_s), unpad_seq(o_ret_s)], axis=1)
            mixed = lax.dynamic_update_slice(mixed, mixed_s, (tp, 0))
            xr, h, ss = _matmul(mixed, w_out, layer=j, res=xr, gain=norm_ffn[i], name="out_proj")
            new_gdn_conv_p.append(cs_p)
            new_gdn_conv_s.append(jnp.swapaxes(cs_s.reshape(gtaps - 1, bs, gdn_c), 0, 1))
            new_gdn_p.append(s_a_p)
            new_gdn_s.append(s_a_s)
            new_ret_p.append(s_b_p)
            new_ret_s.append(s_b_s)
        else:
            h32, = _rmsnorm(xr, norm_mix[i], [F32])
            h_s = seq_s(h32)
            pos = jnp.concatenate([jnp.tile(jnp.arange(lp, dtype=jnp.int32), bp),
                                   jnp.repeat(PAST_LEN + jnp.arange(ls, dtype=jnp.int32), bs)])
            cnt = jnp.stack([jnp.minimum(w, pos + 1).astype(F32) for w in POOL_WINDOWS])[:, :, None]
            zero_row = jnp.zeros((bs, 1, d), F32)
            kw = dict(tm=tm)
            xn = _pool_mixer(h32, hist_p(h32, POOL_HIST), cnt, pool_w_bf[j], pool_scale[j], xr, stride=1,
                             t0=0, nt=ntp, **kw)
            xr = _pool_mixer(h32, hist_s(jnp.concatenate([zero_row, state_pool[j]], axis=1), POOL_HIST), cnt,
                             pool_w_bf[j], pool_scale[j], xr, stride=bs, t0=ntp, nt=1, prev=xn, **kw)
            new_pool_p.append(tail_p(h32, pool_past))
            new_pool_s.append(jnp.concatenate([state_pool[j], h_s], axis=1)[:, -pool_past:])
            ((h,), ss) = _rmsnorm(xr, norm_ffn[i], [BF16]), None
        act, sg_p, sv_p = _ffn_up_act(h, ffn_w_up, ffn_conv_w, ffn_conv_b, layer=i, tm=lp, stride=1, t0=0, nt=bp,
                                      ss=ss)
        act, sg_s, sv_s = _ffn_up_act(h, ffn_w_up, ffn_conv_w, ffn_conv_b, layer=i, tm=tm, stride=bs, t0=ntp,
                                      nt=1, ss=ss, hist=hist_s(state_ffn_conv[i], ftaps - 1), prev=act)
        xr, h, ss = _matmul(act, w_down_bf, layer=i, res=xr, gain=norm_ple[i], name="ffn_down")
        new_ffn_p.append(jnp.concatenate([sg_p, sv_p], axis=-1))
        up_s = jnp.concatenate([sg_s, sv_s], axis=-1).reshape(ftaps - 1, bs, f2)
        new_ffn_s.append(jnp.swapaxes(up_s, 0, 1))
        xr = _matmul(h, ple_w_gate, layer=i, ss=ss, res=xr, p=pj[i], wp=w_proj_bf, name="ple")
    (out_p,) = _rmsnorm(xr, norm_final, [F32], 0, tp)
    (out_s,) = _rmsnorm(xr, norm_final, [F32], tp, ts)
    st = lambda lst: jnp.stack(lst)
    return (out_p.reshape(bp, lp, d), jnp.swapaxes(out_s.reshape(ls, bs, d), 0, 1),
            st(new_gdn_conv_p), st(new_gdn_p), st(new_ret_p), st(new_pool_p), st(new_ffn_p),
            st(new_gdn_conv_s), st(new_gdn_s), st(new_ret_s), st(new_pool_s), st(new_ffn_s))
```
